```python
import jax, jax.numpy as jnp
from jax import lax
import numpy as np

D_MODEL = 2048
BATCH = 1
SEQ = 8192
DEPTH = 2
DEC_BATCH = 8
DEC_SEQ = 16
PAST_LEN = 1024

CHUNK = 64
BAND_CHUNKS = 8
ATT_PAST_ROWS = BAND_CHUNKS * CHUNK
W_MIX = D_MODEL // 2
A_HEADS = 8
A_HEAD_DIM = W_MIX // A_HEADS
REL_CLIP = 128
R_HEADS = 8
R_DK = W_MIX // R_HEADS
R_DV = W_MIX // R_HEADS
G_HEADS = 8
G_DK = W_MIX // G_HEADS
G_DV = W_MIX // G_HEADS
N_BRANCH = 3
N_IN = 11 * W_MIX
D_FF = 5632
CONV_W = 3
ROPE_BASE = 10000.0
EPS = 1e-6

kernel_name = "hybrid_streaming_encoder_step"


def rms_norm(x, g):
    xf = x.astype(jnp.float32)
    y = xf * lax.rsqrt(jnp.mean(xf * xf, axis=-1, keepdims=True) + EPS)
    return (y * g.astype(jnp.float32)).astype(x.dtype)


def head_group_norm(x, g):
    xf = x.astype(jnp.float32)
    mu = jnp.mean(xf, axis=-1, keepdims=True)
    var = jnp.mean(jnp.square(xf - mu), axis=-1, keepdims=True)
    return ((xf - mu) * lax.rsqrt(var + EPS) * g.astype(jnp.float32)).astype(x.dtype)


def rope(x, pos):
    d = x.shape[-1]
    inv = ROPE_BASE ** (-jnp.arange(0, d, 2, dtype=jnp.float32) / d)
    ang = pos.astype(jnp.float32)[:, None] * inv[None, :]
    cos = jnp.cos(ang)[None, :, None, :]
    sin = jnp.sin(ang)[None, :, None, :]
    xf = x.astype(jnp.float32)
    x1, x2 = xf[..., : d // 2], xf[..., d // 2:]
    return jnp.concatenate([x1 * cos - x2 * sin, x1 * sin + x2 * cos], axis=-1).astype(x.dtype)


def chunk_band_attention(q, k, v, k_past, v_past, pos0, rel_bias):
    B, T, H, Dh = q.shape
    P = k_past.shape[1]
    C = min(CHUNK, T)
    n_blocks = T // C
    kk = jnp.concatenate([k_past, k], axis=1)
    vv = jnp.concatenate([v_past, v], axis=1)
    k_pos = pos0 - P + jnp.arange(P + T)
    scale = Dh ** -0.5

    def block(c):
        start = c * C
        qb = lax.dynamic_slice_in_dim(q, start, C, axis=1)
        kb = lax.dynamic_slice_in_dim(kk, start, P + C, axis=1)
        vb = lax.dynamic_slice_in_dim(vv, start, P + C, axis=1)
        qp = pos0 + start + jnp.arange(C)
        kp = lax.dynamic_slice_in_dim(k_pos, start, P + C)
        qc = qp // CHUNK
        kc = kp // CHUNK
        valid = (kp[None, :] >= 0) & (kc[None, :] <= qc[:, None]) & (kc[None, :] >= qc[:, None] - BAND_CHUNKS)
        rel = jnp.clip(qp[:, None] - kp[None, :], -REL_CLIP, REL_CLIP) + REL_CLIP
        bias = rel_bias[:, rel].astype(jnp.float32)
        s = jnp.einsum('bqhd,bkhd->bhqk', qb.astype(jnp.float32), kb.astype(jnp.float32)) * scale + bias[None]
        s = jnp.where(valid[None, None], s, -jnp.inf)
        p = jax.nn.softmax(s, axis=-1)
        return jnp.einsum('bhqk,bkhd->bqhd', p.astype(v.dtype), vb)

    out = lax.map(block, jnp.arange(n_blocks))
    return out.transpose(1, 0, 2, 3, 4).reshape(B, T, H * Dh)


def _to_blocks(x, C):
    B, T, H, d = x.shape
    return x.reshape(B, T // C, C, H, d).transpose(1, 0, 3, 2, 4)


def _from_blocks(o):
    n, B, H, C, d = o.shape
    return o.transpose(1, 0, 3, 2, 4).reshape(B, n * C, H, d)


def retention_chunked(q, k, v, s0):
    T, H = q.shape[1], q.shape[2]
    C = min(CHUNK, T)
    dt = q.dtype
    log_gamma = jnp.log1p(-(2.0 ** (-5.0 - jnp.arange(H, dtype=jnp.float32))))
    idx = jnp.arange(C, dtype=jnp.float32)
    diff = idx[:, None] - idx[None, :]
    decay_intra = jnp.where(diff[None] >= 0, jnp.exp(jnp.maximum(diff, 0.0)[None] * log_gamma[:, None, None]), 0.0).astype(dt)
    decay_q = jnp.exp((idx + 1.0)[None, :] * log_gamma[:, None]).astype(dt)
    decay_k = jnp.exp((C - 1.0 - idx)[None, :] * log_gamma[:, None]).astype(dt)
    decay_s = jnp.exp(C * log_gamma).astype(dt)

    def step(s, blk):
        qb, kb, vb = blk
        a = jnp.einsum('bhnk,bhmk->bhnm', qb, kb) * decay_intra[None]
        o = jnp.einsum('bhnm,bhmv->bhnv', a, vb) + jnp.einsum('bhnk,bhkv->bhnv', qb * decay_q[None, :, :, None], s)
        s_new = s * decay_s[None, :, None, None] + jnp.einsum('bhmk,bhmv->bhkv', kb * decay_k[None, :, :, None], vb)
        return s_new.astype(s.dtype), o

    s_fin, o = lax.scan(step, s0, (_to_blocks(q, C), _to_blocks(k, C), _to_blocks(v, C)))
    return _from_blocks(o), s_fin


def hgrn2_chunked(q, kk, logf, v, s0):
    T = q.shape[1]
    C = min(CHUNK, T)
    dt = q.dtype
    tril = jnp.tril(jnp.ones((C, C), dtype=bool))

    def step(s, blk):
        qb, kb, lf, vb = blk
        b = jnp.cumsum(lf.astype(jnp.float32), axis=2)
        pair = jnp.where(tril[None, None, :, :, None], b[:, :, :, None, :] - b[:, :, None, :, :], -jnp.inf)
        d = jnp.exp(pair).astype(dt)
        a = jnp.einsum('bhnmk,bhmk->bhnm', qb[:, :, :, None, :] * d, kb)
        o = jnp.einsum('bhnm,bhmv->bhnv', a, vb) + jnp.einsum('bhnk,bhkv->bhnv', qb * jnp.exp(b).astype(dt), s)
        b_last = b[:, :, -1:, :]
        s_new = jnp.exp(b_last[:, :, 0, :])[..., None].astype(dt) * s + jnp.einsum(
            'bhmk,bhmv->bhkv', kb * jnp.exp(b_last - b).astype(dt), vb)
        return s_new.astype(s.dtype), o

    s_fin, o = lax.scan(step, s0, (_to_blocks(q, C), _to_blocks(kk, C), _to_blocks(logf, C), _to_blocks(v, C)))
    return _from_blocks(o), s_fin


def trunk_layer(x, pos0, k_past, v_past, ret_s0, hgrn_s0, conv_s0,
                g_mix_pre, g_mix_post, g_ffn_pre, g_ffn_post, w_in, rel_bias, ret_norm_w,
                lb, hgrn_norm_w, w_gate, w_branch, w_out, w_up, w_val, conv_w, conv_b, w_down):
    B, T, _ = x.shape
    h = rms_norm(x, g_mix_pre)
    proj = h @ w_in
    qa, ka, va, qr, kr, vr, gr, qg, fg, ig, gg = jnp.split(proj, 11, axis=-1)

    ka_h = ka.reshape(B, T, A_HEADS, A_HEAD_DIM)
    va_h = va.reshape(B, T, A_HEADS, A_HEAD_DIM)
    y_a = chunk_band_attention(qa.reshape(B, T, A_HEADS, A_HEAD_DIM), ka_h, va_h, k_past, v_past, pos0, rel_bias)

    pos = pos0 + jnp.arange(T)
    qr_h = rope(qr.reshape(B, T, R_HEADS, R_DK), pos) * (R_DK ** -0.5)
    kr_h = rope(kr.reshape(B, T, R_HEADS, R_DK), pos)
    o_r, ret_s = retention_chunked(qr_h, kr_h, vr.reshape(B, T, R_HEADS, R_DV), ret_s0)
    y_r = jax.nn.silu(gr) * head_group_norm(o_r, ret_norm_w).reshape(B, T, W_MIX)

    lb_h = lb.reshape(G_HEADS, G_DK)
    logf = jnp.logaddexp(jnp.log(lb_h), jnp.log1p(-lb_h) + jax.nn.log_sigmoid(fg.reshape(B, T, G_HEADS, G_DK).astype(jnp.float32)))
    k_g = (1.0 - jnp.exp(logf)).astype(x.dtype)
    o_g, hgrn_s = hgrn2_chunked(qg.reshape(B, T, G_HEADS, G_DK), k_g, logf, ig.reshape(B, T, G_HEADS, G_DV), hgrn_s0)
    y_g = jax.nn.silu(gg) * rms_norm(o_g, hgrn_norm_w).reshape(B, T, W_MIX)

    branches = jnp.stack([y_a, y_r, y_g], axis=2)
    proj_b = jnp.einsum('btnw,nwd->btnd', branches, w_branch)
    gates = jax.nn.sigmoid(h @ w_gate).reshape(B, T, N_BRANCH, D_MODEL)
    merged = jnp.sum(gates * proj_b, axis=2)
    x = x + rms_norm(merged @ w_out, g_mix_post)

    h2 = rms_norm(x, g_ffn_pre)
    up = h2 @ w_up
    up_pad = jnp.concatenate([conv_s0, up], axis=1)
    conv = conv_b
    for j in range(CONV_W):
        conv = conv + up_pad[:, j:j + T] * conv_w[j]
    act = jax.nn.gelu(conv, approximate=False) * (h2 @ w_val)
    x = x + rms_norm(act @ w_down, g_ffn_post)
    new_conv = up_pad[:, up_pad.shape[1] - (CONV_W - 1):]
    return x, ka_h, va_h, ret_s, hgrn_s, new_conv


def setup_inputs(seed: int = 0) -> dict:
    key = jax.random.key(seed)
    ks = jax.random.split(key, 26)
    f32 = jnp.float32

    def nrm(k, shape, scale):
        return scale * jax.random.normal(k, shape, f32)

    att_past = min(ATT_PAST_ROWS, PAST_LEN)
    return {
        "x_prompt": nrm(ks[0], (BATCH, SEQ, D_MODEL), 1.0),
        "x_sample": nrm(ks[1], (DEC_BATCH, DEC_SEQ, D_MODEL), 1.0),
        "cache_attn_k": nrm(ks[2], (DEPTH, DEC_BATCH, att_past, A_HEADS, A_HEAD_DIM), 1.0),
        "cache_attn_v": nrm(ks[3], (DEPTH, DEC_BATCH, att_past, A_HEADS, A_HEAD_DIM), 1.0),
        "state_ret": nrm(ks[4], (DEPTH, DEC_BATCH, R_HEADS, R_DK, R_DV), 1.0),
        "state_hgrn": nrm(ks[5], (DEPTH, DEC_BATCH, G_HEADS, G_DK, G_DV), 1.0),
        "state_ffn_conv": nrm(ks[6], (DEPTH, DEC_BATCH, CONV_W - 1, D_FF), 1.0),
        "norm_mix_pre": 1.0 + nrm(ks[7], (DEPTH, D_MODEL), 0.1),
        "norm_mix_post": 1.0 + nrm(ks[8], (DEPTH, D_MODEL), 0.1),
        "norm_ffn_pre": 1.0 + nrm(ks[9], (DEPTH, D_MODEL), 0.1),
        "norm_ffn_post": 1.0 + nrm(ks[10], (DEPTH, D_MODEL), 0.1),
        "w_in": nrm(ks[11], (DEPTH, D_MODEL, N_IN), D_MODEL ** -0.5),
        "attn_rel_bias": nrm(ks[12], (DEPTH, A_HEADS, 2 * REL_CLIP + 1), 0.5),
        "ret_norm_w": 1.0 + nrm(ks[13], (DEPTH, R_DV), 0.1),
        "hgrn_lower_bound": nrm(ks[14], (DEPTH, W_MIX), 1.0),
        "hgrn_norm_w": 1.0 + nrm(ks[15], (DEPTH, G_DV), 0.1),
        "w_gate": nrm(ks[16], (DEPTH, D_MODEL, N_BRANCH * D_MODEL), D_MODEL ** -0.5),
        "w_branch": nrm(ks[17], (DEPTH, N_BRANCH, W_MIX, D_MODEL), W_MIX ** -0.5),
        "w_out": nrm(ks[18], (DEPTH, D_MODEL, D_MODEL), D_MODEL ** -0.5),
        "ffn_w_up": nrm(ks[19], (DEPTH, D_MODEL, D_FF), D_MODEL ** -0.5),
        "ffn_w_val": nrm(ks[20], (DEPTH, D_MODEL, D_FF), D_MODEL ** -0.5),
        "ffn_conv_w": nrm(ks[21], (DEPTH, CONV_W, D_FF), CONV_W ** -0.5),
        "ffn_conv_b": nrm(ks[22], (DEPTH, D_FF), 0.02),
        "ffn_w_down": nrm(ks[23], (DEPTH, D_FF, D_MODEL), D_FF ** -0.5),
    }


def reference(x_prompt, x_sample, cache_attn_k, cache_attn_v, state_ret, state_hgrn, state_ffn_conv,
              norm_mix_pre, norm_mix_post, norm_ffn_pre, norm_ffn_post, w_in, attn_rel_bias, ret_norm_w,
              hgrn_lower_bound, hgrn_norm_w, w_gate, w_branch, w_out, ffn_w_up, ffn_w_val, ffn_conv_w,
              ffn_conv_b, ffn_w_down):
    lb_cum = jnp.cumsum(jax.nn.softmax(hgrn_lower_bound.astype(jnp.float32), axis=0), axis=0)
    lb_layers = lb_cum - lb_cum[0:1]

    Bp, T = x_prompt.shape[0], x_prompt.shape[1]
    dt = x_prompt.dtype
    keep_p = min(ATT_PAST_ROWS, T)
    zk = jnp.zeros((Bp, ATT_PAST_ROWS, A_HEADS, A_HEAD_DIM), dt)
    zr = jnp.zeros((Bp, R_HEADS, R_DK, R_DV), dt)
    zg = jnp.zeros((Bp, G_HEADS, G_DK, G_DV), dt)
    zc = jnp.zeros((Bp, CONV_W - 1, D_FF), dt)

    xp, xs = x_prompt, x_sample
    pk, pv, pr, pg, pc = [], [], [], [], []
    sk, sv, sr, sg, sc = [], [], [], [], []
    for l in range(DEPTH):
        lw = (norm_mix_pre[l], norm_mix_post[l], norm_ffn_pre[l], norm_ffn_post[l], w_in[l], attn_rel_bias[l],
              ret_norm_w[l], lb_layers[l], hgrn_norm_w[l], w_gate[l], w_branch[l], w_out[l], ffn_w_up[l],
              ffn_w_val[l], ffn_conv_w[l], ffn_conv_b[l], ffn_w_down[l])
        xp, k_p, v_p, r_p, g_p, c_p = trunk_layer(xp, 0, zk, zk, zr, zg, zc, *lw)
        pk.append(k_p[:, T - keep_p:])
        pv.append(v_p[:, T - keep_p:])
        pr.append(r_p)
        pg.append(g_p)
        pc.append(c_p)
        xs, k_s, v_s, r_s, g_s, c_s = trunk_layer(xs, PAST_LEN, cache_attn_k[l], cache_attn_v[l], state_ret[l],
                                                  state_hgrn[l], state_ffn_conv[l], *lw)
        sk.append(k_s)
        sv.append(v_s)
        sr.append(r_s)
        sg.append(g_s)
        sc.append(c_s)

    return (xp, xs,
            jnp.stack(pk), jnp.stack(pv), jnp.stack(pr), jnp.stack(pg), jnp.stack(pc),
            jnp.stack(sk), jnp.stack(sv), jnp.stack(sr), jnp.stack(sg), jnp.stack(sc))
```

```python
import functools
import math

import numpy as np
import jax
import jax.numpy as jnp
from jax import lax
from jax.experimental import pallas as pl
from jax.experimental.pallas import tpu as pltpu

F32 = jnp.float32
BF16 = jnp.bfloat16

D_MODEL = 2048
CHUNK = 64
BAND_CHUNKS = 8
ATT_PAST_ROWS = BAND_CHUNKS * CHUNK
W_MIX = D_MODEL // 2
N_HEADS = 8
HEAD_DIM = W_MIX // N_HEADS
REL_CLIP = 128
N_BRANCH = 3
N_IN = 11 * W_MIX
D_FF = 5632
CONV_W = 3
ROPE_BASE = 10000.0
EPS = 1e-6
PAST_LEN = 1024

V7X_VMEM_BYTES = 64 * 1024 * 1024
VMEM_LIMIT = 56 * 1024 * 1024
SUBLANES = 8
LANES = 128
BF16_ROWS = 16

NT_DIMS = (((1,), (1,)), ((), ()))
TN_DIMS = (((0,), (0,)), ((), ()))


def _params(*sem):
    return pltpu.CompilerParams(dimension_semantics=sem, vmem_limit_bytes=VMEM_LIMIT)


def _rms(x, g):
    return x * lax.rsqrt(jnp.mean(x * x, axis=-1, keepdims=True) + EPS) * g


def _dot(a, b):
    return jnp.dot(a, b, preferred_element_type=F32)


def _dot_nt(a, b):
    return lax.dot_general(a, b, NT_DIMS, preferred_element_type=F32)


def _dot_tn(a, b):
    return lax.dot_general(a, b, TN_DIMS, preferred_element_type=F32)


def _silu(x):
    return x * jax.nn.sigmoid(x)


def _norm_proj_kernel(x_ref, g_ref, w_ref, o_ref, h_ref):
    @pl.when(pl.program_id(1) == 0)
    def _():
        h_ref[...] = _rms(x_ref[...], g_ref[...]).astype(BF16)

    o_ref[...] = _dot(h_ref[...], w_ref[...])


def norm_proj(x, g, w, tm, tn):
    rows, d = x.shape
    n = w.shape[1]
    return pl.pallas_call(
        _norm_proj_kernel,
        grid=(rows // tm, n // tn),
        in_specs=[
            pl.BlockSpec((tm, d), lambda i, j: (i, 0)),
            pl.BlockSpec((1, d), lambda i, j: (0, 0)),
            pl.BlockSpec((d, tn), lambda i, j: (0, j)),
        ],
        out_specs=pl.BlockSpec((tm, tn), lambda i, j: (i, j)),
        out_shape=jax.ShapeDtypeStruct((rows, n), F32),
        scratch_shapes=[pltpu.VMEM((tm, d), BF16)],
        compiler_params=_params("parallel", "arbitrary"),
        name="norm_proj",
    )(x, g, w)


def _attention_kernel(q_ref, kp_ref, ko_ref, vp_ref, vo_ref, bias_ref, o_ref, kw_ref, vw_ref,
                      *, chunk, past, group, pos0):
    g = pl.program_id(0)
    kw_ref[0:past, :] = kp_ref[...].astype(BF16)
    kw_ref[past:past + group, :] = ko_ref[...].astype(BF16)
    vw_ref[0:past, :] = vp_ref[...].astype(BF16)
    vw_ref[past:past + group, :] = vo_ref[...].astype(BF16)
    scale = HEAD_DIM ** -0.5
    bias = bias_ref[0]
    span = past + chunk
    for c in range(group // chunk):
        q = q_ref[c * chunk:(c + 1) * chunk, :].astype(BF16)
        k = kw_ref[c * chunk:c * chunk + span, :]
        v = vw_ref[c * chunk:c * chunk + span, :]
        s = _dot_nt(q, k) * scale + bias
        if pos0 < past:
            j = lax.broadcasted_iota(jnp.int32, (chunk, span), 1)
            s = jnp.where(j + (pos0 - past + c * chunk) + g * group >= 0, s, -jnp.inf)
        m = jnp.max(s, axis=-1, keepdims=True)
        p = jnp.exp(s - m)
        l = jnp.sum(p, axis=-1, keepdims=True)
        p = p / l
        o_ref[c * chunk:(c + 1) * chunk, :] = _dot(p.astype(BF16), v).astype(o_ref.dtype)


def attention(q_arr, q_map, kp_arr, kp_map, ko_arr, ko_map, vp_arr, vp_map, vo_arr, vo_map, bias,
              n_groups, group, chunk, past, pos0):
    rows = n_groups * group
    kernel = functools.partial(_attention_kernel, chunk=chunk, past=past, group=group, pos0=pos0)
    span = past + chunk
    return pl.pallas_call(
        kernel,
        grid=(n_groups, N_HEADS),
        in_specs=[
            pl.BlockSpec((group, HEAD_DIM), q_map),
            pl.BlockSpec((past, HEAD_DIM), kp_map),
            pl.BlockSpec((group, HEAD_DIM), ko_map),
            pl.BlockSpec((past, HEAD_DIM), vp_map),
            pl.BlockSpec((group, HEAD_DIM), vo_map),
            pl.BlockSpec((1, chunk, span), lambda g, h: (h, 0, 0)),
        ],
        out_specs=pl.BlockSpec((group, HEAD_DIM), lambda g, h: (g, h)),
        out_shape=jax.ShapeDtypeStruct((rows, W_MIX), BF16),
        scratch_shapes=[pltpu.VMEM((past + group, HEAD_DIM), BF16),
                        pltpu.VMEM((past + group, HEAD_DIM), BF16)],
        compiler_params=_params("parallel", "parallel"),
        name="attention",
    )(q_arr, kp_arr, ko_arr, vp_arr, vo_arr, bias)


def _retention_kernel(q_ref, k_ref, v_ref, g_ref, cos_ref, sin_ref, di_ref, dq_ref, dk_ref, ds_ref,
                      nw_ref, s0_ref, y_ref, sout_ref, s_ref):
    t = pl.program_id(1)

    @pl.when(t == 0)
    def _():
        s_ref[...] = s0_ref[0]

    cosf = cos_ref[...]
    sinf = sin_ref[...]
    nw = nw_ref[...]
    qscale = HEAD_DIM ** -0.5

    def head(h, carry):
        hs = pl.ds(pl.multiple_of(h * HEAD_DIM, HEAD_DIM), HEAD_DIM)
        q = q_ref[:, hs]
        k = k_ref[:, hs]
        q = (q * cosf + pltpu.roll(q, HEAD_DIM // 2, 1) * sinf) * qscale
        k = k * cosf + pltpu.roll(k, HEAD_DIM // 2, 1) * sinf
        v = v_ref[:, hs].astype(BF16)
        s = s_ref[h]
        a = _dot_nt(q.astype(BF16), k.astype(BF16)) * di_ref[h]
        o = _dot(a.astype(BF16), v) + _dot((q * dq_ref[h]).astype(BF16), s.astype(BF16))
        s_ref[h] = s * ds_ref[h] + _dot_tn((k * dk_ref[h]).astype(BF16), v)
        mu = jnp.mean(o, axis=-1, keepdims=True)
        oc = o - mu
        var = jnp.mean(oc * oc, axis=-1, keepdims=True)
        on = oc * lax.rsqrt(var + EPS) * nw
        y_ref[:, hs] = (_silu(g_ref[:, hs]) * on).astype(y_ref.dtype)
        return carry

    lax.fori_loop(0, N_HEADS, head, 0)

    @pl.when(t == pl.num_programs(1) - 1)
    def _():
        sout_ref[0] = s_ref[...]


def _retention_tables(chunk, pos):
    log_gamma = jnp.log1p(-(2.0 ** (-5.0 - jnp.arange(N_HEADS, dtype=F32))))
    idx = jnp.arange(chunk, dtype=F32)
    diff = idx[:, None] - idx[None, :]
    d_intra = jnp.where(diff[None] >= 0, jnp.exp(jnp.maximum(diff, 0.0)[None] * log_gamma[:, None, None]), 0.0)
    d_q = jnp.exp((idx + 1.0)[None, :] * log_gamma[:, None])
    d_k = jnp.exp((chunk - 1.0 - idx)[None, :] * log_gamma[:, None])
    d_s = jnp.exp(chunk * log_gamma)
    d_q = jnp.broadcast_to(d_q[:, :, None], (N_HEADS, chunk, HEAD_DIM))
    d_k = jnp.broadcast_to(d_k[:, :, None], (N_HEADS, chunk, HEAD_DIM))
    d_s = jnp.broadcast_to(d_s[:, None, None], (N_HEADS, HEAD_DIM, HEAD_DIM))
    inv = ROPE_BASE ** (-jnp.arange(0, HEAD_DIM, 2, dtype=F32) / HEAD_DIM)
    ang = pos.astype(F32)[:, None] * inv[None, :]
    cos, sin = jnp.cos(ang), jnp.sin(ang)
    cosf = jnp.concatenate([cos, cos], axis=-1)
    sinf = jnp.concatenate([-sin, sin], axis=-1)
    return d_intra, d_q, d_k, d_s, cosf, sinf


def retention(proj, col0, norm_w, s0, batch, seq, chunk, pos0):
    n_chunks = seq // chunk
    d_intra, d_q, d_k, d_s, cosf, sinf = _retention_tables(chunk, pos0 + jnp.arange(seq))

    def col(c):
        return pl.BlockSpec((chunk, W_MIX), lambda b, t: (b * n_chunks + t, col0 + c))

    def const(shape):
        return pl.BlockSpec(shape, lambda b, t: (0,) * len(shape))

    y, s_out = pl.pallas_call(
        _retention_kernel,
        grid=(batch, n_chunks),
        in_specs=[
            col(0), col(1), col(2), col(3),
            pl.BlockSpec((chunk, HEAD_DIM), lambda b, t: (t, 0)),
            pl.BlockSpec((chunk, HEAD_DIM), lambda b, t: (t, 0)),
            const((N_HEADS, chunk, chunk)),
            const((N_HEADS, chunk, HEAD_DIM)),
            const((N_HEADS, chunk, HEAD_DIM)),
            const((N_HEADS, HEAD_DIM, HEAD_DIM)),
            const((1, HEAD_DIM)),
            pl.BlockSpec((1, N_HEADS, HEAD_DIM, HEAD_DIM), lambda b, t: (b, 0, 0, 0)),
        ],
        out_specs=[
            pl.BlockSpec((chunk, W_MIX), lambda b, t: (b * n_chunks + t, 0)),
            pl.BlockSpec((1, N_HEADS, HEAD_DIM, HEAD_DIM), lambda b, t: (b, 0, 0, 0)),
        ],
        out_shape=[
            jax.ShapeDtypeStruct((batch * seq, W_MIX), BF16),
            jax.ShapeDtypeStruct((batch, N_HEADS, HEAD_DIM, HEAD_DIM), F32),
        ],
        scratch_shapes=[pltpu.VMEM((N_HEADS, HEAD_DIM, HEAD_DIM), F32)],
        compiler_params=_params("parallel", "arbitrary"),
        name="retention",
    )(proj, proj, proj, proj, cosf, sinf, d_intra, d_q, d_k, d_s, norm_w.reshape(1, HEAD_DIM), s0)
    return y, s_out


def _hgrn_level_matrices(chunk):
    levels = int(math.log2(chunk))
    n = np.arange(chunk)
    mats = [np.tril(np.ones((chunk, chunk), np.float32))]
    masks = []
    for lv in range(levels):
        s = chunk >> (lv + 1)
        pair = n // (2 * s)
        qside = (n // s) % 2 == 1
        bound = pair * 2 * s + s - 1
        j = n[None, :]
        c = np.where(qside[:, None], (j > bound[:, None]) & (j <= n[:, None]),
                     (j > n[:, None]) & (j <= bound[:, None]))
        mats.append(c.astype(np.float32))
        masks.append((qside[:, None] & ~qside[None, :] & (pair[:, None] == pair[None, :])).astype(np.float32))
    masks.append(np.eye(chunk, dtype=np.float32))
    return np.concatenate(mats, axis=0), np.stack(masks, axis=0), levels


def _hgrn_kernel(q_ref, f_ref, i_ref, g_ref, loglb_ref, log1mlb_ref, cmat_ref, mask_ref, nw_ref, s0_ref,
                 y_ref, sout_ref, st_ref, e_ref, kk_ref, *, chunk, levels):
    t = pl.program_id(1)

    @pl.when(t == 0)
    def _():
        for h in range(N_HEADS):
            st_ref[h] = s0_ref[0, h].T

    fg = f_ref[...]
    log_sig = jnp.minimum(fg, 0.0) - jnp.log1p(jnp.exp(-jnp.abs(fg)))
    a = loglb_ref[...]
    b = log1mlb_ref[...] + log_sig
    logf = jnp.maximum(a, b) + jnp.log1p(jnp.exp(-jnp.abs(a - b)))
    kk_ref[...] = 1.0 - jnp.exp(logf)
    hi = logf.astype(BF16)
    r1 = logf - hi.astype(F32)
    mid = r1.astype(BF16)
    lo = (r1 - mid.astype(F32)).astype(BF16)
    cmat = cmat_ref[...]
    e_ref[...] = _dot(cmat, hi) + _dot(cmat, mid) + _dot(cmat, lo)

    row = lax.broadcasted_iota(jnp.int32, (chunk, HEAD_DIM), 0)
    nw = nw_ref[...]

    def head(h, carry):
        hs = pl.ds(pl.multiple_of(h * HEAD_DIM, HEAD_DIM), HEAD_DIM)
        q = q_ref[:, hs]
        kk = kk_ref[:, hs]
        v = i_ref[:, hs].astype(BF16)
        bcum = e_ref[0:chunk, hs]
        amat = mask_ref[levels] * _dot_nt(q.astype(BF16), kk.astype(BF16))
        for lv in range(levels):
            s = chunk >> (lv + 1)
            qside = ((row // s) % 2) == 1
            x = (jnp.where(qside, q, kk) * jnp.exp(e_ref[(lv + 1) * chunk:(lv + 2) * chunk, hs])).astype(BF16)
            amat = amat + mask_ref[lv] * _dot_nt(x, x)
        st = st_ref[h]
        o = _dot(amat.astype(BF16), v) + _dot_nt((q * jnp.exp(bcum)).astype(BF16), st.astype(BF16))
        b_last = bcum[chunk - 1:chunk, :]
        kd = (kk * jnp.exp(b_last - bcum)).astype(BF16)
        st_ref[h] = st * jnp.exp(b_last) + _dot_tn(v, kd)
        on = _rms(o, nw)
        y_ref[:, hs] = (_silu(g_ref[:, hs]) * on).astype(y_ref.dtype)
        return carry

    lax.fori_loop(0, N_HEADS, head, 0)

    @pl.when(t == pl.num_programs(1) - 1)
    def _():
        for h in range(N_HEADS):
            sout_ref[0, h] = st_ref[h].T


def hgrn2(proj, col0, log_lb, log1m_lb, norm_w, s0, batch, seq, chunk):
    n_chunks = seq // chunk
    cmat_np, mask_np, levels = _hgrn_level_matrices(chunk)
    cmat = jnp.asarray(cmat_np, dtype=BF16)
    masks = jnp.asarray(mask_np)

    def col(c):
        return pl.BlockSpec((chunk, W_MIX), lambda b, t: (b * n_chunks + t, col0 + c))

    def const(shape):
        return pl.BlockSpec(shape, lambda b, t: (0,) * len(shape))

    kernel = functools.partial(_hgrn_kernel, chunk=chunk, levels=levels)
    y, s_out = pl.pallas_call(
        kernel,
        grid=(batch, n_chunks),
        in_specs=[
            col(0), col(1), col(2), col(3),
            const((1, W_MIX)), const((1, W_MIX)),
            const(((levels + 1) * chunk, chunk)),
            const((levels + 1, chunk, chunk)),
            const((1, HEAD_DIM)),
            pl.BlockSpec((1, N_HEADS, HEAD_DIM, HEAD_DIM), lambda b, t: (b, 0, 0, 0)),
        ],
        out_specs=[
            pl.BlockSpec((chunk, W_MIX), lambda b, t: (b * n_chunks + t, 0)),
            pl.BlockSpec((1, N_HEADS, HEAD_DIM, HEAD_DIM), lambda b, t: (b, 0, 0, 0)),
        ],
        out_shape=[
            jax.ShapeDtypeStruct((batch * seq, W_MIX), BF16),
            jax.ShapeDtypeStruct((batch, N_HEADS, HEAD_DIM, HEAD_DIM), F32),
        ],
        scratch_shapes=[
            pltpu.VMEM((N_HEADS, HEAD_DIM, HEAD_DIM), F32),
            pltpu.VMEM(((levels + 1) * chunk, W_MIX), F32),
            pltpu.VMEM((chunk, W_MIX), F32),
        ],
        compiler_params=_params("parallel", "arbitrary"),
        name="hgrn2",
    )(proj, proj, proj, proj, log_lb, log1m_lb, cmat, masks, norm_w.reshape(1, HEAD_DIM), s0)
    return y, s_out


def _merge_kernel(x_ref, gpre_ref, ya_ref, yr_ref, yg_ref, wg0_ref, wg1_ref, wg2_ref, wb_ref, wo_ref,
                  gpost_ref, o_ref, h_ref, acc_ref):
    j = pl.program_id(1)

    @pl.when(j == 0)
    def _():
        h_ref[...] = _rms(x_ref[...], gpre_ref[...]).astype(BF16)
        acc_ref[...] = jnp.zeros_like(acc_ref)

    h = h_ref[...]
    merged = None
    for n, (y_ref, wg_ref) in enumerate(((ya_ref, wg0_ref), (yr_ref, wg1_ref), (yg_ref, wg2_ref))):
        gate = jax.nn.sigmoid(_dot(h, wg_ref[...]))
        term = gate * _dot(y_ref[...], wb_ref[n])
        merged = term if merged is None else merged + term
    acc_ref[...] += _dot(merged.astype(BF16), wo_ref[...])

    @pl.when(j == pl.num_programs(1) - 1)
    def _():
        o_ref[...] = x_ref[...] + _rms(acc_ref[...], gpost_ref[...])


def merge(x, g_pre, ya, yr, yg, w_gate, w_branch, w_out, g_post, tm, tn):
    rows, d = x.shape
    nj = d // tn

    def gate_spec(n):
        return pl.BlockSpec((d, tn), lambda i, j: (0, n * nj + j))

    return pl.pallas_call(
        _merge_kernel,
        grid=(rows // tm, nj),
        in_specs=[
            pl.BlockSpec((tm, d), lambda i, j: (i, 0)),
            pl.BlockSpec((1, d), lambda i, j: (0, 0)),
            pl.BlockSpec((tm, W_MIX), lambda i, j: (i, 0)),
            pl.BlockSpec((tm, W_MIX), lambda i, j: (i, 0)),
            pl.BlockSpec((tm, W_MIX), lambda i, j: (i, 0)),
            gate_spec(0), gate_spec(1), gate_spec(2),
            pl.BlockSpec((N_BRANCH, W_MIX, tn), lambda i, j: (0, 0, j)),
            pl.BlockSpec((tn, d), lambda i, j: (j, 0)),
            pl.BlockSpec((1, d), lambda i, j: (0, 0)),
        ],
        out_specs=pl.BlockSpec((tm, d), lambda i, j: (i, 0)),
        out_shape=jax.ShapeDtypeStruct((rows, d), F32),
        scratch_shapes=[pltpu.VMEM((tm, d), BF16), pltpu.VMEM((tm, d), F32)],
        compiler_params=_params("parallel", "arbitrary"),
        name="merge",
    )(x, g_pre, ya, yr, yg, w_gate, w_gate, w_gate, w_branch, w_out, g_post)


HALO = BF16_ROWS


def _ffn_kernel(x_ref, xprev_ref, cst_ref, gpre_ref, wup_ref, wval_ref, cw_ref, cb_ref, wdn_ref, gpost_ref,
                o_ref, uplast_ref, h_ref, acc_ref, up_ref, *, tm, tiles_per_seq):
    i = pl.program_id(0)
    j = pl.program_id(1)

    @pl.when(j == 0)
    def _():
        g = gpre_ref[...]
        h_ref[0:HALO, :] = _rms(xprev_ref[...], g).astype(BF16)
        h_ref[HALO:HALO + tm, :] = _rms(x_ref[...], g).astype(BF16)
        acc_ref[...] = jnp.zeros_like(acc_ref)

    up_ref[...] = _dot(h_ref[...], wup_ref[...])

    @pl.when(i % tiles_per_seq == 0)
    def _():
        up_ref[HALO - (CONV_W - 1):HALO, :] = cst_ref[0]

    cw = cw_ref[...]
    conv = cb_ref[...] + up_ref[HALO - 2:HALO - 2 + tm, :] * cw[0:1, :]
    conv = conv + up_ref[HALO - 1:HALO - 1 + tm, :] * cw[1:2, :]
    conv = conv + up_ref[HALO:HALO + tm, :] * cw[2:3, :]
    val = _dot(h_ref[HALO:HALO + tm, :], wval_ref[...])
    act = 0.5 * conv * (1.0 + lax.erf(conv * math.sqrt(0.5))) * val
    acc_ref[...] += _dot(act.astype(BF16), wdn_ref[...])
    uplast_ref[...] = up_ref[HALO + tm - SUBLANES:HALO + tm, :]

    @pl.when(j == pl.num_programs(1) - 1)
    def _():
        o_ref[...] = x_ref[...] + _rms(acc_ref[...], gpost_ref[...])


def ffn(x, conv_state, g_pre, w_up, w_val, conv_w, conv_b, w_down, g_post, batch, seq, tm, tf):
    rows, d = x.shape
    dff = w_up.shape[1]
    tps = seq // tm
    n_tiles = rows // tm
    halo_blocks = tm // HALO
    kernel = functools.partial(_ffn_kernel, tm=tm, tiles_per_seq=tps)
    out, uplast = pl.pallas_call(
        kernel,
        grid=(n_tiles, dff // tf),
        in_specs=[
            pl.BlockSpec((tm, d), lambda i, j: (i, 0)),
            pl.BlockSpec((HALO, d), lambda i, j: (jnp.maximum(i * halo_blocks - 1, 0), 0)),
            pl.BlockSpec((1, CONV_W - 1, tf), lambda i, j: (i // tps, 0, j)),
            pl.BlockSpec((1, d), lambda i, j: (0, 0)),
            pl.BlockSpec((d, tf), lambda i, j: (0, j)),
            pl.BlockSpec((d, tf), lambda i, j: (0, j)),
            pl.BlockSpec((CONV_W, tf), lambda i, j: (0, j)),
            pl.BlockSpec((1, tf), lambda i, j: (0, j)),
            pl.BlockSpec((tf, d), lambda i, j: (j, 0)),
            pl.BlockSpec((1, d), lambda i, j: (0, 0)),
        ],
        out_specs=[
            pl.BlockSpec((tm, d), lambda i, j: (i, 0)),
            pl.BlockSpec((SUBLANES, tf), lambda i, j: (i, j)),
        ],
        out_shape=[
            jax.ShapeDtypeStruct((rows, d), F32),
            jax.ShapeDtypeStruct((n_tiles * SUBLANES, dff), F32),
        ],
        scratch_shapes=[
            pltpu.VMEM((HALO + tm, d), BF16),
            pltpu.VMEM((tm, d), F32),
            pltpu.VMEM((HALO + tm, tf), F32),
        ],
        compiler_params=_params("parallel", "arbitrary"),
        name="ffn",
    )(x, x, conv_state, g_pre, w_up, w_val, conv_w, conv_b, w_down, g_post)
    new_conv = uplast.reshape(batch, tps, SUBLANES, dff)[:, -1, SUBLANES - (CONV_W - 1):, :]
    return out, new_conv


def _rel_bias_band(rel_bias, chunk, past):
    i = jnp.arange(chunk)[:, None]
    j = jnp.arange(past + chunk)[None, :]
    rel = jnp.clip(i - j + past, -REL_CLIP, REL_CLIP) + REL_CLIP
    return rel_bias[:, rel].astype(F32)


def _layer(x, batch, seq, pos0, k_past, v_past, ret_s0, hgrn_s0, conv_s0, lw, tiles):
    (g_mix_pre, g_mix_post, g_ffn_pre, g_ffn_post, w_in, rel_bias, ret_norm_w, log_lb, log1m_lb,
     hgrn_norm_w, w_gate, w_branch, w_out, w_up, w_val, conv_w, conv_b, w_down) = lw
    rows = batch * seq
    proj = norm_proj(x, g_mix_pre, w_in, tiles["tm_proj"], tiles["tn_proj"])

    chunk = min(CHUNK, seq)
    past = ATT_PAST_ROWS
    bias = _rel_bias_band(rel_bias, chunk, past)
    hb = N_HEADS
    if k_past is None:
        group = past
        n_groups = seq // group
        y_a = attention(
            proj, lambda g, h: (g, h),
            proj, lambda g, h: (jnp.maximum(g - 1, 0), hb + h),
            proj, lambda g, h: (g, hb + h),
            proj, lambda g, h: (jnp.maximum(g - 1, 0), 2 * hb + h),
            proj, lambda g, h: (g, 2 * hb + h),
            bias, n_groups, group, chunk, past, pos0)
    else:
        group = seq
        kp = k_past.reshape(batch * past, W_MIX)
        vp = v_past.reshape(batch * past, W_MIX)
        y_a = attention(
            proj, lambda g, h: (g, h),
            kp, lambda g, h: (g, h),
            proj, lambda g, h: (g, hb + h),
            vp, lambda g, h: (g, h),
            proj, lambda g, h: (g, 2 * hb + h),
            bias, batch, group, chunk, past, pos0)

    y_r, ret_s = retention(proj, 3, ret_norm_w, ret_s0, batch, seq, tiles["ret_chunk"], pos0)
    y_g, hgrn_s = hgrn2(proj, 7, log_lb, log1m_lb, hgrn_norm_w, hgrn_s0, batch, seq, chunk)

    x = merge(x, g_mix_pre, y_a, y_r, y_g, w_gate, w_branch, w_out, g_mix_post,
              tiles["tm_merge"], tiles["tn_merge"])
    x, new_conv = ffn(x, conv_s0, g_ffn_pre, w_up, w_val, conv_w, conv_b, w_down, g_ffn_post,
                      batch, seq, tiles["tm_ffn"], tiles["tf_ffn"])
    k_new = proj[:, W_MIX:2 * W_MIX].reshape(batch, seq, N_HEADS, HEAD_DIM)
    v_new = proj[:, 2 * W_MIX:3 * W_MIX].reshape(batch, seq, N_HEADS, HEAD_DIM)
    return x, k_new, v_new, ret_s, hgrn_s, new_conv


PROMPT_TILES = dict(tm_proj=512, tn_proj=1024, ret_chunk=256, tm_merge=512, tn_merge=256, tm_ffn=512, tf_ffn=512)


def _sample_tiles(rows, seq):
    return dict(tm_proj=rows, tn_proj=1024, ret_chunk=seq, tm_merge=rows, tn_merge=256, tm_ffn=seq, tf_ffn=512)


def kernel(x_prompt, x_sample, cache_attn_k, cache_attn_v, state_ret, state_hgrn, state_ffn_conv,
           norm_mix_pre, norm_mix_post, norm_ffn_pre, norm_ffn_post, w_in, attn_rel_bias, ret_norm_w,
           hgrn_lower_bound, hgrn_norm_w, w_gate, w_branch, w_out, ffn_w_up, ffn_w_val, ffn_conv_w,
           ffn_conv_b, ffn_w_down):
    depth = w_in.shape[0]
    bp, tp, d = x_prompt.shape
    bs, ts, _ = x_sample.shape

    lb_cum = jnp.cumsum(jax.nn.softmax(hgrn_lower_bound.astype(F32), axis=0), axis=0)
    lb_layers = lb_cum - lb_cum[0:1]
    log_lb = jnp.log(lb_layers)
    log1m_lb = jnp.log1p(-lb_layers)

    w_in_b, w_gate_b, w_branch_b, w_out_b = (w.astype(BF16) for w in (w_in, w_gate, w_branch, w_out))
    w_up_b, w_val_b, w_down_b = (w.astype(BF16) for w in (ffn_w_up, ffn_w_val, ffn_w_down))

    zr = jnp.zeros((bp, N_HEADS, HEAD_DIM, HEAD_DIM), F32)
    zc = jnp.zeros((bp, CONV_W - 1, D_FF), F32)
    keep_p = min(ATT_PAST_ROWS, tp)

    xp = x_prompt.reshape(bp * tp, d)
    xs = x_sample.reshape(bs * ts, d)
    outs_p = [[] for _ in range(5)]
    outs_s = [[] for _ in range(5)]
    for l in range(depth):
        lw = (norm_mix_pre[l][None], norm_mix_post[l][None], norm_ffn_pre[l][None], norm_ffn_post[l][None],
              w_in_b[l], attn_rel_bias[l], ret_norm_w[l], log_lb[l][None], log1m_lb[l][None], hgrn_norm_w[l],
              w_gate_b[l], w_branch_b[l], w_out_b[l], w_up_b[l], w_val_b[l], ffn_conv_w[l], ffn_conv_b[l][None],
              w_down_b[l])
        xp, k_p, v_p, r_p, g_p, c_p = _layer(xp, bp, tp, 0, None, None, zr, zr, zc, lw, PROMPT_TILES)
        for lst, val in zip(outs_p, (k_p[:, tp - keep_p:], v_p[:, tp - keep_p:], r_p, g_p, c_p)):
            lst.append(val)
        xs, k_s, v_s, r_s, g_s, c_s = _layer(xs, bs, ts, PAST_LEN, cache_attn_k[l], cache_attn_v[l],
                                             state_ret[l], state_hgrn[l], state_ffn_conv[l], lw,
                                             _sample_tiles(bs * ts, ts))
        for lst, val in zip(outs_s, (k_s, v_s, r_s, g_s, c_s)):
            lst.append(val)

    return (xp.reshape(bp, tp, d), xs.reshape(bs, ts, d),
            *(jnp.stack(o) for o in outs_p), *(jnp.stack(o) for o in outs_s))
```

```python
import functools
import math

import numpy as np
import jax
import jax.numpy as jnp
from jax import lax
from jax.experimental import pallas as pl
from jax.experimental.pallas import tpu as pltpu

F32 = jnp.float32
BF16 = jnp.bfloat16

D_MODEL = 2048
CHUNK = 64
BAND_CHUNKS = 8
ATT_PAST_ROWS = BAND_CHUNKS * CHUNK
W_MIX = D_MODEL // 2
N_HEADS = 8
HEAD_DIM = W_MIX // N_HEADS
REL_CLIP = 128
N_BRANCH = 3
N_IN = 11 * W_MIX
D_FF = 5632
CONV_W = 3
ROPE_BASE = 10000.0
EPS = 1e-6
PAST_LEN = 1024

COL_QA, COL_KA, COL_VA, COL_QR, COL_KR, COL_VR, COL_GR, COL_QG, COL_FG, COL_IG, COL_GG = range(11)

V7X_VMEM_BYTES = 64 * 1024 * 1024
VMEM_LIMIT = V7X_VMEM_BYTES - 6 * 1024 * 1024
SUBLANES = 8
LANES = 128
BF16_ROWS = 16
ROW_SUB = 512

NT_DIMS = (((1,), (1,)), ((), ()))
TN_DIMS = (((0,), (0,)), ((), ()))


def _params(*sem):
    return pltpu.CompilerParams(dimension_semantics=sem, vmem_limit_bytes=VMEM_LIMIT)


def _rms(x, g):
    return x * lax.rsqrt(jnp.mean(x * x, axis=-1, keepdims=True) + EPS) * g


def _dot(a, b):
    return jnp.dot(a, b, preferred_element_type=F32)


def _dot_nt(a, b):
    return lax.dot_general(a, b, NT_DIMS, preferred_element_type=F32)


def _dot_tn(a, b):
    return lax.dot_general(a, b, TN_DIMS, preferred_element_type=F32)


def _silu(x):
    return x * jax.nn.sigmoid(x)


def _norm_proj_kernel(x_ref, g_ref, w_ref, o_ref, *rest, f32_cols):
    f32_refs, h_ref = rest[:-1], rest[-1]
    j = pl.program_id(1)

    @pl.when(j == 0)
    def _():
        h_ref[...] = _rms(x_ref[...], g_ref[...]).astype(BF16)

    acc = _dot(h_ref[...], w_ref[...])
    o_ref[...] = acc.astype(o_ref.dtype)
    for col, ref in zip(f32_cols, f32_refs):
        @pl.when(j == col)
        def _():
            ref[...] = acc


def norm_proj(x, g, w, layer, tm, tn, f32_cols=(), row_block0=0, n_row_blocks=None, col_block0=0,
              n_col_blocks=None, out_dtype=BF16):
    d = x.shape[1]
    n_row_blocks = x.shape[0] // tm - row_block0 if n_row_blocks is None else n_row_blocks
    n_col_blocks = w.shape[2] // tn - col_block0 if n_col_blocks is None else n_col_blocks
    rows, n = n_row_blocks * tm, n_col_blocks * tn
    kernel = functools.partial(_norm_proj_kernel, f32_cols=tuple(f32_cols))
    outs = pl.pallas_call(
        kernel,
        grid=(n_row_blocks, n_col_blocks),
        in_specs=[
            pl.BlockSpec((tm, d), lambda i, j: (row_block0 + i, 0)),
            pl.BlockSpec((1, d), lambda i, j: (0, 0)),
            pl.BlockSpec((None, d, tn), lambda i, j: (layer, 0, col_block0 + j)),
        ],
        out_specs=[pl.BlockSpec((tm, tn), lambda i, j: (i, j))]
        + [pl.BlockSpec((tm, tn), lambda i, j: (i, 0)) for _ in f32_cols],
        out_shape=[jax.ShapeDtypeStruct((rows, n), out_dtype)]
        + [jax.ShapeDtypeStruct((rows, tn), F32) for _ in f32_cols],
        scratch_shapes=[pltpu.VMEM((tm, d), BF16)],
        compiler_params=_params("parallel", "arbitrary"),
        name="norm_proj",
    )(x, g, w)
    return outs


def _attention_kernel(q_ref, kp_ref, ko_ref, vp_ref, vo_ref, *rest, past, group, sub):
    n_sub = group // sub
    bias_refs, (o_ref, kw_ref, vw_ref) = rest[:n_sub], rest[n_sub:]
    kw_ref[0:past, :] = kp_ref[...].astype(BF16)
    kw_ref[past:past + group, :] = ko_ref[...].astype(BF16)
    vw_ref[0:past, 0:HEAD_DIM] = vp_ref[...].astype(BF16)
    vw_ref[past:past + group, 0:HEAD_DIM] = vo_ref[...].astype(BF16)
    vw_ref[:, HEAD_DIM:2 * HEAD_DIM] = jnp.ones((past + group, HEAD_DIM), BF16)
    scale = HEAD_DIM ** -0.5
    span = past + sub
    for s in range(n_sub):
        q = (q_ref[s * sub:(s + 1) * sub, :].astype(F32) * scale).astype(BF16)
        k = kw_ref[s * sub:s * sub + span, :]
        v = vw_ref[s * sub:s * sub + span, :]
        sc = _dot_nt(q, k) + bias_refs[s][...]
        p = jnp.exp(sc - jnp.max(sc, axis=-1, keepdims=True))
        ov = _dot(p.astype(BF16), v)
        o_ref[s * sub:(s + 1) * sub, :] = (ov[:, 0:HEAD_DIM] / ov[:, HEAD_DIM:2 * HEAD_DIM]).astype(o_ref.dtype)


def _toeplitz_rows(z, n_rows, width):
    m = z.shape[-1]
    flat = jnp.tile(z, (1, n_rows))[:, :n_rows * (m - 1)]
    return flat.reshape(z.shape[0], n_rows, m - 1)[:, :, :width]


def _attention_tables(rel_bias, sub, chunk, past, pos0, group):
    width = past + sub
    m = np.arange(width + sub - 1)
    idx = np.clip(past + (sub - 1) - m, -REL_CLIP, REL_CLIP) + REL_CLIP
    n_hi = int(np.sum(idx == idx[0]))
    n_lo = int(np.sum(idx == idx[-1]))
    mid = idx[n_hi:idx.size - n_lo]
    assert mid.size and np.all(np.diff(mid) == -1)
    v = jnp.concatenate([jnp.broadcast_to(rel_bias[:, idx[0]:idx[0] + 1], (N_HEADS, n_hi)),
                         jnp.flip(rel_bias[:, int(mid[-1]):int(mid[0]) + 1], axis=1),
                         jnp.broadcast_to(rel_bias[:, idx[-1]:idx[-1] + 1], (N_HEADS, n_lo))], axis=1)
    z = jnp.concatenate([v[:, sub - 1:], jnp.zeros((N_HEADS, 1), F32), v[:, :sub - 1]], axis=1)
    toe = _toeplitz_rows(z, sub, width)
    i = np.arange(sub)[:, None]
    j = np.arange(width)[None, :]
    band = (j >= (i // chunk) * chunk) & (j < (i // chunk) * chunk + past + chunk)
    masks = [band]
    if pos0 < past:
        for s in range(group // sub):
            masks.append(band & (j + (pos0 - past + s * sub) >= 0))
    neg = np.where(np.stack(masks), 0.0, -np.inf).astype(np.float32)
    return toe[None] + jnp.asarray(neg)[:, None]


def attention(q_arr, q_map, kp_arr, kp_block, kp_map, ko_arr, ko_map, vp_arr, vp_map, vo_arr, vo_map, tables,
              n_groups, group, sub, past):
    rows = n_groups * group
    n_sub = group // sub
    n_var = tables.shape[0]
    kernel = functools.partial(_attention_kernel, past=past, group=group, sub=sub)
    width = past + sub

    def table_spec(s):
        if n_var == 1:
            return pl.BlockSpec((None, None, sub, width), lambda h, g: (0, h, 0, 0))
        return pl.BlockSpec((None, None, sub, width), lambda h, g: (jnp.where(g == 0, 1 + s, 0), h, 0, 0))

    return pl.pallas_call(
        kernel,
        grid=(N_HEADS, n_groups),
        in_specs=[
            pl.BlockSpec((group, HEAD_DIM), q_map),
            pl.BlockSpec(kp_block, kp_map),
            pl.BlockSpec((group, HEAD_DIM), ko_map),
            pl.BlockSpec(kp_block, vp_map),
            pl.BlockSpec((group, HEAD_DIM), vo_map),
        ] + [table_spec(s) for s in range(n_sub)],
        out_specs=pl.BlockSpec((group, HEAD_DIM), lambda h, g: (g, h)),
        out_shape=jax.ShapeDtypeStruct((rows, W_MIX), BF16),
        scratch_shapes=[pltpu.VMEM((past + group, HEAD_DIM), BF16),
                        pltpu.VMEM((past + group, 2 * HEAD_DIM), BF16)],
        compiler_params=_params("parallel", "arbitrary"),
        name="attention",
    )(q_arr, kp_arr, ko_arr, vp_arr, vo_arr, *([tables] * n_sub))


def _retention_kernel(q_ref, k_ref, v_ref, g_ref, cos_ref, sin_ref, di_ref, dq_ref, dk_ref, ds_ref,
                      nw_ref, s0_ref, y_ref, sout_ref, s_ref):
    t = pl.program_id(1)

    @pl.when(t == 0)
    def _():
        s_ref[...] = s0_ref[...]

    cosf = cos_ref[...]
    sinf = sin_ref[...]
    nw = nw_ref[...]
    qscale = HEAD_DIM ** -0.5

    def head(h, carry):
        hs = pl.ds(pl.multiple_of(h * HEAD_DIM, HEAD_DIM), HEAD_DIM)
        q = q_ref[:, hs].astype(F32)
        k = k_ref[:, hs].astype(F32)
        q = (q * cosf + pltpu.roll(q, HEAD_DIM // 2, 1) * sinf) * qscale
        k = k * cosf + pltpu.roll(k, HEAD_DIM // 2, 1) * sinf
        v = v_ref[:, hs]
        s = s_ref[h]
        a = _dot_nt(q.astype(BF16), k.astype(BF16)) * di_ref[h]
        o = _dot(a.astype(BF16), v) + _dot((q * dq_ref[h]).astype(BF16), s.astype(BF16))
        s_ref[h] = s * ds_ref[h] + _dot_tn((k * dk_ref[h]).astype(BF16), v)
        mu = jnp.mean(o, axis=-1, keepdims=True)
        oc = o - mu
        var = jnp.mean(oc * oc, axis=-1, keepdims=True)
        on = oc * lax.rsqrt(var + EPS) * nw
        y_ref[:, hs] = (_silu(g_ref[:, hs].astype(F32)) * on).astype(y_ref.dtype)
        return carry

    lax.fori_loop(0, N_HEADS, head, 0)

    @pl.when(t == pl.num_programs(1) - 1)
    def _():
        sout_ref[...] = s_ref[...]


def _retention_tables(chunk, pos):
    log_gamma = jnp.log1p(-(2.0 ** (-5.0 - jnp.arange(N_HEADS, dtype=F32))))
    idx = jnp.arange(chunk, dtype=F32)
    diff = idx[:, None] - idx[None, :]
    d_intra = jnp.where(diff[None] >= 0, jnp.exp(jnp.maximum(diff, 0.0)[None] * log_gamma[:, None, None]), 0.0)
    d_q = jnp.exp((idx + 1.0)[None, :] * log_gamma[:, None])
    d_k = jnp.exp((chunk - 1.0 - idx)[None, :] * log_gamma[:, None])
    d_s = jnp.exp(chunk * log_gamma)
    d_q = jnp.broadcast_to(d_q[:, :, None], (N_HEADS, chunk, HEAD_DIM))
    d_k = jnp.broadcast_to(d_k[:, :, None], (N_HEADS, chunk, HEAD_DIM))
    d_s = jnp.broadcast_to(d_s[:, None, None], (N_HEADS, HEAD_DIM, HEAD_DIM))
    inv = ROPE_BASE ** (-jnp.arange(0, HEAD_DIM, 2, dtype=F32) / HEAD_DIM)
    ang = pos.astype(F32)[:, None] * inv[None, :]
    cos, sin = jnp.cos(ang), jnp.sin(ang)
    cosf = jnp.concatenate([cos, cos], axis=-1)
    sinf = jnp.concatenate([-sin, sin], axis=-1)
    return d_intra, d_q, d_k, d_s, cosf, sinf


def _state_spec(layer):
    return pl.BlockSpec((None, None, N_HEADS, HEAD_DIM, HEAD_DIM), lambda b, t: (layer, b, 0, 0, 0))


def retention(proj, norm_w, s0, s0_layer, batch, seq, chunk, pos0):
    n_chunks = seq // chunk
    d_intra, d_q, d_k, d_s, cosf, sinf = _retention_tables(chunk, pos0 + jnp.arange(seq))

    def col(c):
        return pl.BlockSpec((chunk, W_MIX), lambda b, t: (b * n_chunks + t, c))

    def const(shape):
        return pl.BlockSpec(shape, lambda b, t: (0,) * len(shape))

    y, s_out = pl.pallas_call(
        _retention_kernel,
        grid=(batch, n_chunks),
        in_specs=[
            col(COL_QR), col(COL_KR), col(COL_VR), col(COL_GR),
            pl.BlockSpec((chunk, HEAD_DIM), lambda b, t: (t, 0)),
            pl.BlockSpec((chunk, HEAD_DIM), lambda b, t: (t, 0)),
            const((N_HEADS, chunk, chunk)),
            const((N_HEADS, chunk, HEAD_DIM)),
            const((N_HEADS, chunk, HEAD_DIM)),
            const((N_HEADS, HEAD_DIM, HEAD_DIM)),
            const((1, HEAD_DIM)),
            _state_spec(s0_layer),
        ],
        out_specs=[
            pl.BlockSpec((chunk, W_MIX), lambda b, t: (b * n_chunks + t, 0)),
            pl.BlockSpec((None, N_HEADS, HEAD_DIM, HEAD_DIM), lambda b, t: (b, 0, 0, 0)),
        ],
        out_shape=[
            jax.ShapeDtypeStruct((batch * seq, W_MIX), BF16),
            jax.ShapeDtypeStruct((batch, N_HEADS, HEAD_DIM, HEAD_DIM), F32),
        ],
        scratch_shapes=[pltpu.VMEM((N_HEADS, HEAD_DIM, HEAD_DIM), F32)],
        compiler_params=_params("parallel", "arbitrary"),
        name="retention",
    )(proj, proj, proj, proj, cosf, sinf, d_intra, d_q, d_k, d_s, norm_w.reshape(1, HEAD_DIM), s0)
    return y, s_out


def _hgrn_level_matrices(chunk):
    levels = int(math.log2(chunk))
    n = np.arange(chunk)
    mats = [np.tril(np.ones((chunk, chunk), np.float32))]
    masks = []
    for lv in range(levels):
        s = chunk >> (lv + 1)
        pair = n // (2 * s)
        qside = (n // s) % 2 == 1
        bound = pair * 2 * s + s - 1
        j = n[None, :]
        c = np.where(qside[:, None], (j > bound[:, None]) & (j <= n[:, None]),
                     (j > n[:, None]) & (j <= bound[:, None]))
        mats.append(c.astype(np.float32))
        masks.append((qside[:, None] & ~qside[None, :] & (pair[:, None] == pair[None, :])).astype(np.float32))
    masks.append(np.eye(chunk, dtype=np.float32))
    return np.concatenate(mats, axis=0), np.stack(masks, axis=0), levels


def _hgrn_kernel(q_ref, f_ref, i_ref, g_ref, loglb_ref, log1mlb_ref, cmat_ref, mask_ref, nw_ref, s0_ref,
                 y_ref, sout_ref, st_ref, e_ref, x_ref, *, chunk, levels):
    t = pl.program_id(1)

    @pl.when(t == 0)
    def _():
        for h in range(N_HEADS):
            st_ref[h] = s0_ref[h].T

    fg = f_ref[...]
    log_sig = jnp.minimum(fg, 0.0) - jnp.log1p(jnp.exp(-jnp.abs(fg)))
    a = loglb_ref[...]
    b = log1mlb_ref[...] + log_sig
    logf = jnp.maximum(a, b) + jnp.log1p(jnp.exp(-jnp.abs(a - b)))
    kk = 1.0 - jnp.exp(logf)
    hi = logf.astype(BF16)
    r1 = logf - hi.astype(F32)
    mid = r1.astype(BF16)
    lo = (r1 - mid.astype(F32)).astype(BF16)
    cmat = cmat_ref[...]
    e_ref[...] = _dot(cmat, hi) + _dot(cmat, mid) + _dot(cmat, lo)

    q = q_ref[...].astype(F32)
    row = lax.broadcasted_iota(jnp.int32, (chunk, W_MIX), 0)
    x_ref[levels] = q.astype(BF16)
    x_ref[levels + 1] = kk.astype(BF16)
    for lv in range(levels):
        s = chunk >> (lv + 1)
        qside = ((row // s) % 2) == 1
        x_ref[lv] = (jnp.where(qside, q, kk) * jnp.exp(e_ref[(lv + 1) * chunk:(lv + 2) * chunk, :])).astype(BF16)
    bcum = e_ref[0:chunk, :]
    b_last = bcum[chunk - 1:chunk, :]
    x_ref[levels + 2] = (q * jnp.exp(bcum)).astype(BF16)
    x_ref[levels + 3] = (kk * jnp.exp(b_last - bcum)).astype(BF16)
    decay = jnp.exp(b_last)
    nw = nw_ref[...]

    for h in range(N_HEADS):
        hs = slice(h * HEAD_DIM, (h + 1) * HEAD_DIM)
        amat = mask_ref[levels] * _dot_nt(x_ref[levels, :, hs], x_ref[levels + 1, :, hs])
        for lv in range(levels):
            x = x_ref[lv, :, hs]
            amat = amat + mask_ref[lv] * _dot_nt(x, x)
        v = i_ref[:, hs]
        st = st_ref[h]
        o = _dot(amat.astype(BF16), v) + _dot_nt(x_ref[levels + 2, :, hs], st.astype(BF16))
        st_ref[h] = st * decay[:, hs] + _dot_tn(v, x_ref[levels + 3, :, hs])
        y_ref[:, hs] = (_silu(g_ref[:, hs].astype(F32)) * _rms(o, nw)).astype(y_ref.dtype)

    @pl.when(t == pl.num_programs(1) - 1)
    def _():
        for h in range(N_HEADS):
            sout_ref[h] = st_ref[h].T


def hgrn2(proj, fg, log_lb, log1m_lb, norm_w, s0, s0_layer, batch, seq, chunk):
    n_chunks = seq // chunk
    cmat_np, mask_np, levels = _hgrn_level_matrices(chunk)
    cmat = jnp.asarray(cmat_np, dtype=BF16)
    masks = jnp.asarray(mask_np)

    def col(c):
        return pl.BlockSpec((chunk, W_MIX), lambda b, t: (b * n_chunks + t, c))

    def const(shape):
        return pl.BlockSpec(shape, lambda b, t: (0,) * len(shape))

    kernel = functools.partial(_hgrn_kernel, chunk=chunk, levels=levels)
    y, s_out = pl.pallas_call(
        kernel,
        grid=(batch, n_chunks),
        in_specs=[
            col(COL_QG), col(0), col(COL_IG), col(COL_GG),
            const((1, W_MIX)), const((1, W_MIX)),
            const(((levels + 1) * chunk, chunk)),
            const((levels + 1, chunk, chunk)),
            const((1, HEAD_DIM)),
            _state_spec(s0_layer),
        ],
        out_specs=[
            pl.BlockSpec((chunk, W_MIX), lambda b, t: (b * n_chunks + t, 0)),
            pl.BlockSpec((None, N_HEADS, HEAD_DIM, HEAD_DIM), lambda b, t: (b, 0, 0, 0)),
        ],
        out_shape=[
            jax.ShapeDtypeStruct((batch * seq, W_MIX), BF16),
            jax.ShapeDtypeStruct((batch, N_HEADS, HEAD_DIM, HEAD_DIM), F32),
        ],
        scratch_shapes=[
            pltpu.VMEM((N_HEADS, HEAD_DIM, HEAD_DIM), F32),
            pltpu.VMEM(((levels + 1) * chunk, W_MIX), F32),
            pltpu.VMEM((levels + 4, chunk, W_MIX), BF16),
        ],
        compiler_params=_params("parallel", "arbitrary"),
        name="hgrn2",
    )(proj, fg, proj, proj, log_lb, log1m_lb, cmat, masks, norm_w.reshape(1, HEAD_DIM), s0)
    return y, s_out


def _merge_kernel(x_ref, gpre_ref, ya_ref, yr_ref, yg_ref, wg0_ref, wg1_ref, wg2_ref, wb_ref, wo_ref,
                  gpost_ref, o_ref, h_ref, *, tm, rs):
    j = pl.program_id(1)

    @pl.when(j == 0)
    def _():
        h_ref[...] = _rms(x_ref[...], gpre_ref[...]).astype(BF16)

    for r in range(tm // rs):
        rows = slice(r * rs, (r + 1) * rs)
        h = h_ref[rows, :]
        merged = None
        for n, (y_ref, wg_ref) in enumerate(((ya_ref, wg0_ref), (yr_ref, wg1_ref), (yg_ref, wg2_ref))):
            gate = jax.nn.sigmoid(_dot(h, wg_ref[...]))
            term = gate * _dot(y_ref[rows, :], wb_ref[n])
            merged = term if merged is None else merged + term
        part = _dot(merged.astype(BF16), wo_ref[...])

        @pl.when(j == 0)
        def _():
            o_ref[rows, :] = part

        @pl.when(j > 0)
        def _():
            o_ref[rows, :] += part

    @pl.when(j == pl.num_programs(1) - 1)
    def _():
        o_ref[...] = x_ref[...] + _rms(o_ref[...], gpost_ref[...])


def merge(x, g_pre, ya, yr, yg, w_gate, w_branch, w_out, g_post, layer, tm, tn):
    rows, d = x.shape
    nj = d // tn

    def gate_spec(n):
        return pl.BlockSpec((None, d, tn), lambda i, j: (layer, 0, n * nj + j))

    kernel = functools.partial(_merge_kernel, tm=tm, rs=min(tm, ROW_SUB))
    single = pl.Buffered(1)
    return pl.pallas_call(
        kernel,
        grid=(rows // tm, nj),
        in_specs=[
            pl.BlockSpec((tm, d), lambda i, j: (i, 0), pipeline_mode=single),
            pl.BlockSpec((1, d), lambda i, j: (0, 0)),
            pl.BlockSpec((tm, W_MIX), lambda i, j: (i, 0), pipeline_mode=single),
            pl.BlockSpec((tm, W_MIX), lambda i, j: (i, 0), pipeline_mode=single),
            pl.BlockSpec((tm, W_MIX), lambda i, j: (i, 0), pipeline_mode=single),
            gate_spec(0), gate_spec(1), gate_spec(2),
            pl.BlockSpec((None, N_BRANCH, W_MIX, tn), lambda i, j: (layer, 0, 0, j)),
            pl.BlockSpec((None, tn, d), lambda i, j: (layer, j, 0)),
            pl.BlockSpec((1, d), lambda i, j: (0, 0)),
        ],
        out_specs=pl.BlockSpec((tm, d), lambda i, j: (i, 0)),
        out_shape=jax.ShapeDtypeStruct((rows, d), F32),
        scratch_shapes=[pltpu.VMEM((tm, d), BF16)],
        compiler_params=_params("parallel", "arbitrary"),
        name="merge",
    )(x, g_pre, ya, yr, yg, w_gate, w_gate, w_gate, w_branch, w_out, g_post)


HALO = BF16_ROWS


def _ffn_kernel(x_ref, xprev_ref, cst_ref, gpre_ref, wup_ref, wval_ref, cw_ref, cb_ref, wdn_ref, gpost_ref,
                o_ref, uplast_ref, h_ref, up_ref, *, tm, seq):
    i = pl.program_id(0)
    j = pl.program_id(1)
    seqs_per_tile = max(tm // seq, 1)
    tiles_per_seq = max(seq // tm, 1)

    @pl.when(j == 0)
    def _():
        g = gpre_ref[...]
        h_ref[0:HALO, :] = _rms(xprev_ref[...], g).astype(BF16)
        h_ref[HALO:HALO + tm, :] = _rms(x_ref[...], g).astype(BF16)

    up_ref[...] = _dot(h_ref[...], wup_ref[...])
    if seqs_per_tile == 1:
        @pl.when(i % tiles_per_seq == 0)
        def _():
            up_ref[HALO - (CONV_W - 1):HALO, :] = cst_ref[0]

    cw = cw_ref[...]
    rs = min(tm, ROW_SUB)
    for r in range(tm // rs):
        r0 = HALO + r * rs
        tap2 = up_ref[r0 - 2:r0 - 2 + rs, :]
        tap1 = up_ref[r0 - 1:r0 - 1 + rs, :]
        tap0 = up_ref[r0:r0 + rs, :]
        if seqs_per_tile > 1:
            tf = tap0.shape[1]
            st = cst_ref[...]
            st0 = jnp.broadcast_to(st[:, 0:1, :], (seqs_per_tile, seq, tf)).reshape(tm, tf)
            st1 = jnp.broadcast_to(st[:, 1:2, :], (seqs_per_tile, seq, tf)).reshape(tm, tf)
            pos = lax.broadcasted_iota(jnp.int32, (tm, tf), 0) % seq
            tap1 = jnp.where(pos == 0, st1, tap1)
            tap2 = jnp.where(pos == 0, st0, jnp.where(pos == 1, st1, tap2))
        conv = cb_ref[...] + tap2 * cw[0:1, :] + tap1 * cw[1:2, :] + tap0 * cw[2:3, :]
        val = _dot(h_ref[r0:r0 + rs, :], wval_ref[...])
        act = 0.5 * conv * (1.0 + lax.erf(conv * math.sqrt(0.5))) * val
        part = _dot(act.astype(BF16), wdn_ref[...])
        rows = slice(r * rs, (r + 1) * rs)

        @pl.when(j == 0)
        def _():
            o_ref[rows, :] = part

        @pl.when(j > 0)
        def _():
            o_ref[rows, :] += part

    keep = uplast_ref.shape[0]
    uplast_ref[...] = up_ref[HALO + tm - keep:HALO + tm, :]

    @pl.when(j == pl.num_programs(1) - 1)
    def _():
        o_ref[...] = x_ref[...] + _rms(o_ref[...], gpost_ref[...])


def ffn(x, conv_state, cs_layer, g_pre, w_up, w_val, conv_w, conv_b, w_down, g_post, layer, batch, seq, tm, tf):
    rows, d = x.shape
    dff = w_up.shape[2]
    n_tiles = rows // tm
    seqs_per_tile = max(tm // seq, 1)
    tiles_per_seq = max(seq // tm, 1)
    assert rows == batch * seq and (tm % seq == 0 or seq % tm == 0)
    keep = tm if seqs_per_tile > 1 else SUBLANES
    halo_blocks = tm // HALO
    kernel = functools.partial(_ffn_kernel, tm=tm, seq=seq)
    out, uplast = pl.pallas_call(
        kernel,
        grid=(n_tiles, dff // tf),
        in_specs=[
            pl.BlockSpec((tm, d), lambda i, j: (i, 0), pipeline_mode=pl.Buffered(1)),
            pl.BlockSpec((HALO, d), lambda i, j: (jnp.maximum(i * halo_blocks - 1, 0), 0)),
            pl.BlockSpec((None, seqs_per_tile, CONV_W - 1, tf),
                         lambda i, j: (cs_layer, i * seqs_per_tile // tiles_per_seq, 0, j)),
            pl.BlockSpec((1, d), lambda i, j: (0, 0)),
            pl.BlockSpec((None, d, tf), lambda i, j: (layer, 0, j)),
            pl.BlockSpec((None, d, tf), lambda i, j: (layer, 0, j)),
            pl.BlockSpec((CONV_W, tf), lambda i, j: (0, j)),
            pl.BlockSpec((1, tf), lambda i, j: (0, j)),
            pl.BlockSpec((None, tf, d), lambda i, j: (layer, j, 0)),
            pl.BlockSpec((1, d), lambda i, j: (0, 0)),
        ],
        out_specs=[
            pl.BlockSpec((tm, d), lambda i, j: (i, 0)),
            pl.BlockSpec((keep, tf), lambda i, j: (i, j)),
        ],
        out_shape=[
            jax.ShapeDtypeStruct((rows, d), F32),
            jax.ShapeDtypeStruct((n_tiles * keep, dff), F32),
        ],
        scratch_shapes=[
            pltpu.VMEM((HALO + tm, d), BF16),
            pltpu.VMEM((HALO + tm, tf), F32),
        ],
        compiler_params=_params("parallel", "arbitrary"),
        name="ffn",
    )(x, x, conv_state, g_pre, w_up, w_val, conv_w, conv_b, w_down, g_post)
    new_conv = uplast.reshape(batch, -1, dff)[:, -(CONV_W - 1):, :]
    return out, new_conv


def _layer(x, batch, seq, pos0, layer, caches, cache_layer, vecs, mats, tiles):
    k_past, v_past, ret_s0, hgrn_s0, conv_s0 = caches
    (g_mix_pre, g_mix_post, g_ffn_pre, g_ffn_post, rel_bias, ret_norm_w, log_lb, log1m_lb, hgrn_norm_w,
     conv_w, conv_b) = vecs
    w_in, w_gate, w_branch, w_out, w_up, w_val, w_down = mats
    rows = batch * seq
    tn = W_MIX
    proj, fg = norm_proj(x, g_mix_pre, w_in, layer, tiles["tm_proj"], tn, f32_cols=(COL_FG,))

    chunk = min(CHUNK, seq)
    past = ATT_PAST_ROWS
    hb = N_HEADS
    if k_past is None:
        keep = min(past, seq)
        (kv_new,) = norm_proj(x, g_mix_pre, w_in, layer, keep, tn, row_block0=seq // keep - 1, n_row_blocks=1,
                              col_block0=COL_KA, n_col_blocks=2, out_dtype=F32)
        group, sub = past, tiles["att_sub"]
        tables = _attention_tables(rel_bias, sub, chunk, past, pos0, group)
        y_a = attention(
            proj, lambda h, g: (g, h),
            proj, (past, HEAD_DIM), lambda h, g: (jnp.maximum(g - 1, 0), hb + h),
            proj, lambda h, g: (g, hb + h),
            proj, lambda h, g: (jnp.maximum(g - 1, 0), 2 * hb + h),
            proj, lambda h, g: (g, 2 * hb + h),
            tables, seq // group, group, sub, past)
    else:
        keep = seq
        (kv_new,) = norm_proj(x, g_mix_pre, w_in, layer, rows, tn, col_block0=COL_KA, n_col_blocks=2,
                              out_dtype=F32)
        group = sub = seq
        tables = _attention_tables(rel_bias, sub, chunk, past, pos0, group)
        n_l = k_past.shape[0]
        kp = k_past.reshape(n_l, batch * past, W_MIX)
        vp = v_past.reshape(n_l, batch * past, W_MIX)
        y_a = attention(
            proj, lambda h, g: (g, h),
            kp, (None, past, HEAD_DIM), lambda h, g: (cache_layer, g, h),
            proj, lambda h, g: (g, hb + h),
            vp, lambda h, g: (cache_layer, g, h),
            proj, lambda h, g: (g, 2 * hb + h),
            tables, batch, group, sub, past)

    y_r, ret_s = retention(proj, ret_norm_w, ret_s0, cache_layer, batch, seq, tiles["ret_chunk"], pos0)
    y_g, hgrn_s = hgrn2(proj, fg, log_lb, log1m_lb, hgrn_norm_w, hgrn_s0, cache_layer, batch, seq, chunk)

    x = merge(x, g_mix_pre, y_a, y_r, y_g, w_gate, w_branch, w_out, g_mix_post, layer,
              tiles["tm_merge"], tiles["tn_merge"])
    x, new_conv = ffn(x, conv_s0, cache_layer, g_ffn_pre, w_up, w_val, conv_w, conv_b, w_down, g_ffn_post, layer,
                      batch, seq, tiles["tm_ffn"], tiles["tf_ffn"])
    k_new = kv_new[:, :W_MIX].reshape(batch, keep, N_HEADS, HEAD_DIM)
    v_new = kv_new[:, W_MIX:].reshape(batch, keep, N_HEADS, HEAD_DIM)
    return x, k_new, v_new, ret_s, hgrn_s, new_conv


PROMPT_TILES = dict(tm_proj=1024, att_sub=256, ret_chunk=256, tm_merge=1024, tn_merge=256, tm_ffn=1024, tf_ffn=512)


def _sample_tiles(rows, seq):
    return dict(tm_proj=rows, ret_chunk=seq, tm_merge=rows, tn_merge=256, tm_ffn=rows, tf_ffn=512)


def kernel(x_prompt, x_sample, cache_attn_k, cache_attn_v, state_ret, state_hgrn, state_ffn_conv,
           norm_mix_pre, norm_mix_post, norm_ffn_pre, norm_ffn_post, w_in, attn_rel_bias, ret_norm_w,
           hgrn_lower_bound, hgrn_norm_w, w_gate, w_branch, w_out, ffn_w_up, ffn_w_val, ffn_conv_w,
           ffn_conv_b, ffn_w_down):
    depth = w_in.shape[0]
    bp, tp, d = x_prompt.shape
    bs, ts, _ = x_sample.shape

    lb_cum = jnp.cumsum(jax.nn.softmax(hgrn_lower_bound.astype(F32), axis=0), axis=0)
    lb_layers = lb_cum - lb_cum[0:1]
    log_lb = jnp.log(lb_layers)
    log1m_lb = jnp.log1p(-lb_layers)

    mats = tuple(w.astype(BF16) for w in (w_in, w_gate, w_branch, w_out, ffn_w_up, ffn_w_val, ffn_w_down))

    zs = jnp.zeros((1, bp, N_HEADS, HEAD_DIM, HEAD_DIM), F32)
    zc = jnp.zeros((1, bp, CONV_W - 1, D_FF), F32)
    caches_p = (None, None, zs, zs, zc)
    caches_s = (cache_attn_k, cache_attn_v, state_ret, state_hgrn, state_ffn_conv)

    xp = x_prompt.reshape(bp * tp, d)
    xs = x_sample.reshape(bs * ts, d)
    outs_p = [[] for _ in range(5)]
    outs_s = [[] for _ in range(5)]
    for l in range(depth):
        vecs = (norm_mix_pre[l][None], norm_mix_post[l][None], norm_ffn_pre[l][None], norm_ffn_post[l][None],
                attn_rel_bias[l], ret_norm_w[l], log_lb[l][None], log1m_lb[l][None], hgrn_norm_w[l],
                ffn_conv_w[l], ffn_conv_b[l][None])
        res_p = _layer(xp, bp, tp, 0, l, caches_p, 0, vecs, mats, PROMPT_TILES)
        res_s = _layer(xs, bs, ts, PAST_LEN, l, caches_s, l, vecs, mats, _sample_tiles(bs * ts, ts))
        xp, xs = res_p[0], res_s[0]
        for lst, val in zip(outs_p, res_p[1:]):
            lst.append(val)
        for lst, val in zip(outs_s, res_s[1:]):
            lst.append(val)

    return (xp.reshape(bp, tp, d), xs.reshape(bs, ts, d),
            *(jnp.stack(o) for o in outs_p), *(jnp.stack(o) for o in outs_s))
```

```python
import functools
import math

import numpy as np
import jax
import jax.numpy as jnp
from jax import lax
from jax.experimental import pallas as pl
from jax.experimental.pallas import tpu as pltpu

F32 = jnp.float32
BF16 = jnp.bfloat16

D_MODEL = 2048
CHUNK = 64
BAND_CHUNKS = 8
ATT_PAST_ROWS = BAND_CHUNKS * CHUNK
W_MIX = D_MODEL // 2
N_HEADS = 8
HEAD_DIM = W_MIX // N_HEADS
REL_CLIP = 128
N_BRANCH = 3
N_IN = 11 * W_MIX
D_FF = 5632
CONV_W = 3
ROPE_BASE = 10000.0
EPS = 1e-6
PAST_LEN = 1024

COL_QA, COL_KA, COL_VA, COL_QR, COL_KR, COL_VR, COL_GR, COL_QG, COL_FG, COL_IG, COL_GG = range(11)

V7X_VMEM_BYTES = 64 * 1024 * 1024
VMEM_LIMIT = V7X_VMEM_BYTES - 6 * 1024 * 1024
SUBLANES = 8
LANES = 128
BF16_ROWS = 16
ROW_SUB = 512

NT_DIMS = (((1,), (1,)), ((), ()))
TN_DIMS = (((0,), (0,)), ((), ()))


def _params(*sem):
    return pltpu.CompilerParams(dimension_semantics=sem, vmem_limit_bytes=VMEM_LIMIT)


def _rms(x, g):
    return x * lax.rsqrt(jnp.mean(x * x, axis=-1, keepdims=True) + EPS) * g


def _dot(a, b):
    return jnp.dot(a, b, preferred_element_type=F32)


def _dot_nt(a, b):
    return lax.dot_general(a, b, NT_DIMS, preferred_element_type=F32)


def _dot_tn(a, b):
    return lax.dot_general(a, b, TN_DIMS, preferred_element_type=F32)


def _silu(x):
    return x * jax.nn.sigmoid(x)


def _norm_proj_kernel(x_ref, g_ref, w_ref, o_ref, *rest, f32_cols):
    f32_refs, h_ref = rest[:-1], rest[-1]
    j = pl.program_id(1)

    @pl.when(j == 0)
    def _():
        h_ref[...] = _rms(x_ref[...], g_ref[...]).astype(BF16)

    acc = _dot(h_ref[...], w_ref[...])
    o_ref[...] = acc.astype(o_ref.dtype)
    for col, ref in zip(f32_cols, f32_refs):
        @pl.when(j == col)
        def _():
            ref[...] = acc


def norm_proj(x, g, w, layer, tm, tn, f32_cols=(), row_block0=0, n_row_blocks=None, col_block0=0,
              n_col_blocks=None, out_dtype=BF16):
    d = x.shape[1]
    n_row_blocks = x.shape[0] // tm - row_block0 if n_row_blocks is None else n_row_blocks
    n_col_blocks = w.shape[2] // tn - col_block0 if n_col_blocks is None else n_col_blocks
    rows, n = n_row_blocks * tm, n_col_blocks * tn
    kernel = functools.partial(_norm_proj_kernel, f32_cols=tuple(f32_cols))
    outs = pl.pallas_call(
        kernel,
        grid=(n_row_blocks, n_col_blocks),
        in_specs=[
            pl.BlockSpec((tm, d), lambda i, j: (row_block0 + i, 0)),
            pl.BlockSpec((1, d), lambda i, j: (0, 0)),
            pl.BlockSpec((None, d, tn), lambda i, j: (layer, 0, col_block0 + j)),
        ],
        out_specs=[pl.BlockSpec((tm, tn), lambda i, j: (i, j))]
        + [pl.BlockSpec((tm, tn), lambda i, j: (i, 0)) for _ in f32_cols],
        out_shape=[jax.ShapeDtypeStruct((rows, n), out_dtype)]
        + [jax.ShapeDtypeStruct((rows, tn), F32) for _ in f32_cols],
        scratch_shapes=[pltpu.VMEM((tm, d), BF16)],
        compiler_params=_params("parallel", "arbitrary"),
        name="norm_proj",
    )(x, g, w)
    return outs


def _attention_kernel(q_ref, kp_ref, ko_ref, vp_ref, vo_ref, *rest, past, group, sub):
    n_sub = group // sub
    bias_refs, (o_ref, kw_ref, vw_ref) = rest[:n_sub], rest[n_sub:]
    kw_ref[0:past, :] = kp_ref[...].astype(BF16)
    kw_ref[past:past + group, :] = ko_ref[...].astype(BF16)
    vw_ref[0:past, 0:HEAD_DIM] = vp_ref[...].astype(BF16)
    vw_ref[past:past + group, 0:HEAD_DIM] = vo_ref[...].astype(BF16)
    vw_ref[:, HEAD_DIM:2 * HEAD_DIM] = jnp.ones((past + group, HEAD_DIM), BF16)
    scale = HEAD_DIM ** -0.5
    span = past + sub
    for s in range(n_sub):
        q = (q_ref[s * sub:(s + 1) * sub, :].astype(F32) * scale).astype(BF16)
        k = kw_ref[s * sub:s * sub + span, :]
        v = vw_ref[s * sub:s * sub + span, :]
        sc = _dot_nt(q, k) + bias_refs[s][...]
        p = jnp.exp(sc - jnp.max(sc, axis=-1, keepdims=True))
        ov = _dot(p.astype(BF16), v)
        o_ref[s * sub:(s + 1) * sub, :] = (ov[:, 0:HEAD_DIM] / ov[:, HEAD_DIM:2 * HEAD_DIM]).astype(o_ref.dtype)


def _toeplitz_rows(z, n_rows, width):
    m = z.shape[-1]
    flat = jnp.tile(z, (1, n_rows))[:, :n_rows * (m - 1)]
    return flat.reshape(z.shape[0], n_rows, m - 1)[:, :, :width]


def _attention_tables(rel_bias, sub, chunk, past, pos0, group):
    width = past + sub
    m = np.arange(width + sub - 1)
    idx = np.clip(past + (sub - 1) - m, -REL_CLIP, REL_CLIP) + REL_CLIP
    n_hi = int(np.sum(idx == idx[0]))
    n_lo = int(np.sum(idx == idx[-1]))
    mid = idx[n_hi:idx.size - n_lo]
    assert mid.size and np.all(np.diff(mid) == -1)
    v = jnp.concatenate([jnp.broadcast_to(rel_bias[:, idx[0]:idx[0] + 1], (N_HEADS, n_hi)),
                         jnp.flip(rel_bias[:, int(mid[-1]):int(mid[0]) + 1], axis=1),
                         jnp.broadcast_to(rel_bias[:, idx[-1]:idx[-1] + 1], (N_HEADS, n_lo))], axis=1)
    z = jnp.concatenate([v[:, sub - 1:], jnp.zeros((N_HEADS, 1), F32), v[:, :sub - 1]], axis=1)
    toe = _toeplitz_rows(z, sub, width)
    i = np.arange(sub)[:, None]
    j = np.arange(width)[None, :]
    band = (j >= (i // chunk) * chunk) & (j < (i // chunk) * chunk + past + chunk)
    masks = [band]
    if pos0 < past:
        for s in range(group // sub):
            masks.append(band & (j + (pos0 - past + s * sub) >= 0))
    neg = np.where(np.stack(masks), 0.0, -np.inf).astype(np.float32)
    return toe[None] + jnp.asarray(neg)[:, None]


def attention(q_arr, q_map, kp_arr, kp_block, kp_map, ko_arr, ko_map, vp_arr, vp_map, vo_arr, vo_map, tables,
              n_groups, group, sub, past):
    rows = n_groups * group
    n_sub = group // sub
    n_var = tables.shape[0]
    kernel = functools.partial(_attention_kernel, past=past, group=group, sub=sub)
    width = past + sub

    def table_spec(s):
        if n_var == 1:
            return pl.BlockSpec((None, None, sub, width), lambda h, g: (0, h, 0, 0))
        return pl.BlockSpec((None, None, sub, width), lambda h, g: (jnp.where(g == 0, 1 + s, 0), h, 0, 0))

    return pl.pallas_call(
        kernel,
        grid=(N_HEADS, n_groups),
        in_specs=[
            pl.BlockSpec((group, HEAD_DIM), q_map),
            pl.BlockSpec(kp_block, kp_map),
            pl.BlockSpec((group, HEAD_DIM), ko_map),
            pl.BlockSpec(kp_block, vp_map),
            pl.BlockSpec((group, HEAD_DIM), vo_map),
        ] + [table_spec(s) for s in range(n_sub)],
        out_specs=pl.BlockSpec((group, HEAD_DIM), lambda h, g: (g, h)),
        out_shape=jax.ShapeDtypeStruct((rows, W_MIX), BF16),
        scratch_shapes=[pltpu.VMEM((past + group, HEAD_DIM), BF16),
                        pltpu.VMEM((past + group, 2 * HEAD_DIM), BF16)],
        compiler_params=_params("parallel", "arbitrary"),
        name="attention",
    )(q_arr, kp_arr, ko_arr, vp_arr, vo_arr, *([tables] * n_sub))


def _retention_kernel(q_ref, k_ref, v_ref, g_ref, cos_ref, sin_ref, di_ref, dq_ref, dk_ref, ds_ref,
                      nw_ref, s0_ref, y_ref, sout_ref, s_ref):
    t = pl.program_id(1)

    @pl.when(t == 0)
    def _():
        s_ref[...] = s0_ref[...]

    cosf = cos_ref[...]
    sinf = sin_ref[...]
    nw = nw_ref[...]
    qscale = HEAD_DIM ** -0.5

    for h in range(N_HEADS):
        hs = slice(h * HEAD_DIM, (h + 1) * HEAD_DIM)
        q = q_ref[:, hs].astype(F32)
        k = k_ref[:, hs].astype(F32)
        q = (q * cosf + pltpu.roll(q, HEAD_DIM // 2, 1) * sinf) * qscale
        k = k * cosf + pltpu.roll(k, HEAD_DIM // 2, 1) * sinf
        v = v_ref[:, hs]
        s = s_ref[h]
        a = _dot_nt(q.astype(BF16), k.astype(BF16)) * di_ref[h]
        o = _dot(a.astype(BF16), v) + _dot((q * dq_ref[h]).astype(BF16), s.astype(BF16))
        s_ref[h] = s * ds_ref[h] + _dot_tn((k * dk_ref[h]).astype(BF16), v)
        mu = jnp.mean(o, axis=-1, keepdims=True)
        oc = o - mu
        var = jnp.mean(oc * oc, axis=-1, keepdims=True)
        on = oc * lax.rsqrt(var + EPS) * nw
        y_ref[:, hs] = (_silu(g_ref[:, hs].astype(F32)) * on).astype(y_ref.dtype)

    @pl.when(t == pl.num_programs(1) - 1)
    def _():
        sout_ref[...] = s_ref[...]


def _retention_tables(chunk, pos):
    log_gamma = jnp.log1p(-(2.0 ** (-5.0 - jnp.arange(N_HEADS, dtype=F32))))
    idx = jnp.arange(chunk, dtype=F32)
    diff = idx[:, None] - idx[None, :]
    d_intra = jnp.where(diff[None] >= 0, jnp.exp(jnp.maximum(diff, 0.0)[None] * log_gamma[:, None, None]), 0.0)
    d_q = jnp.exp((idx + 1.0)[None, :] * log_gamma[:, None])
    d_k = jnp.exp((chunk - 1.0 - idx)[None, :] * log_gamma[:, None])
    d_s = jnp.exp(chunk * log_gamma)
    d_q = jnp.broadcast_to(d_q[:, :, None], (N_HEADS, chunk, HEAD_DIM))
    d_k = jnp.broadcast_to(d_k[:, :, None], (N_HEADS, chunk, HEAD_DIM))
    d_s = jnp.broadcast_to(d_s[:, None, None], (N_HEADS, HEAD_DIM, HEAD_DIM))
    inv = ROPE_BASE ** (-jnp.arange(0, HEAD_DIM, 2, dtype=F32) / HEAD_DIM)
    ang = pos.astype(F32)[:, None] * inv[None, :]
    cos, sin = jnp.cos(ang), jnp.sin(ang)
    cosf = jnp.concatenate([cos, cos], axis=-1)
    sinf = jnp.concatenate([-sin, sin], axis=-1)
    return d_intra, d_q, d_k, d_s, cosf, sinf


def _state_spec(layer):
    return pl.BlockSpec((None, None, N_HEADS, HEAD_DIM, HEAD_DIM), lambda b, t: (layer, b, 0, 0, 0))


def retention(proj, norm_w, s0, s0_layer, batch, seq, chunk, pos0):
    n_chunks = seq // chunk
    d_intra, d_q, d_k, d_s, cosf, sinf = _retention_tables(chunk, pos0 + jnp.arange(seq))

    def col(c):
        return pl.BlockSpec((chunk, W_MIX), lambda b, t: (b * n_chunks + t, c))

    def const(shape):
        return pl.BlockSpec(shape, lambda b, t: (0,) * len(shape))

    y, s_out = pl.pallas_call(
        _retention_kernel,
        grid=(batch, n_chunks),
        in_specs=[
            col(COL_QR), col(COL_KR), col(COL_VR), col(COL_GR),
            pl.BlockSpec((chunk, HEAD_DIM), lambda b, t: (t, 0)),
            pl.BlockSpec((chunk, HEAD_DIM), lambda b, t: (t, 0)),
            const((N_HEADS, chunk, chunk)),
            const((N_HEADS, chunk, HEAD_DIM)),
            const((N_HEADS, chunk, HEAD_DIM)),
            const((N_HEADS, HEAD_DIM, HEAD_DIM)),
            const((1, HEAD_DIM)),
            _state_spec(s0_layer),
        ],
        out_specs=[
            pl.BlockSpec((chunk, W_MIX), lambda b, t: (b * n_chunks + t, 0)),
            pl.BlockSpec((None, N_HEADS, HEAD_DIM, HEAD_DIM), lambda b, t: (b, 0, 0, 0)),
        ],
        out_shape=[
            jax.ShapeDtypeStruct((batch * seq, W_MIX), BF16),
            jax.ShapeDtypeStruct((batch, N_HEADS, HEAD_DIM, HEAD_DIM), F32),
        ],
        scratch_shapes=[pltpu.VMEM((N_HEADS, HEAD_DIM, HEAD_DIM), F32)],
        compiler_params=_params("parallel", "arbitrary"),
        name="retention",
    )(proj, proj, proj, proj, cosf, sinf, d_intra, d_q, d_k, d_s, norm_w.reshape(1, HEAD_DIM), s0)
    return y, s_out


def _hgrn_level_masks(chunk):
    levels = int(math.log2(chunk))
    n = np.arange(chunk)
    masks = []
    for lv in range(levels):
        s = chunk >> (lv + 1)
        pair = n // (2 * s)
        qside = (n // s) % 2 == 1
        masks.append((qside[:, None] & ~qside[None, :] & (pair[:, None] == pair[None, :])).astype(np.float32))
    masks.append(np.eye(chunk, dtype=np.float32))
    return np.tril(np.ones((chunk, chunk), np.float32)), np.stack(masks, axis=0), levels


def _hgrn_kernel(q_ref, f_ref, i_ref, g_ref, loglb_ref, log1mlb_ref, tril_ref, mask_ref, nw_ref, s0_ref,
                 y_ref, sout_ref, st_ref, e_ref, x_ref, *, chunk, levels):
    t = pl.program_id(1)

    @pl.when(t == 0)
    def _():
        for h in range(N_HEADS):
            st_ref[h] = s0_ref[h].T

    fg = f_ref[...]
    log_sig = jnp.minimum(fg, 0.0) - jnp.log1p(jnp.exp(-jnp.abs(fg)))
    a = loglb_ref[...]
    b = log1mlb_ref[...] + log_sig
    logf = jnp.maximum(a, b) + jnp.log1p(jnp.exp(-jnp.abs(a - b)))
    kk = 1.0 - jnp.exp(logf)
    hi = logf.astype(BF16)
    r1 = logf - hi.astype(F32)
    mid = r1.astype(BF16)
    lo = (r1 - mid.astype(F32)).astype(BF16)
    tril = tril_ref[...]
    e_ref[...] = _dot(tril, hi) + _dot(tril, mid) + _dot(tril, lo)
    bcum = e_ref[...]

    def boundary_rows(s):
        def bcast(r, n_rows):
            return jnp.broadcast_to(e_ref[r:r + 1, :], (n_rows, W_MIX))
        if 2 * s >= SUBLANES:
            return jnp.concatenate([bcast(p * 2 * s + s - 1, 2 * s) for p in range(chunk // (2 * s))], axis=0)
        sub = lax.broadcasted_iota(jnp.int32, (SUBLANES, W_MIX), 0)
        groups = []
        for g in range(chunk // SUBLANES):
            acc = bcast(g * SUBLANES + s - 1, SUBLANES)
            for p in range(1, SUBLANES // (2 * s)):
                acc = jnp.where(sub >= p * 2 * s, bcast(g * SUBLANES + p * 2 * s + s - 1, SUBLANES), acc)
            groups.append(acc)
        return jnp.concatenate(groups, axis=0)

    q = q_ref[...].astype(F32)
    row = lax.broadcasted_iota(jnp.int32, (chunk, W_MIX), 0)
    x_ref[levels] = q.astype(BF16)
    x_ref[levels + 1] = kk.astype(BF16)
    for lv in range(levels):
        s = chunk >> (lv + 1)
        qside = ((row // s) % 2) == 1
        d = bcum - boundary_rows(s)
        x_ref[lv] = (jnp.where(qside, q, kk) * jnp.exp(jnp.where(qside, d, -d))).astype(BF16)
    b_last = bcum[chunk - 1:chunk, :]
    x_ref[levels + 2] = (q * jnp.exp(bcum)).astype(BF16)
    x_ref[levels + 3] = (kk * jnp.exp(b_last - bcum)).astype(BF16)
    decay = jnp.exp(b_last)
    nw = nw_ref[...]

    for h in range(N_HEADS):
        hs = slice(h * HEAD_DIM, (h + 1) * HEAD_DIM)
        amat = mask_ref[levels] * _dot_nt(x_ref[levels, :, hs], x_ref[levels + 1, :, hs])
        for lv in range(levels):
            x = x_ref[lv, :, hs]
            amat = amat + mask_ref[lv] * _dot_nt(x, x)
        v = i_ref[:, hs]
        st = st_ref[h]
        o = _dot(amat.astype(BF16), v) + _dot_nt(x_ref[levels + 2, :, hs], st.astype(BF16))
        st_ref[h] = st * decay[:, hs] + _dot_tn(v, x_ref[levels + 3, :, hs])
        y_ref[:, hs] = (_silu(g_ref[:, hs].astype(F32)) * _rms(o, nw)).astype(y_ref.dtype)

    @pl.when(t == pl.num_programs(1) - 1)
    def _():
        for h in range(N_HEADS):
            sout_ref[h] = st_ref[h].T


def hgrn2(proj, fg, log_lb, log1m_lb, norm_w, s0, s0_layer, batch, seq, chunk):
    n_chunks = seq // chunk
    tril_np, mask_np, levels = _hgrn_level_masks(chunk)
    tril = jnp.asarray(tril_np, dtype=BF16)
    masks = jnp.asarray(mask_np)

    def col(c):
        return pl.BlockSpec((chunk, W_MIX), lambda b, t: (b * n_chunks + t, c))

    def const(shape):
        return pl.BlockSpec(shape, lambda b, t: (0,) * len(shape))

    kernel = functools.partial(_hgrn_kernel, chunk=chunk, levels=levels)
    y, s_out = pl.pallas_call(
        kernel,
        grid=(batch, n_chunks),
        in_specs=[
            col(COL_QG), col(0), col(COL_IG), col(COL_GG),
            const((1, W_MIX)), const((1, W_MIX)),
            const((chunk, chunk)),
            const((levels + 1, chunk, chunk)),
            const((1, HEAD_DIM)),
            _state_spec(s0_layer),
        ],
        out_specs=[
            pl.BlockSpec((chunk, W_MIX), lambda b, t: (b * n_chunks + t, 0)),
            pl.BlockSpec((None, N_HEADS, HEAD_DIM, HEAD_DIM), lambda b, t: (b, 0, 0, 0)),
        ],
        out_shape=[
            jax.ShapeDtypeStruct((batch * seq, W_MIX), BF16),
            jax.ShapeDtypeStruct((batch, N_HEADS, HEAD_DIM, HEAD_DIM), F32),
        ],
        scratch_shapes=[
            pltpu.VMEM((N_HEADS, HEAD_DIM, HEAD_DIM), F32),
            pltpu.VMEM((chunk, W_MIX), F32),
            pltpu.VMEM((levels + 4, chunk, W_MIX), BF16),
        ],
        compiler_params=_params("parallel", "arbitrary"),
        name="hgrn2",
    )(proj, fg, proj, proj, log_lb, log1m_lb, tril, masks, norm_w.reshape(1, HEAD_DIM), s0)
    return y, s_out


def _merge_kernel(x_ref, gpre_ref, ya_ref, yr_ref, yg_ref, wg0_ref, wg1_ref, wg2_ref, wb_ref, wo_ref,
                  gpost_ref, o_ref, h_ref, *, tm, rs):
    j = pl.program_id(1)

    @pl.when(j == 0)
    def _():
        h_ref[...] = _rms(x_ref[...], gpre_ref[...]).astype(BF16)
        o_ref[...] = jnp.zeros_like(o_ref)

    for r in range(tm // rs):
        rows = slice(r * rs, (r + 1) * rs)
        h = h_ref[rows, :]
        merged = None
        for n, (y_ref, wg_ref) in enumerate(((ya_ref, wg0_ref), (yr_ref, wg1_ref), (yg_ref, wg2_ref))):
            gate = jax.nn.sigmoid(_dot(h, wg_ref[...]))
            term = gate * _dot(y_ref[rows, :], wb_ref[n])
            merged = term if merged is None else merged + term
        o_ref[rows, :] += _dot(merged.astype(BF16), wo_ref[...])

    @pl.when(j == pl.num_programs(1) - 1)
    def _():
        o_ref[...] = x_ref[...] + _rms(o_ref[...], gpost_ref[...])


def merge(x, g_pre, ya, yr, yg, w_gate, w_branch, w_out, g_post, layer, tm, tn):
    rows, d = x.shape
    nj = d // tn

    def gate_spec(n):
        return pl.BlockSpec((None, d, tn), lambda i, j: (layer, 0, n * nj + j))

    kernel = functools.partial(_merge_kernel, tm=tm, rs=min(tm, ROW_SUB))
    single = pl.Buffered(1)
    return pl.pallas_call(
        kernel,
        grid=(rows // tm, nj),
        in_specs=[
            pl.BlockSpec((tm, d), lambda i, j: (i, 0), pipeline_mode=single),
            pl.BlockSpec((1, d), lambda i, j: (0, 0)),
            pl.BlockSpec((tm, W_MIX), lambda i, j: (i, 0), pipeline_mode=single),
            pl.BlockSpec((tm, W_MIX), lambda i, j: (i, 0), pipeline_mode=single),
            pl.BlockSpec((tm, W_MIX), lambda i, j: (i, 0), pipeline_mode=single),
            gate_spec(0), gate_spec(1), gate_spec(2),
            pl.BlockSpec((None, N_BRANCH, W_MIX, tn), lambda i, j: (layer, 0, 0, j)),
            pl.BlockSpec((None, tn, d), lambda i, j: (layer, j, 0)),
            pl.BlockSpec((1, d), lambda i, j: (0, 0)),
        ],
        out_specs=pl.BlockSpec((tm, d), lambda i, j: (i, 0)),
        out_shape=jax.ShapeDtypeStruct((rows, d), F32),
        scratch_shapes=[pltpu.VMEM((tm, d), BF16)],
        compiler_params=_params("parallel", "arbitrary"),
        name="merge",
    )(x, g_pre, ya, yr, yg, w_gate, w_gate, w_gate, w_branch, w_out, g_post)


HALO = BF16_ROWS


def _ffn_kernel(x_ref, xprev_ref, cst_ref, gpre_ref, wup_ref, wval_ref, cw_ref, cb_ref, wdn_ref, gpost_ref,
                o_ref, uplast_ref, h_ref, up_ref, *, tm, seq):
    i = pl.program_id(0)
    j = pl.program_id(1)
    seqs_per_tile = max(tm // seq, 1)
    tiles_per_seq = max(seq // tm, 1)

    @pl.when(j == 0)
    def _():
        g = gpre_ref[...]
        h_ref[0:HALO, :] = _rms(xprev_ref[...], g).astype(BF16)
        h_ref[HALO:HALO + tm, :] = _rms(x_ref[...], g).astype(BF16)
        o_ref[...] = jnp.zeros_like(o_ref)

    cw = cw_ref[...]
    rs = min(tm, ROW_SUB)
    n_sub = tm // rs

    def up_block(r):
        lo = 0 if r == 0 else HALO + r * rs
        hi = HALO + (r + 1) * rs
        up_ref[lo:hi, :] = _dot(h_ref[lo:hi, :], wup_ref[...])

    up_block(0)
    if seqs_per_tile == 1:
        first = i % tiles_per_seq == 0
        halo_rows = slice(HALO - (CONV_W - 1), HALO)
        up_ref[halo_rows, :] = jnp.where(first, cst_ref[0], up_ref[halo_rows, :])
    for r in range(n_sub):
        if r + 1 < n_sub:
            up_block(r + 1)
        r0 = HALO + r * rs
        tap2 = up_ref[r0 - 2:r0 - 2 + rs, :]
        tap1 = up_ref[r0 - 1:r0 - 1 + rs, :]
        tap0 = up_ref[r0:r0 + rs, :]
        if seqs_per_tile > 1:
            tf = tap0.shape[1]
            st = cst_ref[...]
            st0 = jnp.broadcast_to(st[:, 0:1, :], (seqs_per_tile, seq, tf)).reshape(tm, tf)
            st1 = jnp.broadcast_to(st[:, 1:2, :], (seqs_per_tile, seq, tf)).reshape(tm, tf)
            pos = lax.broadcasted_iota(jnp.int32, (tm, tf), 0) % seq
            tap1 = jnp.where(pos == 0, st1, tap1)
            tap2 = jnp.where(pos == 0, st0, jnp.where(pos == 1, st1, tap2))
        conv = cb_ref[...] + tap2 * cw[0:1, :] + tap1 * cw[1:2, :] + tap0 * cw[2:3, :]
        val = _dot(h_ref[r0:r0 + rs, :], wval_ref[...])
        act = 0.5 * conv * (1.0 + lax.erf(conv * math.sqrt(0.5))) * val
        o_ref[r * rs:(r + 1) * rs, :] += _dot(act.astype(BF16), wdn_ref[...])

    keep = uplast_ref.shape[0]
    uplast_ref[...] = up_ref[HALO + tm - keep:HALO + tm, :]

    @pl.when(j == pl.num_programs(1) - 1)
    def _():
        o_ref[...] = x_ref[...] + _rms(o_ref[...], gpost_ref[...])


def ffn(x, conv_state, cs_layer, g_pre, w_up, w_val, conv_w, conv_b, w_down, g_post, layer, batch, seq, tm, tf):
    rows, d = x.shape
    dff = w_up.shape[2]
    n_tiles = rows // tm
    seqs_per_tile = max(tm // seq, 1)
    tiles_per_seq = max(seq // tm, 1)
    assert rows == batch * seq and (tm % seq == 0 or seq % tm == 0)
    keep = tm if seqs_per_tile > 1 else SUBLANES
    halo_blocks = tm // HALO
    kernel = functools.partial(_ffn_kernel, tm=tm, seq=seq)
    out, uplast = pl.pallas_call(
        kernel,
        grid=(n_tiles, dff // tf),
        in_specs=[
            pl.BlockSpec((tm, d), lambda i, j: (i, 0), pipeline_mode=pl.Buffered(1)),
            pl.BlockSpec((HALO, d), lambda i, j: (jnp.maximum(i * halo_blocks - 1, 0), 0)),
            pl.BlockSpec((None, seqs_per_tile, CONV_W - 1, tf),
                         lambda i, j: (cs_layer, i * seqs_per_tile // tiles_per_seq, 0, j)),
            pl.BlockSpec((1, d), lambda i, j: (0, 0)),
            pl.BlockSpec((None, d, tf), lambda i, j: (layer, 0, j)),
            pl.BlockSpec((None, d, tf), lambda i, j: (layer, 0, j)),
            pl.BlockSpec((CONV_W, tf), lambda i, j: (0, j)),
            pl.BlockSpec((1, tf), lambda i, j: (0, j)),
            pl.BlockSpec((None, tf, d), lambda i, j: (layer, j, 0)),
            pl.BlockSpec((1, d), lambda i, j: (0, 0)),
        ],
        out_specs=[
            pl.BlockSpec((tm, d), lambda i, j: (i, 0)),
            pl.BlockSpec((keep, tf), lambda i, j: (i, j)),
        ],
        out_shape=[
            jax.ShapeDtypeStruct((rows, d), F32),
            jax.ShapeDtypeStruct((n_tiles * keep, dff), F32),
        ],
        scratch_shapes=[
            pltpu.VMEM((HALO + tm, d), BF16),
            pltpu.VMEM((HALO + tm, tf), F32),
        ],
        compiler_params=_params("parallel", "arbitrary"),
        name="ffn",
    )(x, x, conv_state, g_pre, w_up, w_val, conv_w, conv_b, w_down, g_post)
    new_conv = uplast.reshape(batch, -1, dff)[:, -(CONV_W - 1):, :]
    return out, new_conv


def _layer(x, batch, seq, pos0, layer, caches, cache_layer, vecs, mats, tiles):
    k_past, v_past, ret_s0, hgrn_s0, conv_s0 = caches
    (g_mix_pre, g_mix_post, g_ffn_pre, g_ffn_post, rel_bias, ret_norm_w, log_lb, log1m_lb, hgrn_norm_w,
     conv_w, conv_b) = vecs
    w_in, w_gate, w_branch, w_out, w_up, w_val, w_down = mats
    rows = batch * seq
    tn = W_MIX
    proj, fg = norm_proj(x, g_mix_pre, w_in, layer, tiles["tm_proj"], tn, f32_cols=(COL_FG,))

    chunk = min(CHUNK, seq)
    past = ATT_PAST_ROWS
    hb = N_HEADS
    if k_past is None:
        keep = min(past, seq)
        (kv_new,) = norm_proj(x, g_mix_pre, w_in, layer, keep, tn, row_block0=seq // keep - 1, n_row_blocks=1,
                              col_block0=COL_KA, n_col_blocks=2, out_dtype=F32)
        group, sub = past, tiles["att_sub"]
        tables = _attention_tables(rel_bias, sub, chunk, past, pos0, group)
        y_a = attention(
            proj, lambda h, g: (g, h),
            proj, (past, HEAD_DIM), lambda h, g: (jnp.maximum(g - 1, 0), hb + h),
            proj, lambda h, g: (g, hb + h),
            proj, lambda h, g: (jnp.maximum(g - 1, 0), 2 * hb + h),
            proj, lambda h, g: (g, 2 * hb + h),
            tables, seq // group, group, sub, past)
    else:
        keep = seq
        (kv_new,) = norm_proj(x, g_mix_pre, w_in, layer, rows, tn, col_block0=COL_KA, n_col_blocks=2,
                              out_dtype=F32)
        group = sub = seq
        tables = _attention_tables(rel_bias, sub, chunk, past, pos0, group)
        n_l = k_past.shape[0]
        kp = k_past.reshape(n_l, batch * past, W_MIX)
        vp = v_past.reshape(n_l, batch * past, W_MIX)
        y_a = attention(
            proj, lambda h, g: (g, h),
            kp, (None, past, HEAD_DIM), lambda h, g: (cache_layer, g, h),
            proj, lambda h, g: (g, hb + h),
            vp, lambda h, g: (cache_layer, g, h),
            proj, lambda h, g: (g, 2 * hb + h),
            tables, batch, group, sub, past)

    y_r, ret_s = retention(proj, ret_norm_w, ret_s0, cache_layer, batch, seq, tiles["ret_chunk"], pos0)
    y_g, hgrn_s = hgrn2(proj, fg, log_lb, log1m_lb, hgrn_norm_w, hgrn_s0, cache_layer, batch, seq, chunk)

    x = merge(x, g_mix_pre, y_a, y_r, y_g, w_gate, w_branch, w_out, g_mix_post, layer,
              tiles["tm_merge"], tiles["tn_merge"])
    x, new_conv = ffn(x, conv_s0, cache_layer, g_ffn_pre, w_up, w_val, conv_w, conv_b, w_down, g_ffn_post, layer,
                      batch, seq, tiles["tm_ffn"], tiles["tf_ffn"])
    k_new = kv_new[:, :W_MIX].reshape(batch, keep, N_HEADS, HEAD_DIM)
    v_new = kv_new[:, W_MIX:].reshape(batch, keep, N_HEADS, HEAD_DIM)
    return x, k_new, v_new, ret_s, hgrn_s, new_conv


PROMPT_TILES = dict(tm_proj=1024, att_sub=256, ret_chunk=256, tm_merge=1024, tn_merge=256, tm_ffn=1024, tf_ffn=512)


def _sample_tiles(rows, seq):
    return dict(tm_proj=rows, ret_chunk=seq, tm_merge=rows, tn_merge=256, tm_ffn=rows, tf_ffn=512)


def kernel(x_prompt, x_sample, cache_attn_k, cache_attn_v, state_ret, state_hgrn, state_ffn_conv,
           norm_mix_pre, norm_mix_post, norm_ffn_pre, norm_ffn_post, w_in, attn_rel_bias, ret_norm_w,
           hgrn_lower_bound, hgrn_norm_w, w_gate, w_branch, w_out, ffn_w_up, ffn_w_val, ffn_conv_w,
           ffn_conv_b, ffn_w_down):
    depth = w_in.shape[0]
    bp, tp, d = x_prompt.shape
    bs, ts, _ = x_sample.shape

    lb_cum = jnp.cumsum(jax.nn.softmax(hgrn_lower_bound.astype(F32), axis=0), axis=0)
    lb_layers = lb_cum - lb_cum[0:1]
    log_lb = jnp.log(lb_layers)
    log1m_lb = jnp.log1p(-lb_layers)

    mats = tuple(w.astype(BF16) for w in (w_in, w_gate, w_branch, w_out, ffn_w_up, ffn_w_val, ffn_w_down))

    zs = jnp.zeros((1, bp, N_HEADS, HEAD_DIM, HEAD_DIM), F32)
    zc = jnp.zeros((1, bp, CONV_W - 1, D_FF), F32)
    caches_p = (None, None, zs, zs, zc)
    caches_s = (cache_attn_k, cache_attn_v, state_ret, state_hgrn, state_ffn_conv)

    xp = x_prompt.reshape(bp * tp, d)
    xs = x_sample.reshape(bs * ts, d)
    outs_p = [[] for _ in range(5)]
    outs_s = [[] for _ in range(5)]
    for l in range(depth):
        vecs = (norm_mix_pre[l][None], norm_mix_post[l][None], norm_ffn_pre[l][None], norm_ffn_post[l][None],
                attn_rel_bias[l], ret_norm_w[l], log_lb[l][None], log1m_lb[l][None], hgrn_norm_w[l],
                ffn_conv_w[l], ffn_conv_b[l][None])
        res_p = _layer(xp, bp, tp, 0, l, caches_p, 0, vecs, mats, PROMPT_TILES)
        res_s = _layer(xs, bs, ts, PAST_LEN, l, caches_s, l, vecs, mats, _sample_tiles(bs * ts, ts))
        xp, xs = res_p[0], res_s[0]
        for lst, val in zip(outs_p, res_p[1:]):
            lst.append(val)
        for lst, val in zip(outs_s, res_s[1:]):
            lst.append(val)

    return (xp.reshape(bp, tp, d), xs.reshape(bs, ts, d),
            *(jnp.stack(o) for o in outs_p), *(jnp.stack(o) for o in outs_s))
```

```python
import functools
import math

import numpy as np
import jax
import jax.numpy as jnp
from jax import lax
from jax.experimental import pallas as pl
from jax.experimental.pallas import tpu as pltpu

F32 = jnp.float32
BF16 = jnp.bfloat16

D_MODEL = 2048
CHUNK = 64
BAND_CHUNKS = 8
ATT_PAST_ROWS = BAND_CHUNKS * CHUNK
W_MIX = D_MODEL // 2
N_HEADS = 8
HEAD_DIM = W_MIX // N_HEADS
REL_CLIP = 128
N_BRANCH = 3
N_IN = 11 * W_MIX
D_FF = 5632
CONV_W = 3
ROPE_BASE = 10000.0
EPS = 1e-6
PAST_LEN = 1024

COL_QA, COL_KA, COL_VA, COL_QR, COL_KR, COL_VR, COL_GR, COL_QG, COL_FG, COL_IG, COL_GG = range(11)

V7X_VMEM_BYTES = 64 * 1024 * 1024
VMEM_LIMIT = V7X_VMEM_BYTES - 6 * 1024 * 1024
SUBLANES = 8
LANES = 128
BF16_ROWS = 16
ROW_SUB = 512

NT_DIMS = (((1,), (1,)), ((), ()))
TN_DIMS = (((0,), (0,)), ((), ()))


def _params(*sem):
    return pltpu.CompilerParams(dimension_semantics=sem, vmem_limit_bytes=VMEM_LIMIT)


def _rms(x, g):
    return x * lax.rsqrt(jnp.mean(x * x, axis=-1, keepdims=True) + EPS) * g


def _dot(a, b):
    return jnp.dot(a, b, preferred_element_type=F32)


def _dot_nt(a, b):
    return lax.dot_general(a, b, NT_DIMS, preferred_element_type=F32)


def _dot_tn(a, b):
    return lax.dot_general(a, b, TN_DIMS, preferred_element_type=F32)


def _silu(x):
    return x * jax.nn.sigmoid(x)


def _norm_proj_kernel(x_ref, g_ref, w_ref, o_ref, *rest, f32_cols):
    f32_refs, h_ref = rest[:-1], rest[-1]
    j = pl.program_id(1)

    @pl.when(j == 0)
    def _():
        h_ref[...] = _rms(x_ref[...], g_ref[...]).astype(BF16)

    acc = _dot(h_ref[...], w_ref[...])
    o_ref[...] = acc.astype(o_ref.dtype)
    for col, ref in zip(f32_cols, f32_refs):
        @pl.when(j == col)
        def _():
            ref[...] = acc


def norm_proj(x, g, w, layer, tm, tn, f32_cols=(), row_block0=0, n_row_blocks=None, col_block0=0,
              n_col_blocks=None, out_dtype=BF16):
    d = x.shape[1]
    n_row_blocks = x.shape[0] // tm - row_block0 if n_row_blocks is None else n_row_blocks
    n_col_blocks = w.shape[2] // tn - col_block0 if n_col_blocks is None else n_col_blocks
    rows, n = n_row_blocks * tm, n_col_blocks * tn
    kernel = functools.partial(_norm_proj_kernel, f32_cols=tuple(f32_cols))
    outs = pl.pallas_call(
        kernel,
        grid=(n_row_blocks, n_col_blocks),
        in_specs=[
            pl.BlockSpec((tm, d), lambda i, j: (row_block0 + i, 0)),
            pl.BlockSpec((1, d), lambda i, j: (0, 0)),
            pl.BlockSpec((None, d, tn), lambda i, j: (layer, 0, col_block0 + j)),
        ],
        out_specs=[pl.BlockSpec((tm, tn), lambda i, j: (i, j))]
        + [pl.BlockSpec((tm, tn), lambda i, j: (i, 0)) for _ in f32_cols],
        out_shape=[jax.ShapeDtypeStruct((rows, n), out_dtype)]
        + [jax.ShapeDtypeStruct((rows, tn), F32) for _ in f32_cols],
        scratch_shapes=[pltpu.VMEM((tm, d), BF16)],
        compiler_params=_params("parallel", "arbitrary"),
        name="norm_proj",
    )(x, g, w)
    return outs


def _attention_kernel(q_ref, kp_ref, ko_ref, vp_ref, vo_ref, *rest, past, group, sub, hps):
    n_sub = group // sub
    bias_refs, (o_ref, kw_ref, vw_ref) = rest[:n_sub], rest[n_sub:]
    kw_ref[0:past, :] = kp_ref[...].astype(BF16)
    kw_ref[past:past + group, :] = ko_ref[...].astype(BF16)
    for hh in range(hps):
        hs = slice(hh * HEAD_DIM, (hh + 1) * HEAD_DIM)
        vs = slice(2 * hh * HEAD_DIM, (2 * hh + 1) * HEAD_DIM)
        vw_ref[0:past, vs] = vp_ref[:, hs].astype(BF16)
        vw_ref[past:past + group, vs] = vo_ref[:, hs].astype(BF16)
        vw_ref[:, (2 * hh + 1) * HEAD_DIM:(2 * hh + 2) * HEAD_DIM] = jnp.ones((past + group, HEAD_DIM), BF16)
    scale = HEAD_DIM ** -0.5
    span = past + sub
    steps = [(hh, s) for hh in range(hps) for s in range(n_sub)]
    scores = []
    for hh, s in steps:
        hs = slice(hh * HEAD_DIM, (hh + 1) * HEAD_DIM)
        q = (q_ref[s * sub:(s + 1) * sub, hs].astype(F32) * scale).astype(BF16)
        scores.append(_dot_nt(q, kw_ref[s * sub:s * sub + span, hs]) + bias_refs[s][hh])
    probs = [jnp.exp(sc - jnp.max(sc, axis=-1, keepdims=True)).astype(BF16) for sc in scores]
    for (hh, s), p in zip(steps, probs):
        ov = _dot(p, vw_ref[s * sub:s * sub + span, 2 * hh * HEAD_DIM:(2 * hh + 2) * HEAD_DIM])
        o_ref[s * sub:(s + 1) * sub, hh * HEAD_DIM:(hh + 1) * HEAD_DIM] = (
            ov[:, 0:HEAD_DIM] / ov[:, HEAD_DIM:2 * HEAD_DIM]).astype(o_ref.dtype)


def _toeplitz_rows(z, n_rows, width):
    m = z.shape[-1]
    flat = jnp.tile(z, (1, n_rows))[:, :n_rows * (m - 1)]
    return flat.reshape(z.shape[0], n_rows, m - 1)[:, :, :width]


def _n_first_group_variants(sub, past, pos0, group):
    return min(group // sub, max(0, -(-(past - pos0) // sub)))


def _attention_tables(rel_bias, sub, chunk, past, pos0, group):
    width = past + sub
    m = np.arange(width + sub - 1)
    idx = np.clip(past + (sub - 1) - m, -REL_CLIP, REL_CLIP) + REL_CLIP
    n_hi = int(np.sum(idx == idx[0]))
    n_lo = int(np.sum(idx == idx[-1]))
    mid = idx[n_hi:idx.size - n_lo]
    assert mid.size and np.all(np.diff(mid) == -1)
    v = jnp.concatenate([jnp.broadcast_to(rel_bias[:, idx[0]:idx[0] + 1], (N_HEADS, n_hi)),
                         jnp.flip(rel_bias[:, int(mid[-1]):int(mid[0]) + 1], axis=1),
                         jnp.broadcast_to(rel_bias[:, idx[-1]:idx[-1] + 1], (N_HEADS, n_lo))], axis=1)
    z = jnp.concatenate([v[:, sub - 1:], jnp.zeros((N_HEADS, 1), F32), v[:, :sub - 1]], axis=1)
    toe = _toeplitz_rows(z, sub, width)
    i = np.arange(sub)[:, None]
    j = np.arange(width)[None, :]
    band = (j >= (i // chunk) * chunk) & (j < (i // chunk) * chunk + past + chunk)
    masks = [band]
    for s in range(_n_first_group_variants(sub, past, pos0, group)):
        masks.append(band & (j + (pos0 - past + s * sub) >= 0))
    neg = np.where(np.stack(masks), 0.0, -np.inf).astype(np.float32)
    return toe[None] + jnp.asarray(neg)[:, None]


def attention(q_arr, q_map, kp_arr, kp_lead, kp_map, ko_arr, ko_map, vp_arr, vp_map, vo_arr, vo_map, tables,
              n_groups, group, sub, past, hps):
    rows = n_groups * group
    n_sub = group // sub
    n_var = tables.shape[0]
    kernel = functools.partial(_attention_kernel, past=past, group=group, sub=sub, hps=hps)
    width = past + sub
    lanes = hps * HEAD_DIM

    def table_spec(s):
        if s >= n_var - 1:
            return pl.BlockSpec((None, hps, sub, width), lambda h, g: (0, h, 0, 0))
        return pl.BlockSpec((None, hps, sub, width), lambda h, g: (jnp.where(g == 0, 1 + s, 0), h, 0, 0))

    return pl.pallas_call(
        kernel,
        grid=(N_HEADS // hps, n_groups),
        in_specs=[
            pl.BlockSpec((group, lanes), q_map),
            pl.BlockSpec(kp_lead + (past, lanes), kp_map),
            pl.BlockSpec((group, lanes), ko_map),
            pl.BlockSpec(kp_lead + (past, lanes), vp_map),
            pl.BlockSpec((group, lanes), vo_map),
        ] + [table_spec(s) for s in range(n_sub)],
        out_specs=pl.BlockSpec((group, lanes), lambda h, g: (g, h)),
        out_shape=jax.ShapeDtypeStruct((rows, W_MIX), BF16),
        scratch_shapes=[pltpu.VMEM((past + group, lanes), BF16),
                        pltpu.VMEM((past + group, 2 * lanes), BF16)],
        compiler_params=_params("parallel", "arbitrary"),
        name="attention",
    )(q_arr, kp_arr, ko_arr, vp_arr, vo_arr, *([tables] * n_sub))


def _retention_kernel(q_ref, k_ref, v_ref, g_ref, cos_ref, sin_ref, di_ref, dq_ref, dk_ref, ds_ref,
                      nw_ref, s0_ref, y_ref, sout_ref, s_ref):
    t = pl.program_id(1)

    @pl.when(t == 0)
    def _():
        s_ref[...] = s0_ref[...]

    cosf = cos_ref[...]
    sinf = sin_ref[...]
    nw = nw_ref[...]
    qscale = HEAD_DIM ** -0.5

    heads = [slice(h * HEAD_DIM, (h + 1) * HEAD_DIM) for h in range(N_HEADS)]
    qs, ks, qds, kds = [], [], [], []
    for h, hs in enumerate(heads):
        q = q_ref[:, hs].astype(F32)
        k = k_ref[:, hs].astype(F32)
        q = (q * cosf + pltpu.roll(q, HEAD_DIM // 2, 1) * sinf) * qscale
        k = k * cosf + pltpu.roll(k, HEAD_DIM // 2, 1) * sinf
        qs.append(q.astype(BF16))
        ks.append(k.astype(BF16))
        qds.append((q * dq_ref[h]).astype(BF16))
        kds.append((k * dk_ref[h]).astype(BF16))
    amats = [(_dot_nt(qs[h], ks[h]) * di_ref[h]).astype(BF16) for h in range(N_HEADS)]
    outs = [_dot(amats[h], v_ref[:, hs]) + _dot(qds[h], s_ref[h].astype(BF16)) for h, hs in enumerate(heads)]
    for h, hs in enumerate(heads):
        s_ref[h] = s_ref[h] * ds_ref[h] + _dot_tn(kds[h], v_ref[:, hs])
    for h, hs in enumerate(heads):
        o = outs[h]
        mu = jnp.mean(o, axis=-1, keepdims=True)
        oc = o - mu
        var = jnp.mean(oc * oc, axis=-1, keepdims=True)
        on = oc * lax.rsqrt(var + EPS) * nw
        y_ref[:, hs] = (_silu(g_ref[:, hs].astype(F32)) * on).astype(y_ref.dtype)

    @pl.when(t == pl.num_programs(1) - 1)
    def _():
        sout_ref[...] = s_ref[...]


def _retention_tables(chunk, pos):
    log_gamma = jnp.log1p(-(2.0 ** (-5.0 - jnp.arange(N_HEADS, dtype=F32))))
    idx = jnp.arange(chunk, dtype=F32)
    diff = idx[:, None] - idx[None, :]
    d_intra = jnp.where(diff[None] >= 0, jnp.exp(jnp.maximum(diff, 0.0)[None] * log_gamma[:, None, None]), 0.0)
    d_q = jnp.exp((idx + 1.0)[None, :] * log_gamma[:, None])
    d_k = jnp.exp((chunk - 1.0 - idx)[None, :] * log_gamma[:, None])
    d_s = jnp.exp(chunk * log_gamma)
    d_q = jnp.broadcast_to(d_q[:, :, None], (N_HEADS, chunk, HEAD_DIM))
    d_k = jnp.broadcast_to(d_k[:, :, None], (N_HEADS, chunk, HEAD_DIM))
    d_s = jnp.broadcast_to(d_s[:, None, None], (N_HEADS, HEAD_DIM, HEAD_DIM))
    inv = ROPE_BASE ** (-jnp.arange(0, HEAD_DIM, 2, dtype=F32) / HEAD_DIM)
    ang = pos.astype(F32)[:, None] * inv[None, :]
    cos, sin = jnp.cos(ang), jnp.sin(ang)
    cosf = jnp.concatenate([cos, cos], axis=-1)
    sinf = jnp.concatenate([-sin, sin], axis=-1)
    return d_intra, d_q, d_k, d_s, cosf, sinf


def _state_spec(layer):
    return pl.BlockSpec((None, None, N_HEADS, HEAD_DIM, HEAD_DIM), lambda b, t: (layer, b, 0, 0, 0))


def retention(proj, norm_w, s0, s0_layer, batch, seq, chunk, pos0):
    n_chunks = seq // chunk
    d_intra, d_q, d_k, d_s, cosf, sinf = _retention_tables(chunk, pos0 + jnp.arange(seq))

    def col(c):
        return pl.BlockSpec((chunk, W_MIX), lambda b, t: (b * n_chunks + t, c))

    def const(shape):
        return pl.BlockSpec(shape, lambda b, t: (0,) * len(shape))

    y, s_out = pl.pallas_call(
        _retention_kernel,
        grid=(batch, n_chunks),
        in_specs=[
            col(COL_QR), col(COL_KR), col(COL_VR), col(COL_GR),
            pl.BlockSpec((chunk, HEAD_DIM), lambda b, t: (t, 0)),
            pl.BlockSpec((chunk, HEAD_DIM), lambda b, t: (t, 0)),
            const((N_HEADS, chunk, chunk)),
            const((N_HEADS, chunk, HEAD_DIM)),
            const((N_HEADS, chunk, HEAD_DIM)),
            const((N_HEADS, HEAD_DIM, HEAD_DIM)),
            const((1, HEAD_DIM)),
            _state_spec(s0_layer),
        ],
        out_specs=[
            pl.BlockSpec((chunk, W_MIX), lambda b, t: (b * n_chunks + t, 0)),
            pl.BlockSpec((None, N_HEADS, HEAD_DIM, HEAD_DIM), lambda b, t: (b, 0, 0, 0)),
        ],
        out_shape=[
            jax.ShapeDtypeStruct((batch * seq, W_MIX), BF16),
            jax.ShapeDtypeStruct((batch, N_HEADS, HEAD_DIM, HEAD_DIM), F32),
        ],
        scratch_shapes=[pltpu.VMEM((N_HEADS, HEAD_DIM, HEAD_DIM), F32)],
        compiler_params=_params("parallel", "arbitrary"),
        name="retention",
    )(proj, proj, proj, proj, cosf, sinf, d_intra, d_q, d_k, d_s, norm_w.reshape(1, HEAD_DIM), s0)
    return y, s_out


def _hgrn_level_masks(chunk):
    levels = int(math.log2(chunk))
    n = np.arange(chunk)
    masks = []
    for lv in range(levels):
        s = chunk >> (lv + 1)
        pair = n // (2 * s)
        qside = (n // s) % 2 == 1
        masks.append((qside[:, None] & ~qside[None, :] & (pair[:, None] == pair[None, :])).astype(np.float32))
    masks.append(np.eye(chunk, dtype=np.float32))
    return np.tril(np.ones((chunk, chunk), np.float32)), np.stack(masks, axis=0), levels


def _hgrn_kernel(q_ref, f_ref, i_ref, g_ref, loglb_ref, log1mlb_ref, tril_ref, mask_ref, nw_ref, s0_ref,
                 y_ref, sout_ref, st_ref, e_ref, x_ref, *, chunk, levels):
    t = pl.program_id(1)

    @pl.when(t == 0)
    def _():
        for h in range(N_HEADS):
            st_ref[h] = s0_ref[h].T

    fg = f_ref[...]
    log_sig = jnp.minimum(fg, 0.0) - jnp.log(1.0 + jnp.exp(-jnp.abs(fg)))
    a = loglb_ref[...]
    b = log1mlb_ref[...] + log_sig
    logf = jnp.maximum(a, b) + jnp.log(1.0 + jnp.exp(-jnp.abs(a - b)))
    kk = 1.0 - jnp.exp(logf)
    hi = logf.astype(BF16)
    r1 = logf - hi.astype(F32)
    mid = r1.astype(BF16)
    lo = (r1 - mid.astype(F32)).astype(BF16)
    tril = tril_ref[...]
    e_ref[...] = _dot(tril, hi) + _dot(tril, mid) + _dot(tril, lo)
    bcum = e_ref[...]

    def boundary_rows(s):
        def bcast(r, n_rows):
            return jnp.broadcast_to(e_ref[r:r + 1, :], (n_rows, W_MIX))
        if 2 * s >= SUBLANES:
            return jnp.concatenate([bcast(p * 2 * s + s - 1, 2 * s) for p in range(chunk // (2 * s))], axis=0)
        sub = lax.broadcasted_iota(jnp.int32, (SUBLANES, W_MIX), 0)
        groups = []
        for g in range(chunk // SUBLANES):
            acc = bcast(g * SUBLANES + s - 1, SUBLANES)
            for p in range(1, SUBLANES // (2 * s)):
                acc = jnp.where(sub >= p * 2 * s, bcast(g * SUBLANES + p * 2 * s + s - 1, SUBLANES), acc)
            groups.append(acc)
        return jnp.concatenate(groups, axis=0)

    q = q_ref[...].astype(F32)
    row = lax.broadcasted_iota(jnp.int32, (chunk, W_MIX), 0)
    x_ref[levels] = q.astype(BF16)
    x_ref[levels + 1] = kk.astype(BF16)
    for lv in range(levels):
        s = chunk >> (lv + 1)
        qside = ((row // s) % 2) == 1
        d = bcum - boundary_rows(s)
        x_ref[lv] = (jnp.where(qside, q, kk) * jnp.exp(jnp.where(qside, d, -d))).astype(BF16)
    b_last = bcum[chunk - 1:chunk, :]
    x_ref[levels + 2] = (q * jnp.exp(bcum)).astype(BF16)
    x_ref[levels + 3] = (kk * jnp.exp(b_last - bcum)).astype(BF16)
    decay = jnp.exp(b_last)
    nw = nw_ref[...]

    heads = [slice(h * HEAD_DIM, (h + 1) * HEAD_DIM) for h in range(N_HEADS)]
    amats = []
    for hs in heads:
        amat = mask_ref[levels] * _dot_nt(x_ref[levels, :, hs], x_ref[levels + 1, :, hs])
        for lv in range(levels):
            x = x_ref[lv, :, hs]
            amat = amat + mask_ref[lv] * _dot_nt(x, x)
        amats.append(amat.astype(BF16))
    outs = []
    for h, hs in enumerate(heads):
        outs.append(_dot(amats[h], i_ref[:, hs]) + _dot_nt(x_ref[levels + 2, :, hs], st_ref[h].astype(BF16)))
    for h, hs in enumerate(heads):
        st_ref[h] = st_ref[h] * decay[:, hs] + _dot_tn(i_ref[:, hs], x_ref[levels + 3, :, hs])
    for h, hs in enumerate(heads):
        y_ref[:, hs] = (_silu(g_ref[:, hs].astype(F32)) * _rms(outs[h], nw)).astype(y_ref.dtype)

    @pl.when(t == pl.num_programs(1) - 1)
    def _():
        for h in range(N_HEADS):
            sout_ref[h] = st_ref[h].T


def hgrn2(proj, fg, log_lb, log1m_lb, norm_w, s0, s0_layer, batch, seq, chunk):
    n_chunks = seq // chunk
    tril_np, mask_np, levels = _hgrn_level_masks(chunk)
    tril = jnp.asarray(tril_np, dtype=BF16)
    masks = jnp.asarray(mask_np)

    def col(c):
        return pl.BlockSpec((chunk, W_MIX), lambda b, t: (b * n_chunks + t, c))

    def const(shape):
        return pl.BlockSpec(shape, lambda b, t: (0,) * len(shape))

    kernel = functools.partial(_hgrn_kernel, chunk=chunk, levels=levels)
    y, s_out = pl.pallas_call(
        kernel,
        grid=(batch, n_chunks),
        in_specs=[
            col(COL_QG), col(0), col(COL_IG), col(COL_GG),
            const((1, W_MIX)), const((1, W_MIX)),
            const((chunk, chunk)),
            const((levels + 1, chunk, chunk)),
            const((1, HEAD_DIM)),
            _state_spec(s0_layer),
        ],
        out_specs=[
            pl.BlockSpec((chunk, W_MIX), lambda b, t: (b * n_chunks + t, 0)),
            pl.BlockSpec((None, N_HEADS, HEAD_DIM, HEAD_DIM), lambda b, t: (b, 0, 0, 0)),
        ],
        out_shape=[
            jax.ShapeDtypeStruct((batch * seq, W_MIX), BF16),
            jax.ShapeDtypeStruct((batch, N_HEADS, HEAD_DIM, HEAD_DIM), F32),
        ],
        scratch_shapes=[
            pltpu.VMEM((N_HEADS, HEAD_DIM, HEAD_DIM), F32),
            pltpu.VMEM((chunk, W_MIX), F32),
            pltpu.VMEM((levels + 4, chunk, W_MIX), BF16),
        ],
        compiler_params=_params("parallel", "arbitrary"),
        name="hgrn2",
    )(proj, fg, proj, proj, log_lb, log1m_lb, tril, masks, norm_w.reshape(1, HEAD_DIM), s0)
    return y, s_out


def _merge_kernel(x_ref, gpre_ref, ya_ref, yr_ref, yg_ref, wg0_ref, wg1_ref, wg2_ref, wb_ref, wo_ref,
                  gpost_ref, o_ref, h_ref, *, tm, rs):
    j = pl.program_id(1)

    @pl.when(j == 0)
    def _():
        h_ref[...] = _rms(x_ref[...], gpre_ref[...]).astype(BF16)
        o_ref[...] = jnp.zeros_like(o_ref)

    for r in range(tm // rs):
        rows = slice(r * rs, (r + 1) * rs)
        h = h_ref[rows, :]
        merged = None
        for n, (y_ref, wg_ref) in enumerate(((ya_ref, wg0_ref), (yr_ref, wg1_ref), (yg_ref, wg2_ref))):
            gate = jax.nn.sigmoid(_dot(h, wg_ref[...]))
            term = gate * _dot(y_ref[rows, :], wb_ref[n])
            merged = term if merged is None else merged + term
        o_ref[rows, :] += _dot(merged.astype(BF16), wo_ref[...])

    @pl.when(j == pl.num_programs(1) - 1)
    def _():
        o_ref[...] = x_ref[...] + _rms(o_ref[...], gpost_ref[...])


def merge(x, g_pre, ya, yr, yg, w_gate, w_branch, w_out, g_post, layer, tm, tn):
    rows, d = x.shape
    nj = d // tn

    def gate_spec(n):
        return pl.BlockSpec((None, d, tn), lambda i, j: (layer, 0, n * nj + j))

    kernel = functools.partial(_merge_kernel, tm=tm, rs=min(tm, ROW_SUB))
    single = pl.Buffered(1)
    return pl.pallas_call(
        kernel,
        grid=(rows // tm, nj),
        in_specs=[
            pl.BlockSpec((tm, d), lambda i, j: (i, 0), pipeline_mode=single),
            pl.BlockSpec((1, d), lambda i, j: (0, 0)),
            pl.BlockSpec((tm, W_MIX), lambda i, j: (i, 0), pipeline_mode=single),
            pl.BlockSpec((tm, W_MIX), lambda i, j: (i, 0), pipeline_mode=single),
            pl.BlockSpec((tm, W_MIX), lambda i, j: (i, 0), pipeline_mode=single),
            gate_spec(0), gate_spec(1), gate_spec(2),
            pl.BlockSpec((None, N_BRANCH, W_MIX, tn), lambda i, j: (layer, 0, 0, j)),
            pl.BlockSpec((None, tn, d), lambda i, j: (layer, j, 0)),
            pl.BlockSpec((1, d), lambda i, j: (0, 0)),
        ],
        out_specs=pl.BlockSpec((tm, d), lambda i, j: (i, 0)),
        out_shape=jax.ShapeDtypeStruct((rows, d), F32),
        scratch_shapes=[pltpu.VMEM((tm, d), BF16)],
        compiler_params=_params("parallel", "arbitrary"),
        name="merge",
    )(x, g_pre, ya, yr, yg, w_gate, w_gate, w_gate, w_branch, w_out, g_post)


HALO = BF16_ROWS


def _ffn_kernel(x_ref, xprev_ref, cst_ref, gpre_ref, wup_ref, wval_ref, cw_ref, cb_ref, wdn_ref, gpost_ref,
                o_ref, uplast_ref, h_ref, up_ref, *, tm, seq):
    i = pl.program_id(0)
    j = pl.program_id(1)
    seqs_per_tile = max(tm // seq, 1)
    tiles_per_seq = max(seq // tm, 1)

    @pl.when(j == 0)
    def _():
        g = gpre_ref[...]
        h_ref[0:HALO, :] = _rms(xprev_ref[...], g).astype(BF16)
        h_ref[HALO:HALO + tm, :] = _rms(x_ref[...], g).astype(BF16)
        o_ref[...] = jnp.zeros_like(o_ref)

    cw = cw_ref[...]
    rs = min(tm, ROW_SUB)
    n_sub = tm // rs

    def up_block(r):
        lo = 0 if r == 0 else HALO + r * rs
        hi = HALO + (r + 1) * rs
        up_ref[lo:hi, :] = _dot(h_ref[lo:hi, :], wup_ref[...])

    up_block(0)
    if seqs_per_tile == 1:
        first = i % tiles_per_seq == 0
        halo_rows = slice(HALO - (CONV_W - 1), HALO)
        up_ref[halo_rows, :] = jnp.where(first, cst_ref[0], up_ref[halo_rows, :])
    for r in range(n_sub):
        if r + 1 < n_sub:
            up_block(r + 1)
        r0 = HALO + r * rs
        tap2 = up_ref[r0 - 2:r0 - 2 + rs, :]
        tap1 = up_ref[r0 - 1:r0 - 1 + rs, :]
        tap0 = up_ref[r0:r0 + rs, :]
        if seqs_per_tile > 1:
            tf = tap0.shape[1]
            st = cst_ref[...]
            st0 = jnp.broadcast_to(st[:, 0:1, :], (seqs_per_tile, seq, tf)).reshape(tm, tf)
            st1 = jnp.broadcast_to(st[:, 1:2, :], (seqs_per_tile, seq, tf)).reshape(tm, tf)
            pos = lax.broadcasted_iota(jnp.int32, (tm, tf), 0) % seq
            tap1 = jnp.where(pos == 0, st1, tap1)
            tap2 = jnp.where(pos == 0, st0, jnp.where(pos == 1, st1, tap2))
        conv = cb_ref[...] + tap2 * cw[0:1, :] + tap1 * cw[1:2, :] + tap0 * cw[2:3, :]
        val = _dot(h_ref[r0:r0 + rs, :], wval_ref[...])
        act = 0.5 * conv * (1.0 + lax.erf(conv * math.sqrt(0.5))) * val
        o_ref[r * rs:(r + 1) * rs, :] += _dot(act.astype(BF16), wdn_ref[...])

    keep = uplast_ref.shape[0]
    uplast_ref[...] = up_ref[HALO + tm - keep:HALO + tm, :]

    @pl.when(j == pl.num_programs(1) - 1)
    def _():
        o_ref[...] = x_ref[...] + _rms(o_ref[...], gpost_ref[...])


def ffn(x, conv_state, cs_layer, g_pre, w_up, w_val, conv_w, conv_b, w_down, g_post, layer, batch, seq, tm, tf):
    rows, d = x.shape
    dff = w_up.shape[2]
    n_tiles = rows // tm
    seqs_per_tile = max(tm // seq, 1)
    tiles_per_seq = max(seq // tm, 1)
    assert rows == batch * seq and (tm % seq == 0 or seq % tm == 0)
    keep = tm if seqs_per_tile > 1 else SUBLANES
    halo_blocks = tm // HALO
    kernel = functools.partial(_ffn_kernel, tm=tm, seq=seq)
    out, uplast = pl.pallas_call(
        kernel,
        grid=(n_tiles, dff // tf),
        in_specs=[
            pl.BlockSpec((tm, d), lambda i, j: (i, 0), pipeline_mode=pl.Buffered(1)),
            pl.BlockSpec((HALO, d), lambda i, j: (jnp.maximum(i * halo_blocks - 1, 0), 0)),
            pl.BlockSpec((None, seqs_per_tile, CONV_W - 1, tf),
                         lambda i, j: (cs_layer, i * seqs_per_tile // tiles_per_seq, 0, j)),
            pl.BlockSpec((1, d), lambda i, j: (0, 0)),
            pl.BlockSpec((None, d, tf), lambda i, j: (layer, 0, j)),
            pl.BlockSpec((None, d, tf), lambda i, j: (layer, 0, j)),
            pl.BlockSpec((CONV_W, tf), lambda i, j: (0, j)),
            pl.BlockSpec((1, tf), lambda i, j: (0, j)),
            pl.BlockSpec((None, tf, d), lambda i, j: (layer, j, 0)),
            pl.BlockSpec((1, d), lambda i, j: (0, 0)),
        ],
        out_specs=[
            pl.BlockSpec((tm, d), lambda i, j: (i, 0)),
            pl.BlockSpec((keep, tf), lambda i, j: (i, j)),
        ],
        out_shape=[
            jax.ShapeDtypeStruct((rows, d), F32),
            jax.ShapeDtypeStruct((n_tiles * keep, dff), F32),
        ],
        scratch_shapes=[
            pltpu.VMEM((HALO + tm, d), BF16),
            pltpu.VMEM((HALO + tm, tf), F32),
        ],
        compiler_params=_params("parallel", "arbitrary"),
        name="ffn",
    )(x, x, conv_state, g_pre, w_up, w_val, conv_w, conv_b, w_down, g_post)
    new_conv = uplast.reshape(batch, -1, dff)[:, -(CONV_W - 1):, :]
    return out, new_conv


def _layer(x, batch, seq, pos0, layer, caches, cache_layer, vecs, mats, tiles):
    k_past, v_past, ret_s0, hgrn_s0, conv_s0 = caches
    (g_mix_pre, g_mix_post, g_ffn_pre, g_ffn_post, rel_bias, ret_norm_w, log_lb, log1m_lb, hgrn_norm_w,
     conv_w, conv_b) = vecs
    w_in, w_gate, w_branch, w_out, w_up, w_val, w_down = mats
    rows = batch * seq
    tn = W_MIX
    proj, fg = norm_proj(x, g_mix_pre, w_in, layer, tiles["tm_proj"], tn, f32_cols=(COL_FG,))

    chunk = min(CHUNK, seq)
    past = ATT_PAST_ROWS
    hps = tiles["att_heads"]
    hb = N_HEADS // hps
    if k_past is None:
        keep = min(past, seq)
        (kv_new,) = norm_proj(x, g_mix_pre, w_in, layer, keep, tn, row_block0=seq // keep - 1, n_row_blocks=1,
                              col_block0=COL_KA, n_col_blocks=2, out_dtype=F32)
        group, sub = tiles["att_group"], tiles["att_sub"]
        gp = group // past
        tables = _attention_tables(rel_bias, sub, chunk, past, pos0, group)
        y_a = attention(
            proj, lambda h, g: (g, h),
            proj, (), lambda h, g: (jnp.maximum(g * gp - 1, 0), hb + h),
            proj, lambda h, g: (g, hb + h),
            proj, lambda h, g: (jnp.maximum(g * gp - 1, 0), 2 * hb + h),
            proj, lambda h, g: (g, 2 * hb + h),
            tables, seq // group, group, sub, past, hps)
    else:
        keep = seq
        (kv_new,) = norm_proj(x, g_mix_pre, w_in, layer, rows, tn, col_block0=COL_KA, n_col_blocks=2,
                              out_dtype=F32)
        group = sub = seq
        tables = _attention_tables(rel_bias, sub, chunk, past, pos0, group)
        n_l = k_past.shape[0]
        kp = k_past.reshape(n_l, batch * past, W_MIX)
        vp = v_past.reshape(n_l, batch * past, W_MIX)
        y_a = attention(
            proj, lambda h, g: (g, h),
            kp, (None,), lambda h, g: (cache_layer, g, h),
            proj, lambda h, g: (g, hb + h),
            vp, lambda h, g: (cache_layer, g, h),
            proj, lambda h, g: (g, 2 * hb + h),
            tables, batch, group, sub, past, hps)

    y_r, ret_s = retention(proj, ret_norm_w, ret_s0, cache_layer, batch, seq, tiles["ret_chunk"], pos0)
    y_g, hgrn_s = hgrn2(proj, fg, log_lb, log1m_lb, hgrn_norm_w, hgrn_s0, cache_layer, batch, seq, chunk)

    x = merge(x, g_mix_pre, y_a, y_r, y_g, w_gate, w_branch, w_out, g_mix_post, layer,
              tiles["tm_merge"], tiles["tn_merge"])
    x, new_conv = ffn(x, conv_s0, cache_layer, g_ffn_pre, w_up, w_val, conv_w, conv_b, w_down, g_ffn_post, layer,
                      batch, seq, tiles["tm_ffn"], tiles["tf_ffn"])
    k_new = kv_new[:, :W_MIX].reshape(batch, keep, N_HEADS, HEAD_DIM)
    v_new = kv_new[:, W_MIX:].reshape(batch, keep, N_HEADS, HEAD_DIM)
    return x, k_new, v_new, ret_s, hgrn_s, new_conv


PROMPT_TILES = dict(tm_proj=1024, att_group=1024, att_sub=256, att_heads=1, ret_chunk=256, tm_merge=1024, tn_merge=256, tm_ffn=1024, tf_ffn=512)


def _sample_tiles(rows, seq):
    return dict(tm_proj=rows, att_heads=N_HEADS, ret_chunk=seq, tm_merge=rows, tn_merge=256, tm_ffn=rows, tf_ffn=512)


def kernel(x_prompt, x_sample, cache_attn_k, cache_attn_v, state_ret, state_hgrn, state_ffn_conv,
           norm_mix_pre, norm_mix_post, norm_ffn_pre, norm_ffn_post, w_in, attn_rel_bias, ret_norm_w,
           hgrn_lower_bound, hgrn_norm_w, w_gate, w_branch, w_out, ffn_w_up, ffn_w_val, ffn_conv_w,
           ffn_conv_b, ffn_w_down):
    depth = w_in.shape[0]
    bp, tp, d = x_prompt.shape
    bs, ts, _ = x_sample.shape

    lb_cum = jnp.cumsum(jax.nn.softmax(hgrn_lower_bound.astype(F32), axis=0), axis=0)
    lb_layers = lb_cum - lb_cum[0:1]
    log_lb = jnp.log(lb_layers)
    log1m_lb = jnp.log1p(-lb_layers)

    mats = tuple(w.astype(BF16) for w in (w_in, w_gate, w_branch, w_out, ffn_w_up, ffn_w_val, ffn_w_down))

    zs = jnp.zeros((1, bp, N_HEADS, HEAD_DIM, HEAD_DIM), F32)
    zc = jnp.zeros((1, bp, CONV_W - 1, D_FF), F32)
    caches_p = (None, None, zs, zs, zc)
    caches_s = (cache_attn_k, cache_attn_v, state_ret, state_hgrn, state_ffn_conv)

    xp = x_prompt.reshape(bp * tp, d)
    xs = x_sample.reshape(bs * ts, d)
    outs_p = [[] for _ in range(5)]
    outs_s = [[] for _ in range(5)]
    for l in range(depth):
        vecs = (norm_mix_pre[l][None], norm_mix_post[l][None], norm_ffn_pre[l][None], norm_ffn_post[l][None],
                attn_rel_bias[l], ret_norm_w[l], log_lb[l][None], log1m_lb[l][None], hgrn_norm_w[l],
                ffn_conv_w[l], ffn_conv_b[l][None])
        res_p = _layer(xp, bp, tp, 0, l, caches_p, 0, vecs, mats, PROMPT_TILES)
        res_s = _layer(xs, bs, ts, PAST_LEN, l, caches_s, l, vecs, mats, _sample_tiles(bs * ts, ts))
        xp, xs = res_p[0], res_s[0]
        for lst, val in zip(outs_p, res_p[1:]):
            lst.append(val)
        for lst, val in zip(outs_s, res_s[1:]):
            lst.append(val)

    return (xp.reshape(bp, tp, d), xs.reshape(bs, ts, d),
            *(jnp.stack(o) for o in outs_p), *(jnp.stack(o) for o in outs_s))
```

```python
import functools
import math

import numpy as np
import jax
import jax.numpy as jnp
from jax import lax
from jax.experimental import pallas as pl
from jax.experimental.pallas import tpu as pltpu

F32 = jnp.float32
BF16 = jnp.bfloat16

D_MODEL = 2048
CHUNK = 64
BAND_CHUNKS = 8
ATT_PAST_ROWS = BAND_CHUNKS * CHUNK
W_MIX = D_MODEL // 2
N_HEADS = 8
HEAD_DIM = W_MIX // N_HEADS
REL_CLIP = 128
N_BRANCH = 3
N_IN = 11 * W_MIX
D_FF = 5632
CONV_W = 3
ROPE_BASE = 10000.0
EPS = 1e-6
PAST_LEN = 1024

COL_QA, COL_KA, COL_VA, COL_QR, COL_KR, COL_VR, COL_GR, COL_QG, COL_FG, COL_IG, COL_GG = range(11)

V7X_VMEM_BYTES = 64 * 1024 * 1024
VMEM_LIMIT = V7X_VMEM_BYTES - 6 * 1024 * 1024
SUBLANES = 8
LANES = 128
BF16_ROWS = 16
ROW_SUB = 512

NT_DIMS = (((1,), (1,)), ((), ()))
TN_DIMS = (((0,), (0,)), ((), ()))


def _params(*sem):
    return pltpu.CompilerParams(dimension_semantics=sem, vmem_limit_bytes=VMEM_LIMIT)


def _rms(x, g):
    return x * lax.rsqrt(jnp.mean(x * x, axis=-1, keepdims=True) + EPS) * g


def _dot(a, b):
    return jnp.dot(a, b, preferred_element_type=F32)


def _dot_nt(a, b):
    return lax.dot_general(a, b, NT_DIMS, preferred_element_type=F32)


def _dot_tn(a, b):
    return lax.dot_general(a, b, TN_DIMS, preferred_element_type=F32)


def _silu(x):
    return x * jax.nn.sigmoid(x)


def _norm_proj_kernel(x_ref, g_ref, w_ref, o_ref, *rest, f32_cols):
    f32_refs, h_ref = rest[:-1], rest[-1]
    j = pl.program_id(1)

    @pl.when(j == 0)
    def _():
        h_ref[...] = _rms(x_ref[...], g_ref[...]).astype(BF16)

    acc = _dot(h_ref[...], w_ref[...].astype(BF16))
    o_ref[...] = acc.astype(o_ref.dtype)
    for col, ref in zip(f32_cols, f32_refs):
        @pl.when(j == col)
        def _():
            ref[...] = acc


def norm_proj(x, g, w, layer, tm, tn, f32_cols=(), row_block0=0, n_row_blocks=None, col_block0=0,
              n_col_blocks=None, out_dtype=BF16):
    d = x.shape[1]
    n_row_blocks = x.shape[0] // tm - row_block0 if n_row_blocks is None else n_row_blocks
    n_col_blocks = w.shape[2] // tn - col_block0 if n_col_blocks is None else n_col_blocks
    rows, n = n_row_blocks * tm, n_col_blocks * tn
    kernel = functools.partial(_norm_proj_kernel, f32_cols=tuple(f32_cols))
    outs = pl.pallas_call(
        kernel,
        grid=(n_row_blocks, n_col_blocks),
        in_specs=[
            pl.BlockSpec((tm, d), lambda i, j: (row_block0 + i, 0)),
            pl.BlockSpec((1, d), lambda i, j: (0, 0)),
            pl.BlockSpec((None, d, tn), lambda i, j: (layer, 0, col_block0 + j)),
        ],
        out_specs=[pl.BlockSpec((tm, tn), lambda i, j: (i, j))]
        + [pl.BlockSpec((tm, tn), lambda i, j: (i, 0)) for _ in f32_cols],
        out_shape=[jax.ShapeDtypeStruct((rows, n), out_dtype)]
        + [jax.ShapeDtypeStruct((rows, tn), F32) for _ in f32_cols],
        scratch_shapes=[pltpu.VMEM((tm, d), BF16)],
        compiler_params=_params("parallel", "arbitrary"),
        name="norm_proj",
    )(x, g, w)
    return outs


def _attention_kernel(q_ref, kp_ref, ko_ref, vp_ref, vo_ref, *rest, past, group, sub, hps):
    n_sub = group // sub
    bias_refs, (o_ref, kw_ref, vw_ref) = rest[:n_sub], rest[n_sub:]
    kw_ref[0:past, :] = kp_ref[...].astype(BF16)
    kw_ref[past:past + group, :] = ko_ref[...].astype(BF16)
    for hh in range(hps):
        hs = slice(hh * HEAD_DIM, (hh + 1) * HEAD_DIM)
        vs = slice(2 * hh * HEAD_DIM, (2 * hh + 1) * HEAD_DIM)
        vw_ref[0:past, vs] = vp_ref[:, hs].astype(BF16)
        vw_ref[past:past + group, vs] = vo_ref[:, hs].astype(BF16)
        vw_ref[:, (2 * hh + 1) * HEAD_DIM:(2 * hh + 2) * HEAD_DIM] = jnp.ones((past + group, HEAD_DIM), BF16)
    scale = HEAD_DIM ** -0.5
    span = past + sub
    steps = [(hh, s) for hh in range(hps) for s in range(n_sub)]
    scores = []
    for hh, s in steps:
        hs = slice(hh * HEAD_DIM, (hh + 1) * HEAD_DIM)
        q = (q_ref[s * sub:(s + 1) * sub, hs].astype(F32) * scale).astype(BF16)
        scores.append(_dot_nt(q, kw_ref[s * sub:s * sub + span, hs]) + bias_refs[s][hh])
    probs = [jnp.exp(sc - jnp.max(sc, axis=-1, keepdims=True)).astype(BF16) for sc in scores]
    for (hh, s), p in zip(steps, probs):
        ov = _dot(p, vw_ref[s * sub:s * sub + span, 2 * hh * HEAD_DIM:(2 * hh + 2) * HEAD_DIM])
        o_ref[s * sub:(s + 1) * sub, hh * HEAD_DIM:(hh + 1) * HEAD_DIM] = (
            ov[:, 0:HEAD_DIM] / ov[:, HEAD_DIM:2 * HEAD_DIM]).astype(o_ref.dtype)


def _toeplitz_rows(z, n_rows, width):
    m = z.shape[-1]
    flat = jnp.tile(z, (1, n_rows))[:, :n_rows * (m - 1)]
    return flat.reshape(z.shape[0], n_rows, m - 1)[:, :, :width]


def _n_first_group_variants(sub, past, pos0, group):
    return min(group // sub, max(0, -(-(past - pos0) // sub)))


def _attention_tables(rel_bias, sub, chunk, past, pos0, group):
    width = past + sub
    m = np.arange(width + sub - 1)
    idx = np.clip(past + (sub - 1) - m, -REL_CLIP, REL_CLIP) + REL_CLIP
    n_hi = int(np.sum(idx == idx[0]))
    n_lo = int(np.sum(idx == idx[-1]))
    mid = idx[n_hi:idx.size - n_lo]
    assert mid.size and np.all(np.diff(mid) == -1)
    v = jnp.concatenate([jnp.broadcast_to(rel_bias[:, idx[0]:idx[0] + 1], (N_HEADS, n_hi)),
                         jnp.flip(rel_bias[:, int(mid[-1]):int(mid[0]) + 1], axis=1),
                         jnp.broadcast_to(rel_bias[:, idx[-1]:idx[-1] + 1], (N_HEADS, n_lo))], axis=1)
    z = jnp.concatenate([v[:, sub - 1:], jnp.zeros((N_HEADS, 1), F32), v[:, :sub - 1]], axis=1)
    toe = _toeplitz_rows(z, sub, width)
    i = np.arange(sub)[:, None]
    j = np.arange(width)[None, :]
    band = (j >= (i // chunk) * chunk) & (j < (i // chunk) * chunk + past + chunk)
    masks = [band]
    for s in range(_n_first_group_variants(sub, past, pos0, group)):
        masks.append(band & (j + (pos0 - past + s * sub) >= 0))
    neg = np.where(np.stack(masks), 0.0, -np.inf).astype(np.float32)
    return toe[None] + jnp.asarray(neg)[:, None]


def attention(q_arr, q_map, kp_arr, kp_lead, kp_map, ko_arr, ko_map, vp_arr, vp_map, vo_arr, vo_map, tables,
              n_groups, group, sub, past, hps):
    rows = n_groups * group
    n_sub = group // sub
    n_var = tables.shape[0]
    kernel = functools.partial(_attention_kernel, past=past, group=group, sub=sub, hps=hps)
    width = past + sub
    lanes = hps * HEAD_DIM

    def table_spec(s):
        if s >= n_var - 1:
            return pl.BlockSpec((None, hps, sub, width), lambda h, g: (0, h, 0, 0))
        return pl.BlockSpec((None, hps, sub, width), lambda h, g: (jnp.where(g == 0, 1 + s, 0), h, 0, 0))

    return pl.pallas_call(
        kernel,
        grid=(N_HEADS // hps, n_groups),
        in_specs=[
            pl.BlockSpec((group, lanes), q_map),
            pl.BlockSpec(kp_lead + (past, lanes), kp_map),
            pl.BlockSpec((group, lanes), ko_map),
            pl.BlockSpec(kp_lead + (past, lanes), vp_map),
            pl.BlockSpec((group, lanes), vo_map),
        ] + [table_spec(s) for s in range(n_sub)],
        out_specs=pl.BlockSpec((group, lanes), lambda h, g: (g, h)),
        out_shape=jax.ShapeDtypeStruct((rows, W_MIX), BF16),
        scratch_shapes=[pltpu.VMEM((past + group, lanes), BF16),
                        pltpu.VMEM((past + group, 2 * lanes), BF16)],
        compiler_params=_params("parallel", "arbitrary"),
        name="attention",
    )(q_arr, kp_arr, ko_arr, vp_arr, vo_arr, *([tables] * n_sub))


def _retention_kernel(q_ref, k_ref, v_ref, g_ref, cos_ref, sin_ref, di_ref, dq_ref, dk_ref, ds_ref,
                      nw_ref, s0_ref, y_ref, sout_ref, s_ref):
    t = pl.program_id(1)

    @pl.when(t == 0)
    def _():
        s_ref[...] = s0_ref[...]

    cosf = cos_ref[...]
    sinf = sin_ref[...]
    nw = nw_ref[...]
    qscale = HEAD_DIM ** -0.5

    heads = [slice(h * HEAD_DIM, (h + 1) * HEAD_DIM) for h in range(N_HEADS)]
    qs, ks, qds, kds = [], [], [], []
    for h, hs in enumerate(heads):
        q = q_ref[:, hs].astype(F32)
        k = k_ref[:, hs].astype(F32)
        q = (q * cosf + pltpu.roll(q, HEAD_DIM // 2, 1) * sinf) * qscale
        k = k * cosf + pltpu.roll(k, HEAD_DIM // 2, 1) * sinf
        qs.append(q.astype(BF16))
        ks.append(k.astype(BF16))
        qds.append((q * dq_ref[h]).astype(BF16))
        kds.append((k * dk_ref[h]).astype(BF16))
    amats = [(_dot_nt(qs[h], ks[h]) * di_ref[h]).astype(BF16) for h in range(N_HEADS)]
    outs = [_dot(amats[h], v_ref[:, hs]) + _dot(qds[h], s_ref[h].astype(BF16)) for h, hs in enumerate(heads)]
    for h, hs in enumerate(heads):
        s_ref[h] = s_ref[h] * ds_ref[h] + _dot_tn(kds[h], v_ref[:, hs])
    for h, hs in enumerate(heads):
        o = outs[h]
        mu = jnp.mean(o, axis=-1, keepdims=True)
        oc = o - mu
        var = jnp.mean(oc * oc, axis=-1, keepdims=True)
        on = oc * lax.rsqrt(var + EPS) * nw
        y_ref[:, hs] = (_silu(g_ref[:, hs].astype(F32)) * on).astype(y_ref.dtype)

    @pl.when(t == pl.num_programs(1) - 1)
    def _():
        sout_ref[...] = s_ref[...]


def _retention_tables(chunk, pos):
    log_gamma = jnp.log1p(-(2.0 ** (-5.0 - jnp.arange(N_HEADS, dtype=F32))))
    idx = jnp.arange(chunk, dtype=F32)
    diff = idx[:, None] - idx[None, :]
    d_intra = jnp.where(diff[None] >= 0, jnp.exp(jnp.maximum(diff, 0.0)[None] * log_gamma[:, None, None]), 0.0)
    d_q = jnp.exp((idx + 1.0)[None, :] * log_gamma[:, None])
    d_k = jnp.exp((chunk - 1.0 - idx)[None, :] * log_gamma[:, None])
    d_s = jnp.exp(chunk * log_gamma)
    d_q = jnp.broadcast_to(d_q[:, :, None], (N_HEADS, chunk, HEAD_DIM))
    d_k = jnp.broadcast_to(d_k[:, :, None], (N_HEADS, chunk, HEAD_DIM))
    d_s = jnp.broadcast_to(d_s[:, None, None], (N_HEADS, HEAD_DIM, HEAD_DIM))
    inv = ROPE_BASE ** (-jnp.arange(0, HEAD_DIM, 2, dtype=F32) / HEAD_DIM)
    ang = pos.astype(F32)[:, None] * inv[None, :]
    cos, sin = jnp.cos(ang), jnp.sin(ang)
    cosf = jnp.concatenate([cos, cos], axis=-1)
    sinf = jnp.concatenate([-sin, sin], axis=-1)
    return d_intra, d_q, d_k, d_s, cosf, sinf


def _state_spec(layer):
    return pl.BlockSpec((None, None, N_HEADS, HEAD_DIM, HEAD_DIM), lambda b, t: (layer, b, 0, 0, 0))


def retention(proj, norm_w, s0, s0_layer, batch, seq, chunk, pos0):
    n_chunks = seq // chunk
    d_intra, d_q, d_k, d_s, cosf, sinf = _retention_tables(chunk, pos0 + jnp.arange(seq))

    def col(c):
        return pl.BlockSpec((chunk, W_MIX), lambda b, t: (b * n_chunks + t, c))

    def const(shape):
        return pl.BlockSpec(shape, lambda b, t: (0,) * len(shape))

    y, s_out = pl.pallas_call(
        _retention_kernel,
        grid=(batch, n_chunks),
        in_specs=[
            col(COL_QR), col(COL_KR), col(COL_VR), col(COL_GR),
            pl.BlockSpec((chunk, HEAD_DIM), lambda b, t: (t, 0)),
            pl.BlockSpec((chunk, HEAD_DIM), lambda b, t: (t, 0)),
            const((N_HEADS, chunk, chunk)),
            const((N_HEADS, chunk, HEAD_DIM)),
            const((N_HEADS, chunk, HEAD_DIM)),
            const((N_HEADS, HEAD_DIM, HEAD_DIM)),
            const((1, HEAD_DIM)),
            _state_spec(s0_layer),
        ],
        out_specs=[
            pl.BlockSpec((chunk, W_MIX), lambda b, t: (b * n_chunks + t, 0)),
            pl.BlockSpec((None, N_HEADS, HEAD_DIM, HEAD_DIM), lambda b, t: (b, 0, 0, 0)),
        ],
        out_shape=[
            jax.ShapeDtypeStruct((batch * seq, W_MIX), BF16),
            jax.ShapeDtypeStruct((batch, N_HEADS, HEAD_DIM, HEAD_DIM), F32),
        ],
        scratch_shapes=[pltpu.VMEM((N_HEADS, HEAD_DIM, HEAD_DIM), F32)],
        compiler_params=_params("parallel", "arbitrary"),
        name="retention",
    )(proj, proj, proj, proj, cosf, sinf, d_intra, d_q, d_k, d_s, norm_w.reshape(1, HEAD_DIM), s0)
    return y, s_out


def _hgrn_level_masks(chunk):
    levels = int(math.log2(chunk))
    n = np.arange(chunk)
    masks = []
    for lv in range(levels):
        s = chunk >> (lv + 1)
        pair = n // (2 * s)
        qside = (n // s) % 2 == 1
        masks.append((qside[:, None] & ~qside[None, :] & (pair[:, None] == pair[None, :])).astype(np.float32))
    masks.append(np.eye(chunk, dtype=np.float32))
    return np.tril(np.ones((chunk, chunk), np.float32)), np.stack(masks, axis=0), levels


def _hgrn_kernel(q_ref, f_ref, i_ref, g_ref, loglb_ref, log1mlb_ref, tril_ref, mask_ref, nw_ref, s0_ref,
                 y_ref, sout_ref, st_ref, e_ref, x_ref, *, chunk, levels, cps):
    t = pl.program_id(1)

    @pl.when(t == 0)
    def _():
        for h in range(N_HEADS):
            st_ref[h] = s0_ref[h].T

    fg = f_ref[...]
    log_sig = jnp.minimum(fg, 0.0) - jnp.log(1.0 + jnp.exp(-jnp.abs(fg)))
    a = loglb_ref[...]
    b = log1mlb_ref[...] + log_sig
    logf = jnp.maximum(a, b) + jnp.log(1.0 + jnp.exp(-jnp.abs(a - b)))
    f = jnp.exp(logf)
    kk = 1.0 - f
    hi = logf.astype(BF16)
    r1 = logf - hi.astype(F32)
    mid = r1.astype(BF16)
    lo = (r1 - mid.astype(F32)).astype(BF16)
    tril = tril_ref[...]
    chunks = [slice(c * chunk, (c + 1) * chunk) for c in range(cps)]
    for rc in chunks:
        e_ref[rc, :] = _dot(tril, hi[rc]) + _dot(tril, mid[rc]) + _dot(tril, lo[rc])
    bcum = e_ref[...]

    def bcast(r, n_rows):
        return jnp.broadcast_to(e_ref[r:r + 1, :], (n_rows, W_MIX))

    def boundary_rows(s):
        if 2 * s >= SUBLANES:
            return jnp.concatenate([bcast(p * 2 * s + s - 1, 2 * s) for p in range(cps * chunk // (2 * s))], axis=0)
        sub = lax.broadcasted_iota(jnp.int32, (SUBLANES, W_MIX), 0)
        groups = []
        for g in range(cps * chunk // SUBLANES):
            acc = bcast(g * SUBLANES + s - 1, SUBLANES)
            for p in range(1, SUBLANES // (2 * s)):
                acc = jnp.where(sub >= p * 2 * s, bcast(g * SUBLANES + p * 2 * s + s - 1, SUBLANES), acc)
            groups.append(acc)
        return jnp.concatenate(groups, axis=0)

    q = q_ref[...].astype(F32)
    row = lax.broadcasted_iota(jnp.int32, (cps * chunk, W_MIX), 0)
    x_ref[levels] = q.astype(BF16)
    x_ref[levels + 1] = kk.astype(BF16)
    for lv in range(levels):
        s = chunk >> (lv + 1)
        qside = ((row // s) % 2) == 1
        if s == 1:
            x_ref[lv] = jnp.where(qside, q * f, kk).astype(BF16)
        else:
            d = bcum - boundary_rows(s)
            x_ref[lv] = (jnp.where(qside, q, kk) * jnp.exp(jnp.where(qside, d, -d))).astype(BF16)
    b_last = jnp.concatenate([bcast(rc.stop - 1, chunk) for rc in chunks], axis=0)
    x_ref[levels + 2] = (q * jnp.exp(bcum)).astype(BF16)
    x_ref[levels + 3] = (kk * jnp.exp(b_last - bcum)).astype(BF16)
    decays = [jnp.exp(e_ref[rc.stop - 1:rc.stop, :]) for rc in chunks]
    nw = nw_ref[...]

    heads = [slice(h * HEAD_DIM, (h + 1) * HEAD_DIM) for h in range(N_HEADS)]
    amats = {}
    for c, rc in enumerate(chunks):
        for h, hs in enumerate(heads):
            amat = mask_ref[levels] * _dot_nt(x_ref[levels, rc, hs], x_ref[levels + 1, rc, hs])
            for lv in range(levels):
                x = x_ref[lv, rc, hs]
                amat = amat + mask_ref[lv] * _dot_nt(x, x)
            amats[c, h] = amat.astype(BF16)
    states = [st_ref[h] for h in range(N_HEADS)]
    outs = {}
    for c, rc in enumerate(chunks):
        for h, hs in enumerate(heads):
            outs[c, h] = (_dot(amats[c, h], i_ref[rc, hs])
                          + _dot_nt(x_ref[levels + 2, rc, hs], states[h].astype(BF16)))
        for h, hs in enumerate(heads):
            states[h] = states[h] * decays[c][:, hs] + _dot_tn(i_ref[rc, hs], x_ref[levels + 3, rc, hs])
    for h in range(N_HEADS):
        st_ref[h] = states[h]
    for c, rc in enumerate(chunks):
        for h, hs in enumerate(heads):
            y_ref[rc, hs] = (_silu(g_ref[rc, hs].astype(F32)) * _rms(outs[c, h], nw)).astype(y_ref.dtype)

    @pl.when(t == pl.num_programs(1) - 1)
    def _():
        for h in range(N_HEADS):
            sout_ref[h] = st_ref[h].T


def hgrn2(proj, fg, log_lb, log1m_lb, norm_w, s0, s0_layer, batch, seq, chunk, cps):
    n_steps = seq // (chunk * cps)
    rows = chunk * cps
    tril_np, mask_np, levels = _hgrn_level_masks(chunk)
    tril = jnp.asarray(tril_np, dtype=BF16)
    masks = jnp.asarray(mask_np)

    def col(c):
        return pl.BlockSpec((rows, W_MIX), lambda b, t: (b * n_steps + t, c))

    def const(shape):
        return pl.BlockSpec(shape, lambda b, t: (0,) * len(shape))

    kernel = functools.partial(_hgrn_kernel, chunk=chunk, levels=levels, cps=cps)
    y, s_out = pl.pallas_call(
        kernel,
        grid=(batch, n_steps),
        in_specs=[
            col(COL_QG), col(0), col(COL_IG), col(COL_GG),
            const((1, W_MIX)), const((1, W_MIX)),
            const((chunk, chunk)),
            const((levels + 1, chunk, chunk)),
            const((1, HEAD_DIM)),
            _state_spec(s0_layer),
        ],
        out_specs=[
            pl.BlockSpec((rows, W_MIX), lambda b, t: (b * n_steps + t, 0)),
            pl.BlockSpec((None, N_HEADS, HEAD_DIM, HEAD_DIM), lambda b, t: (b, 0, 0, 0)),
        ],
        out_shape=[
            jax.ShapeDtypeStruct((batch * seq, W_MIX), BF16),
            jax.ShapeDtypeStruct((batch, N_HEADS, HEAD_DIM, HEAD_DIM), F32),
        ],
        scratch_shapes=[
            pltpu.VMEM((N_HEADS, HEAD_DIM, HEAD_DIM), F32),
            pltpu.VMEM((rows, W_MIX), F32),
            pltpu.VMEM((levels + 4, rows, W_MIX), BF16),
        ],
        compiler_params=_params("parallel", "arbitrary"),
        name="hgrn2",
    )(proj, fg, proj, proj, log_lb, log1m_lb, tril, masks, norm_w.reshape(1, HEAD_DIM), s0)
    return y, s_out


def _merge_kernel(x_ref, gpre_ref, ya_ref, yr_ref, yg_ref, wg0_ref, wg1_ref, wg2_ref, wb_ref, wo_ref,
                  gpost_ref, o_ref, h_ref, *, tm, rs):
    j = pl.program_id(1)

    @pl.when(j == 0)
    def _():
        h_ref[...] = _rms(x_ref[...], gpre_ref[...]).astype(BF16)
        o_ref[...] = jnp.zeros_like(o_ref)

    for r in range(tm // rs):
        rows = slice(r * rs, (r + 1) * rs)
        h = h_ref[rows, :]
        merged = None
        for n, (y_ref, wg_ref) in enumerate(((ya_ref, wg0_ref), (yr_ref, wg1_ref), (yg_ref, wg2_ref))):
            gate = jax.nn.sigmoid(_dot(h, wg_ref[...]))
            term = gate * _dot(y_ref[rows, :], wb_ref[n])
            merged = term if merged is None else merged + term
        o_ref[rows, :] += _dot(merged.astype(BF16), wo_ref[...])

    @pl.when(j == pl.num_programs(1) - 1)
    def _():
        o_ref[...] = x_ref[...] + _rms(o_ref[...], gpost_ref[...])


def merge(x, g_pre, ya, yr, yg, w_gate, w_branch, w_out, g_post, layer, tm, tn):
    rows, d = x.shape
    nj = d // tn

    def gate_spec(n):
        return pl.BlockSpec((None, d, tn), lambda i, j: (layer, 0, n * nj + j))

    kernel = functools.partial(_merge_kernel, tm=tm, rs=min(tm, ROW_SUB))
    single = pl.Buffered(1)
    return pl.pallas_call(
        kernel,
        grid=(rows // tm, nj),
        in_specs=[
            pl.BlockSpec((tm, d), lambda i, j: (i, 0), pipeline_mode=single),
            pl.BlockSpec((1, d), lambda i, j: (0, 0)),
            pl.BlockSpec((tm, W_MIX), lambda i, j: (i, 0), pipeline_mode=single),
            pl.BlockSpec((tm, W_MIX), lambda i, j: (i, 0), pipeline_mode=single),
            pl.BlockSpec((tm, W_MIX), lambda i, j: (i, 0), pipeline_mode=single),
            gate_spec(0), gate_spec(1), gate_spec(2),
            pl.BlockSpec((None, N_BRANCH, W_MIX, tn), lambda i, j: (layer, 0, 0, j)),
            pl.BlockSpec((None, tn, d), lambda i, j: (layer, j, 0)),
            pl.BlockSpec((1, d), lambda i, j: (0, 0)),
        ],
        out_specs=pl.BlockSpec((tm, d), lambda i, j: (i, 0)),
        out_shape=jax.ShapeDtypeStruct((rows, d), F32),
        scratch_shapes=[pltpu.VMEM((tm, d), BF16)],
        compiler_params=_params("parallel", "arbitrary"),
        name="merge",
    )(x, g_pre, ya, yr, yg, w_gate, w_gate, w_gate, w_branch, w_out, g_post)


HALO = BF16_ROWS


def _ffn_kernel(x_ref, xprev_ref, cst_ref, gpre_ref, wup_ref, wval_ref, cw_ref, cb_ref, wdn_ref, gpost_ref,
                o_ref, uplast_ref, h_ref, up_ref, *, tm, seq):
    i = pl.program_id(0)
    j = pl.program_id(1)
    seqs_per_tile = max(tm // seq, 1)
    tiles_per_seq = max(seq // tm, 1)

    @pl.when(j == 0)
    def _():
        g = gpre_ref[...]
        h_ref[0:HALO, :] = _rms(xprev_ref[...], g).astype(BF16)
        h_ref[HALO:HALO + tm, :] = _rms(x_ref[...], g).astype(BF16)
        o_ref[...] = jnp.zeros_like(o_ref)

    cw = cw_ref[...]
    rs = min(tm, ROW_SUB)
    n_sub = tm // rs

    def up_block(r):
        lo = 0 if r == 0 else HALO + r * rs
        hi = HALO + (r + 1) * rs
        up_ref[lo:hi, :] = _dot(h_ref[lo:hi, :], wup_ref[...])

    up_block(0)
    if seqs_per_tile == 1:
        first = i % tiles_per_seq == 0
        halo_rows = slice(HALO - (CONV_W - 1), HALO)
        up_ref[halo_rows, :] = jnp.where(first, cst_ref[0], up_ref[halo_rows, :])
    for r in range(n_sub):
        if r + 1 < n_sub:
            up_block(r + 1)
        r0 = HALO + r * rs
        tap2 = up_ref[r0 - 2:r0 - 2 + rs, :]
        tap1 = up_ref[r0 - 1:r0 - 1 + rs, :]
        tap0 = up_ref[r0:r0 + rs, :]
        if seqs_per_tile > 1:
            tf = tap0.shape[1]
            st = cst_ref[...]
            st0 = jnp.broadcast_to(st[:, 0:1, :], (seqs_per_tile, seq, tf)).reshape(tm, tf)
            st1 = jnp.broadcast_to(st[:, 1:2, :], (seqs_per_tile, seq, tf)).reshape(tm, tf)
            pos = lax.broadcasted_iota(jnp.int32, (tm, tf), 0) % seq
            tap1 = jnp.where(pos == 0, st1, tap1)
            tap2 = jnp.where(pos == 0, st0, jnp.where(pos == 1, st1, tap2))
        conv = cb_ref[...] + tap2 * cw[0:1, :] + tap1 * cw[1:2, :] + tap0 * cw[2:3, :]
        val = _dot(h_ref[r0:r0 + rs, :], wval_ref[...])
        act = 0.5 * conv * (1.0 + lax.erf(conv * math.sqrt(0.5))) * val
        o_ref[r * rs:(r + 1) * rs, :] += _dot(act.astype(BF16), wdn_ref[...])

    keep = uplast_ref.shape[0]
    uplast_ref[...] = up_ref[HALO + tm - keep:HALO + tm, :]

    @pl.when(j == pl.num_programs(1) - 1)
    def _():
        o_ref[...] = x_ref[...] + _rms(o_ref[...], gpost_ref[...])


def ffn(x, conv_state, cs_layer, g_pre, w_up, w_val, conv_w, conv_b, w_down, g_post, layer, batch, seq, tm, tf):
    rows, d = x.shape
    dff = w_up.shape[2]
    n_tiles = rows // tm
    seqs_per_tile = max(tm // seq, 1)
    tiles_per_seq = max(seq // tm, 1)
    assert rows == batch * seq and (tm % seq == 0 or seq % tm == 0)
    keep = tm if seqs_per_tile > 1 else SUBLANES
    halo_blocks = tm // HALO
    kernel = functools.partial(_ffn_kernel, tm=tm, seq=seq)
    out, uplast = pl.pallas_call(
        kernel,
        grid=(n_tiles, dff // tf),
        in_specs=[
            pl.BlockSpec((tm, d), lambda i, j: (i, 0), pipeline_mode=pl.Buffered(1)),
            pl.BlockSpec((HALO, d), lambda i, j: (jnp.maximum(i * halo_blocks - 1, 0), 0)),
            pl.BlockSpec((None, seqs_per_tile, CONV_W - 1, tf),
                         lambda i, j: (cs_layer, i * seqs_per_tile // tiles_per_seq, 0, j)),
            pl.BlockSpec((1, d), lambda i, j: (0, 0)),
            pl.BlockSpec((None, d, tf), lambda i, j: (layer, 0, j)),
            pl.BlockSpec((None, d, tf), lambda i, j: (layer, 0, j)),
            pl.BlockSpec((CONV_W, tf), lambda i, j: (0, j)),
            pl.BlockSpec((1, tf), lambda i, j: (0, j)),
            pl.BlockSpec((None, tf, d), lambda i, j: (layer, j, 0)),
            pl.BlockSpec((1, d), lambda i, j: (0, 0)),
        ],
        out_specs=[
            pl.BlockSpec((tm, d), lambda i, j: (i, 0)),
            pl.BlockSpec((keep, tf), lambda i, j: (i, j)),
        ],
        out_shape=[
            jax.ShapeDtypeStruct((rows, d), F32),
            jax.ShapeDtypeStruct((n_tiles * keep, dff), F32),
        ],
        scratch_shapes=[
            pltpu.VMEM((HALO + tm, d), BF16),
            pltpu.VMEM((HALO + tm, tf), F32),
        ],
        compiler_params=_params("parallel", "arbitrary"),
        name="ffn",
    )(x, x, conv_state, g_pre, w_up, w_val, conv_w, conv_b, w_down, g_post)
    new_conv = uplast.reshape(batch, -1, dff)[:, -(CONV_W - 1):, :]
    return out, new_conv


def _layer(x, batch, seq, pos0, layer, caches, cache_layer, vecs, mats, tiles):
    k_past, v_past, ret_s0, hgrn_s0, conv_s0 = caches
    (g_mix_pre, g_mix_post, g_ffn_pre, g_ffn_post, rel_bias, ret_norm_w, log_lb, log1m_lb, hgrn_norm_w,
     conv_w, conv_b) = vecs
    w_in, w_gate, w_branch, w_out, w_up, w_val, w_down = mats
    rows = batch * seq
    tn = W_MIX
    proj, fg = norm_proj(x, g_mix_pre, w_in, layer, tiles["tm_proj"], tn, f32_cols=(COL_FG,))

    chunk = min(CHUNK, seq)
    past = ATT_PAST_ROWS
    hps = tiles["att_heads"]
    hb = N_HEADS // hps
    if k_past is None:
        keep = min(past, seq)
        (kv_new,) = norm_proj(x, g_mix_pre, w_in, layer, keep, tn, row_block0=seq // keep - 1, n_row_blocks=1,
                              col_block0=COL_KA, n_col_blocks=2, out_dtype=F32)
        group, sub = tiles["att_group"], tiles["att_sub"]
        gp = group // past
        tables = _attention_tables(rel_bias, sub, chunk, past, pos0, group)
        y_a = attention(
            proj, lambda h, g: (g, h),
            proj, (), lambda h, g: (jnp.maximum(g * gp - 1, 0), hb + h),
            proj, lambda h, g: (g, hb + h),
            proj, lambda h, g: (jnp.maximum(g * gp - 1, 0), 2 * hb + h),
            proj, lambda h, g: (g, 2 * hb + h),
            tables, seq // group, group, sub, past, hps)
    else:
        keep = seq
        (kv_new,) = norm_proj(x, g_mix_pre, w_in, layer, rows, tn, col_block0=COL_KA, n_col_blocks=2,
                              out_dtype=F32)
        group = sub = seq
        tables = _attention_tables(rel_bias, sub, chunk, past, pos0, group)
        n_l = k_past.shape[0]
        kp = k_past.reshape(n_l, batch * past, W_MIX)
        vp = v_past.reshape(n_l, batch * past, W_MIX)
        y_a = attention(
            proj, lambda h, g: (g, h),
            kp, (None,), lambda h, g: (cache_layer, g, h),
            proj, lambda h, g: (g, hb + h),
            vp, lambda h, g: (cache_layer, g, h),
            proj, lambda h, g: (g, 2 * hb + h),
            tables, batch, group, sub, past, hps)

    y_r, ret_s = retention(proj, ret_norm_w, ret_s0, cache_layer, batch, seq, tiles["ret_chunk"], pos0)
    y_g, hgrn_s = hgrn2(proj, fg, log_lb, log1m_lb, hgrn_norm_w, hgrn_s0, cache_layer, batch, seq, chunk,
                          tiles["hgrn_cps"])

    x = merge(x, g_mix_pre, y_a, y_r, y_g, w_gate, w_branch, w_out, g_mix_post, layer,
              tiles["tm_merge"], tiles["tn_merge"])
    x, new_conv = ffn(x, conv_s0, cache_layer, g_ffn_pre, w_up, w_val, conv_w, conv_b, w_down, g_ffn_post, layer,
                      batch, seq, tiles["tm_ffn"], tiles["tf_ffn"])
    k_new = kv_new[:, :W_MIX].reshape(batch, keep, N_HEADS, HEAD_DIM)
    v_new = kv_new[:, W_MIX:].reshape(batch, keep, N_HEADS, HEAD_DIM)
    return x, k_new, v_new, ret_s, hgrn_s, new_conv


PROMPT_TILES = dict(tm_proj=1024, att_group=1024, att_sub=256, att_heads=1, ret_chunk=256, hgrn_cps=2, tm_merge=1024, tn_merge=256, tm_ffn=1024, tf_ffn=512)


def _sample_tiles(rows, seq):
    return dict(tm_proj=rows, att_heads=N_HEADS, ret_chunk=seq, hgrn_cps=1, tm_merge=rows, tn_merge=256, tm_ffn=rows, tf_ffn=512)


def kernel(x_prompt, x_sample, cache_attn_k, cache_attn_v, state_ret, state_hgrn, state_ffn_conv,
           norm_mix_pre, norm_mix_post, norm_ffn_pre, norm_ffn_post, w_in, attn_rel_bias, ret_norm_w,
           hgrn_lower_bound, hgrn_norm_w, w_gate, w_branch, w_out, ffn_w_up, ffn_w_val, ffn_conv_w,
           ffn_conv_b, ffn_w_down):
    depth = w_in.shape[0]
    bp, tp, d = x_prompt.shape
    bs, ts, _ = x_sample.shape

    lb_cum = jnp.cumsum(jax.nn.softmax(hgrn_lower_bound.astype(F32), axis=0), axis=0)
    lb_layers = lb_cum - lb_cum[0:1]
    log_lb = jnp.log(lb_layers)
    log1m_lb = jnp.log1p(-lb_layers)

    mats = (w_in,) + tuple(w.astype(BF16) for w in (w_gate, w_branch, w_out, ffn_w_up, ffn_w_val, ffn_w_down))

    zs = jnp.zeros((1, bp, N_HEADS, HEAD_DIM, HEAD_DIM), F32)
    zc = jnp.zeros((1, bp, CONV_W - 1, D_FF), F32)
    caches_p = (None, None, zs, zs, zc)
    caches_s = (cache_attn_k, cache_attn_v, state_ret, state_hgrn, state_ffn_conv)

    xp = x_prompt.reshape(bp * tp, d)
    xs = x_sample.reshape(bs * ts, d)
    outs_p = [[] for _ in range(5)]
    outs_s = [[] for _ in range(5)]
    for l in range(depth):
        vecs = (norm_mix_pre[l][None], norm_mix_post[l][None], norm_ffn_pre[l][None], norm_ffn_post[l][None],
                attn_rel_bias[l], ret_norm_w[l], log_lb[l][None], log1m_lb[l][None], hgrn_norm_w[l],
                ffn_conv_w[l], ffn_conv_b[l][None])
        res_p = _layer(xp, bp, tp, 0, l, caches_p, 0, vecs, mats, PROMPT_TILES)
        res_s = _layer(xs, bs, ts, PAST_LEN, l, caches_s, l, vecs, mats, _sample_tiles(bs * ts, ts))
        xp, xs = res_p[0], res_s[0]
        for lst, val in zip(outs_p, res_p[1:]):
            lst.append(val)
        for lst, val in zip(outs_s, res_s[1:]):
            lst.append(val)

    return (xp.reshape(bp, tp, d), xs.reshape(bs, ts, d),
            *(jnp.stack(o) for o in outs_p), *(jnp.stack(o) for o in outs_s))
```

```python
import functools
import math

import numpy as np
import jax
import jax.numpy as jnp
from jax import lax
from jax.experimental import pallas as pl
from jax.experimental.pallas import tpu as pltpu

F32 = jnp.float32
BF16 = jnp.bfloat16

D_MODEL = 2048
CHUNK = 64
BAND_CHUNKS = 8
ATT_PAST_ROWS = BAND_CHUNKS * CHUNK
W_MIX = D_MODEL // 2
N_HEADS = 8
HEAD_DIM = W_MIX // N_HEADS
REL_CLIP = 128
N_BRANCH = 3
N_IN = 11 * W_MIX
D_FF = 5632
CONV_W = 3
ROPE_BASE = 10000.0
EPS = 1e-6
PAST_LEN = 1024

COL_QA, COL_KA, COL_VA, COL_QR, COL_KR, COL_VR, COL_GR, COL_QG, COL_FG, COL_IG, COL_GG = range(11)

V7X_VMEM_BYTES = 64 * 1024 * 1024
VMEM_LIMIT = V7X_VMEM_BYTES - 4 * 1024 * 1024
SUBLANES = 8
LANES = 128
BF16_ROWS = 16
ROW_SUB = 512
PROLOGUE_ROWS = 256

NT_DIMS = (((1,), (1,)), ((), ()))
TN_DIMS = (((0,), (0,)), ((), ()))


def _params(*sem):
    return pltpu.CompilerParams(dimension_semantics=sem, vmem_limit_bytes=VMEM_LIMIT)


def _rms(x, g):
    return x * lax.rsqrt(jnp.mean(x * x, axis=-1, keepdims=True) + EPS) * g


def _dot(a, b):
    return jnp.dot(a, b, preferred_element_type=F32)


def _dot_nt(a, b):
    return lax.dot_general(a, b, NT_DIMS, preferred_element_type=F32)


def _dot_tn(a, b):
    return lax.dot_general(a, b, TN_DIMS, preferred_element_type=F32)


def _silu(x):
    return x * jax.nn.sigmoid(x)


def _row_blocks(n):
    step = min(n, PROLOGUE_ROWS)
    return [slice(r, r + step) for r in range(0, n, step)]


def _norm_proj_kernel(x_ref, g_ref, w_ref, o_ref, *rest, f32_cols):
    f32_refs, h_ref = rest[:-1], rest[-1]
    j = pl.program_id(1)

    @pl.when(j == 0)
    def _():
        for rows in _row_blocks(x_ref.shape[0]):
            h_ref[rows, :] = _rms(x_ref[rows, :], g_ref[...]).astype(BF16)

    acc = _dot(h_ref[...], w_ref[...].astype(BF16))
    o_ref[...] = acc.astype(o_ref.dtype)
    for col, ref in zip(f32_cols, f32_refs):
        @pl.when(j == col)
        def _():
            ref[...] = acc


def norm_proj(x, g, w, layer, tm, tn, f32_cols=(), row_block0=0, n_row_blocks=None, col_block0=0,
              n_col_blocks=None, out_dtype=BF16):
    d = x.shape[1]
    n_row_blocks = x.shape[0] // tm - row_block0 if n_row_blocks is None else n_row_blocks
    n_col_blocks = w.shape[2] // tn - col_block0 if n_col_blocks is None else n_col_blocks
    rows, n = n_row_blocks * tm, n_col_blocks * tn
    kernel = functools.partial(_norm_proj_kernel, f32_cols=tuple(f32_cols))
    outs = pl.pallas_call(
        kernel,
        grid=(n_row_blocks, n_col_blocks),
        in_specs=[
            pl.BlockSpec((tm, d), lambda i, j: (row_block0 + i, 0)),
            pl.BlockSpec((1, d), lambda i, j: (0, 0)),
            pl.BlockSpec((None, d, tn), lambda i, j: (layer, 0, col_block0 + j)),
        ],
        out_specs=[pl.BlockSpec((tm, tn), lambda i, j: (i, j))]
        + [pl.BlockSpec((tm, tn), lambda i, j: (i, 0)) for _ in f32_cols],
        out_shape=[jax.ShapeDtypeStruct((rows, n), out_dtype)]
        + [jax.ShapeDtypeStruct((rows, tn), F32) for _ in f32_cols],
        scratch_shapes=[pltpu.VMEM((tm, d), BF16)],
        compiler_params=_params("parallel", "arbitrary"),
        name="norm_proj",
    )(x, g, w)
    return outs


def _attention_kernel(q_ref, kp_ref, ko_ref, vp_ref, vo_ref, *rest, past, group, sub, hps):
    n_sub = group // sub
    bias_refs, (o_ref, kw_ref, vw_ref) = rest[:n_sub], rest[n_sub:]
    kw_ref[0:past, :] = kp_ref[...].astype(BF16)
    kw_ref[past:past + group, :] = ko_ref[...].astype(BF16)
    for hh in range(hps):
        hs = slice(hh * HEAD_DIM, (hh + 1) * HEAD_DIM)
        vs = slice(2 * hh * HEAD_DIM, (2 * hh + 1) * HEAD_DIM)
        vw_ref[0:past, vs] = vp_ref[:, hs].astype(BF16)
        vw_ref[past:past + group, vs] = vo_ref[:, hs].astype(BF16)
        vw_ref[:, (2 * hh + 1) * HEAD_DIM:(2 * hh + 2) * HEAD_DIM] = jnp.ones((past + group, HEAD_DIM), BF16)
    scale = HEAD_DIM ** -0.5
    span = past + sub
    steps = [(hh, s) for hh in range(hps) for s in range(n_sub)]
    scores = []
    for hh, s in steps:
        hs = slice(hh * HEAD_DIM, (hh + 1) * HEAD_DIM)
        q = (q_ref[s * sub:(s + 1) * sub, hs].astype(F32) * scale).astype(BF16)
        scores.append(_dot_nt(q, kw_ref[s * sub:s * sub + span, hs]) + bias_refs[s][hh])
    probs = [jnp.exp(sc - jnp.max(sc, axis=-1, keepdims=True)).astype(BF16) for sc in scores]
    for (hh, s), p in zip(steps, probs):
        ov = _dot(p, vw_ref[s * sub:s * sub + span, 2 * hh * HEAD_DIM:(2 * hh + 2) * HEAD_DIM])
        o_ref[s * sub:(s + 1) * sub, hh * HEAD_DIM:(hh + 1) * HEAD_DIM] = (
            ov[:, 0:HEAD_DIM] / ov[:, HEAD_DIM:2 * HEAD_DIM]).astype(o_ref.dtype)


def _toeplitz_rows(z, n_rows, width):
    m = z.shape[-1]
    flat = jnp.tile(z, (1, n_rows))[:, :n_rows * (m - 1)]
    return flat.reshape(z.shape[0], n_rows, m - 1)[:, :, :width]


def _n_first_group_variants(sub, past, pos0, group):
    return min(group // sub, max(0, -(-(past - pos0) // sub)))


def _attention_tables(rel_bias, sub, chunk, past, pos0, group):
    width = past + sub
    m = np.arange(width + sub - 1)
    idx = np.clip(past + (sub - 1) - m, -REL_CLIP, REL_CLIP) + REL_CLIP
    n_hi = int(np.sum(idx == idx[0]))
    n_lo = int(np.sum(idx == idx[-1]))
    mid = idx[n_hi:idx.size - n_lo]
    assert mid.size and np.all(np.diff(mid) == -1)
    v = jnp.concatenate([jnp.broadcast_to(rel_bias[:, idx[0]:idx[0] + 1], (N_HEADS, n_hi)),
                         jnp.flip(rel_bias[:, int(mid[-1]):int(mid[0]) + 1], axis=1),
                         jnp.broadcast_to(rel_bias[:, idx[-1]:idx[-1] + 1], (N_HEADS, n_lo))], axis=1)
    z = jnp.concatenate([v[:, sub - 1:], jnp.zeros((N_HEADS, 1), F32), v[:, :sub - 1]], axis=1)
    toe = _toeplitz_rows(z, sub, width)
    i = np.arange(sub)[:, None]
    j = np.arange(width)[None, :]
    band = (j >= (i // chunk) * chunk) & (j < (i // chunk) * chunk + past + chunk)
    masks = [band]
    for s in range(_n_first_group_variants(sub, past, pos0, group)):
        masks.append(band & (j + (pos0 - past + s * sub) >= 0))
    neg = np.where(np.stack(masks), 0.0, -np.inf).astype(np.float32)
    return toe[None] + jnp.asarray(neg)[:, None]


def attention(q_arr, q_map, kp_arr, kp_lead, kp_map, ko_arr, ko_map, vp_arr, vp_map, vo_arr, vo_map, tables,
              n_groups, group, sub, past, hps):
    rows = n_groups * group
    n_sub = group // sub
    n_var = tables.shape[0]
    kernel = functools.partial(_attention_kernel, past=past, group=group, sub=sub, hps=hps)
    width = past + sub
    lanes = hps * HEAD_DIM

    def table_spec(s):
        if s >= n_var - 1:
            return pl.BlockSpec((None, hps, sub, width), lambda h, g: (0, h, 0, 0))
        return pl.BlockSpec((None, hps, sub, width), lambda h, g: (jnp.where(g == 0, 1 + s, 0), h, 0, 0))

    return pl.pallas_call(
        kernel,
        grid=(N_HEADS // hps, n_groups),
        in_specs=[
            pl.BlockSpec((group, lanes), q_map),
            pl.BlockSpec(kp_lead + (past, lanes), kp_map),
            pl.BlockSpec((group, lanes), ko_map),
            pl.BlockSpec(kp_lead + (past, lanes), vp_map),
            pl.BlockSpec((group, lanes), vo_map),
        ] + [table_spec(s) for s in range(n_sub)],
        out_specs=pl.BlockSpec((group, lanes), lambda h, g: (g, h)),
        out_shape=jax.ShapeDtypeStruct((rows, W_MIX), BF16),
        scratch_shapes=[pltpu.VMEM((past + group, lanes), BF16),
                        pltpu.VMEM((past + group, 2 * lanes), BF16)],
        compiler_params=_params("parallel", "arbitrary"),
        name="attention",
    )(q_arr, kp_arr, ko_arr, vp_arr, vo_arr, *([tables] * n_sub))


def _retention_kernel(q_ref, k_ref, v_ref, g_ref, cos_ref, sin_ref, di_ref, dq_ref, dk_ref, ds_ref,
                      nw_ref, s0_ref, y_ref, sout_ref, s_ref):
    t = pl.program_id(1)

    @pl.when(t == 0)
    def _():
        s_ref[...] = s0_ref[...]

    cosf = cos_ref[...]
    sinf = sin_ref[...]
    nw = nw_ref[...]
    qscale = HEAD_DIM ** -0.5

    heads = [slice(h * HEAD_DIM, (h + 1) * HEAD_DIM) for h in range(N_HEADS)]
    qs, ks, qds, kds = [], [], [], []
    for h, hs in enumerate(heads):
        q = q_ref[:, hs].astype(F32)
        k = k_ref[:, hs].astype(F32)
        q = (q * cosf + pltpu.roll(q, HEAD_DIM // 2, 1) * sinf) * qscale
        k = k * cosf + pltpu.roll(k, HEAD_DIM // 2, 1) * sinf
        qs.append(q.astype(BF16))
        ks.append(k.astype(BF16))
        qds.append((q * dq_ref[h]).astype(BF16))
        kds.append((k * dk_ref[h]).astype(BF16))
    amats = [(_dot_nt(qs[h], ks[h]) * di_ref[h]).astype(BF16) for h in range(N_HEADS)]
    outs = [_dot(amats[h], v_ref[:, hs]) + _dot(qds[h], s_ref[h].astype(BF16)) for h, hs in enumerate(heads)]
    for h, hs in enumerate(heads):
        s_ref[h] = s_ref[h] * ds_ref[h] + _dot_tn(kds[h], v_ref[:, hs])
    for h, hs in enumerate(heads):
        o = outs[h]
        mu = jnp.mean(o, axis=-1, keepdims=True)
        oc = o - mu
        var = jnp.mean(oc * oc, axis=-1, keepdims=True)
        on = oc * lax.rsqrt(var + EPS) * nw
        y_ref[:, hs] = (_silu(g_ref[:, hs].astype(F32)) * on).astype(y_ref.dtype)

    @pl.when(t == pl.num_programs(1) - 1)
    def _():
        sout_ref[...] = s_ref[...]


def _retention_tables(chunk, pos):
    log_gamma = jnp.log1p(-(2.0 ** (-5.0 - jnp.arange(N_HEADS, dtype=F32))))
    idx = jnp.arange(chunk, dtype=F32)
    diff = idx[:, None] - idx[None, :]
    d_intra = jnp.where(diff[None] >= 0, jnp.exp(jnp.maximum(diff, 0.0)[None] * log_gamma[:, None, None]), 0.0)
    d_q = jnp.exp((idx + 1.0)[None, :] * log_gamma[:, None])
    d_k = jnp.exp((chunk - 1.0 - idx)[None, :] * log_gamma[:, None])
    d_s = jnp.exp(chunk * log_gamma)
    d_q = jnp.broadcast_to(d_q[:, :, None], (N_HEADS, chunk, HEAD_DIM))
    d_k = jnp.broadcast_to(d_k[:, :, None], (N_HEADS, chunk, HEAD_DIM))
    d_s = jnp.broadcast_to(d_s[:, None, None], (N_HEADS, HEAD_DIM, HEAD_DIM))
    inv = ROPE_BASE ** (-jnp.arange(0, HEAD_DIM, 2, dtype=F32) / HEAD_DIM)
    ang = pos.astype(F32)[:, None] * inv[None, :]
    cos, sin = jnp.cos(ang), jnp.sin(ang)
    cosf = jnp.concatenate([cos, cos], axis=-1)
    sinf = jnp.concatenate([-sin, sin], axis=-1)
    return d_intra, d_q, d_k, d_s, cosf, sinf


def _state_spec(layer):
    return pl.BlockSpec((None, None, N_HEADS, HEAD_DIM, HEAD_DIM), lambda b, t: (layer, b, 0, 0, 0))


def retention(proj, norm_w, s0, s0_layer, batch, seq, chunk, pos0):
    n_chunks = seq // chunk
    d_intra, d_q, d_k, d_s, cosf, sinf = _retention_tables(chunk, pos0 + jnp.arange(seq))

    def col(c):
        return pl.BlockSpec((chunk, W_MIX), lambda b, t: (b * n_chunks + t, c))

    def const(shape):
        return pl.BlockSpec(shape, lambda b, t: (0,) * len(shape))

    y, s_out = pl.pallas_call(
        _retention_kernel,
        grid=(batch, n_chunks),
        in_specs=[
            col(COL_QR), col(COL_KR), col(COL_VR), col(COL_GR),
            pl.BlockSpec((chunk, HEAD_DIM), lambda b, t: (t, 0)),
            pl.BlockSpec((chunk, HEAD_DIM), lambda b, t: (t, 0)),
            const((N_HEADS, chunk, chunk)),
            const((N_HEADS, chunk, HEAD_DIM)),
            const((N_HEADS, chunk, HEAD_DIM)),
            const((N_HEADS, HEAD_DIM, HEAD_DIM)),
            const((1, HEAD_DIM)),
            _state_spec(s0_layer),
        ],
        out_specs=[
            pl.BlockSpec((chunk, W_MIX), lambda b, t: (b * n_chunks + t, 0)),
            pl.BlockSpec((None, N_HEADS, HEAD_DIM, HEAD_DIM), lambda b, t: (b, 0, 0, 0)),
        ],
        out_shape=[
            jax.ShapeDtypeStruct((batch * seq, W_MIX), BF16),
            jax.ShapeDtypeStruct((batch, N_HEADS, HEAD_DIM, HEAD_DIM), F32),
        ],
        scratch_shapes=[pltpu.VMEM((N_HEADS, HEAD_DIM, HEAD_DIM), F32)],
        compiler_params=_params("parallel", "arbitrary"),
        name="retention",
    )(proj, proj, proj, proj, cosf, sinf, d_intra, d_q, d_k, d_s, norm_w.reshape(1, HEAD_DIM), s0)
    return y, s_out


def _hgrn_level_masks(chunk):
    levels = int(math.log2(chunk))
    n = np.arange(chunk)
    masks = []
    for lv in range(levels):
        s = chunk >> (lv + 1)
        pair = n // (2 * s)
        qside = (n // s) % 2 == 1
        masks.append((qside[:, None] & ~qside[None, :] & (pair[:, None] == pair[None, :])).astype(np.float32))
    masks.append(np.eye(chunk, dtype=np.float32))
    return np.tril(np.ones((chunk, chunk), np.float32)), np.stack(masks, axis=0), levels


def _hgrn_kernel(q_ref, f_ref, i_ref, g_ref, loglb_ref, log1mlb_ref, tril_ref, mask_ref, nw_ref, s0_ref,
                 y_ref, sout_ref, st_ref, e_ref, x_ref, *, chunk, levels, cps):
    t = pl.program_id(1)

    @pl.when(t == 0)
    def _():
        for h in range(N_HEADS):
            st_ref[h] = s0_ref[h].T

    fg = f_ref[...]
    log_sig = jnp.minimum(fg, 0.0) - jnp.log(1.0 + jnp.exp(-jnp.abs(fg)))
    a = loglb_ref[...]
    b = log1mlb_ref[...] + log_sig
    logf = jnp.maximum(a, b) + jnp.log(1.0 + jnp.exp(-jnp.abs(a - b)))
    f = jnp.exp(logf)
    kk = 1.0 - f
    hi = logf.astype(BF16)
    r1 = logf - hi.astype(F32)
    mid = r1.astype(BF16)
    lo = (r1 - mid.astype(F32)).astype(BF16)
    tril = tril_ref[...]
    chunks = [slice(c * chunk, (c + 1) * chunk) for c in range(cps)]
    for rc in chunks:
        e_ref[rc, :] = _dot(tril, hi[rc]) + _dot(tril, mid[rc]) + _dot(tril, lo[rc])
    bcum = e_ref[...]

    def bcast(r, n_rows):
        return jnp.broadcast_to(e_ref[r:r + 1, :], (n_rows, W_MIX))

    def boundary_rows(s):
        if 2 * s >= SUBLANES:
            return jnp.concatenate([bcast(p * 2 * s + s - 1, 2 * s) for p in range(cps * chunk // (2 * s))], axis=0)
        sub = lax.broadcasted_iota(jnp.int32, (SUBLANES, W_MIX), 0)
        groups = []
        for g in range(cps * chunk // SUBLANES):
            acc = bcast(g * SUBLANES + s - 1, SUBLANES)
            for p in range(1, SUBLANES // (2 * s)):
                acc = jnp.where(sub >= p * 2 * s, bcast(g * SUBLANES + p * 2 * s + s - 1, SUBLANES), acc)
            groups.append(acc)
        return jnp.concatenate(groups, axis=0)

    q = q_ref[...].astype(F32)
    row = lax.broadcasted_iota(jnp.int32, (cps * chunk, W_MIX), 0)
    x_ref[levels] = q.astype(BF16)
    x_ref[levels + 1] = kk.astype(BF16)
    for lv in range(levels):
        s = chunk >> (lv + 1)
        qside = ((row // s) % 2) == 1
        if s == 1:
            x_ref[lv] = jnp.where(qside, q * f, kk).astype(BF16)
        else:
            d = bcum - boundary_rows(s)
            x_ref[lv] = (jnp.where(qside, q, kk) * jnp.exp(jnp.where(qside, d, -d))).astype(BF16)
    b_last = jnp.concatenate([bcast(rc.stop - 1, chunk) for rc in chunks], axis=0)
    x_ref[levels + 2] = (q * jnp.exp(bcum)).astype(BF16)
    x_ref[levels + 3] = (kk * jnp.exp(b_last - bcum)).astype(BF16)
    decays = [jnp.exp(e_ref[rc.stop - 1:rc.stop, :]) for rc in chunks]
    nw = nw_ref[...]

    heads = [slice(h * HEAD_DIM, (h + 1) * HEAD_DIM) for h in range(N_HEADS)]
    amats = {}
    for c, rc in enumerate(chunks):
        for h, hs in enumerate(heads):
            amat = mask_ref[levels] * _dot_nt(x_ref[levels, rc, hs], x_ref[levels + 1, rc, hs])
            for lv in range(levels):
                x = x_ref[lv, rc, hs]
                amat = amat + mask_ref[lv] * _dot_nt(x, x)
            amats[c, h] = amat.astype(BF16)
    states = [st_ref[h] for h in range(N_HEADS)]
    outs = {}
    for c, rc in enumerate(chunks):
        for h, hs in enumerate(heads):
            outs[c, h] = (_dot(amats[c, h], i_ref[rc, hs])
                          + _dot_nt(x_ref[levels + 2, rc, hs], states[h].astype(BF16)))
        for h, hs in enumerate(heads):
            states[h] = states[h] * decays[c][:, hs] + _dot_tn(i_ref[rc, hs], x_ref[levels + 3, rc, hs])
    for h in range(N_HEADS):
        st_ref[h] = states[h]
    for c, rc in enumerate(chunks):
        for h, hs in enumerate(heads):
            y_ref[rc, hs] = (_silu(g_ref[rc, hs].astype(F32)) * _rms(outs[c, h], nw)).astype(y_ref.dtype)

    @pl.when(t == pl.num_programs(1) - 1)
    def _():
        for h in range(N_HEADS):
            sout_ref[h] = st_ref[h].T


def hgrn2(proj, fg, log_lb, log1m_lb, norm_w, s0, s0_layer, batch, seq, chunk, cps):
    n_steps = seq // (chunk * cps)
    rows = chunk * cps
    tril_np, mask_np, levels = _hgrn_level_masks(chunk)
    tril = jnp.asarray(tril_np, dtype=BF16)
    masks = jnp.asarray(mask_np)

    def col(c):
        return pl.BlockSpec((rows, W_MIX), lambda b, t: (b * n_steps + t, c))

    def const(shape):
        return pl.BlockSpec(shape, lambda b, t: (0,) * len(shape))

    kernel = functools.partial(_hgrn_kernel, chunk=chunk, levels=levels, cps=cps)
    y, s_out = pl.pallas_call(
        kernel,
        grid=(batch, n_steps),
        in_specs=[
            col(COL_QG), col(0), col(COL_IG), col(COL_GG),
            const((1, W_MIX)), const((1, W_MIX)),
            const((chunk, chunk)),
            const((levels + 1, chunk, chunk)),
            const((1, HEAD_DIM)),
            _state_spec(s0_layer),
        ],
        out_specs=[
            pl.BlockSpec((rows, W_MIX), lambda b, t: (b * n_steps + t, 0)),
            pl.BlockSpec((None, N_HEADS, HEAD_DIM, HEAD_DIM), lambda b, t: (b, 0, 0, 0)),
        ],
        out_shape=[
            jax.ShapeDtypeStruct((batch * seq, W_MIX), BF16),
            jax.ShapeDtypeStruct((batch, N_HEADS, HEAD_DIM, HEAD_DIM), F32),
        ],
        scratch_shapes=[
            pltpu.VMEM((N_HEADS, HEAD_DIM, HEAD_DIM), F32),
            pltpu.VMEM((rows, W_MIX), F32),
            pltpu.VMEM((levels + 4, rows, W_MIX), BF16),
        ],
        compiler_params=_params("parallel", "arbitrary"),
        name="hgrn2",
    )(proj, fg, proj, proj, log_lb, log1m_lb, tril, masks, norm_w.reshape(1, HEAD_DIM), s0)
    return y, s_out


def _merge_kernel(x_ref, gpre_ref, ya_ref, yr_ref, yg_ref, wg0_ref, wg1_ref, wg2_ref, wb_ref, wo_ref,
                  gpost_ref, o_ref, h_ref, *, tm, rs):
    j = pl.program_id(1)

    @pl.when(j == 0)
    def _():
        for rows in _row_blocks(tm):
            h_ref[rows, :] = _rms(x_ref[rows, :], gpre_ref[...]).astype(BF16)
        o_ref[...] = jnp.zeros_like(o_ref)

    for r in range(tm // rs):
        rows = slice(r * rs, (r + 1) * rs)
        h = h_ref[rows, :]
        merged = None
        for n, (y_ref, wg_ref) in enumerate(((ya_ref, wg0_ref), (yr_ref, wg1_ref), (yg_ref, wg2_ref))):
            gate = jax.nn.sigmoid(_dot(h, wg_ref[...]))
            term = gate * _dot(y_ref[rows, :], wb_ref[n])
            merged = term if merged is None else merged + term
        o_ref[rows, :] += _dot(merged.astype(BF16), wo_ref[...])

    @pl.when(j == pl.num_programs(1) - 1)
    def _():
        for rows in _row_blocks(tm):
            o_ref[rows, :] = x_ref[rows, :] + _rms(o_ref[rows, :], gpost_ref[...])


def merge(x, g_pre, ya, yr, yg, w_gate, w_branch, w_out, g_post, layer, tm, tn):
    rows, d = x.shape
    nj = d // tn

    def gate_spec(n):
        return pl.BlockSpec((None, d, tn), lambda i, j: (layer, 0, n * nj + j))

    kernel = functools.partial(_merge_kernel, tm=tm, rs=min(tm, ROW_SUB))
    single = pl.Buffered(1)
    return pl.pallas_call(
        kernel,
        grid=(rows // tm, nj),
        in_specs=[
            pl.BlockSpec((tm, d), lambda i, j: (i, 0)),
            pl.BlockSpec((1, d), lambda i, j: (0, 0)),
            pl.BlockSpec((tm, W_MIX), lambda i, j: (i, 0), pipeline_mode=single),
            pl.BlockSpec((tm, W_MIX), lambda i, j: (i, 0), pipeline_mode=single),
            pl.BlockSpec((tm, W_MIX), lambda i, j: (i, 0), pipeline_mode=single),
            gate_spec(0), gate_spec(1), gate_spec(2),
            pl.BlockSpec((None, N_BRANCH, W_MIX, tn), lambda i, j: (layer, 0, 0, j)),
            pl.BlockSpec((None, tn, d), lambda i, j: (layer, j, 0)),
            pl.BlockSpec((1, d), lambda i, j: (0, 0)),
        ],
        out_specs=pl.BlockSpec((tm, d), lambda i, j: (i, 0)),
        out_shape=jax.ShapeDtypeStruct((rows, d), F32),
        scratch_shapes=[pltpu.VMEM((tm, d), BF16)],
        compiler_params=_params("parallel", "arbitrary"),
        name="merge",
    )(x, g_pre, ya, yr, yg, w_gate, w_gate, w_gate, w_branch, w_out, g_post)


HALO = BF16_ROWS


def _ffn_kernel(x_ref, xprev_ref, cst_ref, gpre_ref, wup_ref, wval_ref, cw_ref, cb_ref, wdn_ref, gpost_ref,
                o_ref, uplast_ref, h_ref, up_ref, *, tm, seq):
    i = pl.program_id(0)
    j = pl.program_id(1)
    seqs_per_tile = max(tm // seq, 1)
    tiles_per_seq = max(seq // tm, 1)

    @pl.when(j == 0)
    def _():
        g = gpre_ref[...]
        h_ref[0:HALO, :] = _rms(xprev_ref[...], g).astype(BF16)
        for rows in _row_blocks(tm):
            h_ref[HALO + rows.start:HALO + rows.stop, :] = _rms(x_ref[rows, :], g).astype(BF16)
        o_ref[...] = jnp.zeros_like(o_ref)

    cw = cw_ref[...]
    wdn = wdn_ref[...].astype(BF16)
    rs = min(tm, ROW_SUB)
    n_sub = tm // rs

    def up_block(r):
        lo = 0 if r == 0 else HALO + r * rs
        hi = HALO + (r + 1) * rs
        up_ref[lo:hi, :] = _dot(h_ref[lo:hi, :], wup_ref[...])

    up_block(0)
    if seqs_per_tile == 1:
        first = i % tiles_per_seq == 0
        halo_rows = slice(HALO - (CONV_W - 1), HALO)
        up_ref[halo_rows, :] = jnp.where(first, cst_ref[0], up_ref[halo_rows, :])
    for r in range(n_sub):
        if r + 1 < n_sub:
            up_block(r + 1)
        r0 = HALO + r * rs
        tap2 = up_ref[r0 - 2:r0 - 2 + rs, :]
        tap1 = up_ref[r0 - 1:r0 - 1 + rs, :]
        tap0 = up_ref[r0:r0 + rs, :]
        if seqs_per_tile > 1:
            tf = tap0.shape[1]
            st = cst_ref[...]
            st0 = jnp.broadcast_to(st[:, 0:1, :], (seqs_per_tile, seq, tf)).reshape(tm, tf)
            st1 = jnp.broadcast_to(st[:, 1:2, :], (seqs_per_tile, seq, tf)).reshape(tm, tf)
            pos = lax.broadcasted_iota(jnp.int32, (tm, tf), 0) % seq
            tap1 = jnp.where(pos == 0, st1, tap1)
            tap2 = jnp.where(pos == 0, st0, jnp.where(pos == 1, st1, tap2))
        conv = cb_ref[...] + tap2 * cw[0:1, :] + tap1 * cw[1:2, :] + tap0 * cw[2:3, :]
        val = _dot(h_ref[r0:r0 + rs, :], wval_ref[...])
        act = 0.5 * conv * (1.0 + lax.erf(conv * math.sqrt(0.5))) * val
        o_ref[r * rs:(r + 1) * rs, :] += _dot(act.astype(BF16), wdn)

    keep = uplast_ref.shape[0]
    uplast_ref[...] = up_ref[HALO + tm - keep:HALO + tm, :]

    @pl.when(j == pl.num_programs(1) - 1)
    def _():
        for rows in _row_blocks(tm):
            o_ref[rows, :] = x_ref[rows, :] + _rms(o_ref[rows, :], gpost_ref[...])


def ffn(x, conv_state, cs_layer, g_pre, w_up, w_val, conv_w, conv_b, w_down, g_post, layer, batch, seq, tm, tf):
    rows, d = x.shape
    dff = w_up.shape[2]
    n_tiles = rows // tm
    seqs_per_tile = max(tm // seq, 1)
    tiles_per_seq = max(seq // tm, 1)
    assert rows == batch * seq and (tm % seq == 0 or seq % tm == 0)
    keep = tm if seqs_per_tile > 1 else SUBLANES
    halo_blocks = tm // HALO
    kernel = functools.partial(_ffn_kernel, tm=tm, seq=seq)
    out, uplast = pl.pallas_call(
        kernel,
        grid=(n_tiles, dff // tf),
        in_specs=[
            pl.BlockSpec((tm, d), lambda i, j: (i, 0)),
            pl.BlockSpec((HALO, d), lambda i, j: (jnp.maximum(i * halo_blocks - 1, 0), 0)),
            pl.BlockSpec((None, seqs_per_tile, CONV_W - 1, tf),
                         lambda i, j: (cs_layer, i * seqs_per_tile // tiles_per_seq, 0, j)),
            pl.BlockSpec((1, d), lambda i, j: (0, 0)),
            pl.BlockSpec((None, d, tf), lambda i, j: (layer, 0, j)),
            pl.BlockSpec((None, d, tf), lambda i, j: (layer, 0, j)),
            pl.BlockSpec((CONV_W, tf), lambda i, j: (0, j)),
            pl.BlockSpec((1, tf), lambda i, j: (0, j)),
            pl.BlockSpec((None, tf, d), lambda i, j: (layer, j, 0)),
            pl.BlockSpec((1, d), lambda i, j: (0, 0)),
        ],
        out_specs=[
            pl.BlockSpec((tm, d), lambda i, j: (i, 0)),
            pl.BlockSpec((keep, tf), lambda i, j: (i, j)),
        ],
        out_shape=[
            jax.ShapeDtypeStruct((rows, d), F32),
            jax.ShapeDtypeStruct((n_tiles * keep, dff), F32),
        ],
        scratch_shapes=[
            pltpu.VMEM((HALO + tm, d), BF16),
            pltpu.VMEM((HALO + tm, tf), F32),
        ],
        compiler_params=_params("parallel", "arbitrary"),
        name="ffn",
    )(x, x, conv_state, g_pre, w_up, w_val, conv_w, conv_b, w_down, g_post)
    new_conv = uplast.reshape(batch, -1, dff)[:, -(CONV_W - 1):, :]
    return out, new_conv


def _layer(x, batch, seq, pos0, layer, caches, cache_layer, vecs, mats, tiles):
    k_past, v_past, ret_s0, hgrn_s0, conv_s0 = caches
    (g_mix_pre, g_mix_post, g_ffn_pre, g_ffn_post, rel_bias, ret_norm_w, log_lb, log1m_lb, hgrn_norm_w,
     conv_w, conv_b) = vecs
    w_in, w_gate, w_branch, w_out, w_up, w_val, w_down = mats
    rows = batch * seq
    tn = W_MIX
    proj, fg = norm_proj(x, g_mix_pre, w_in, layer, tiles["tm_proj"], tn, f32_cols=(COL_FG,))

    chunk = min(CHUNK, seq)
    past = ATT_PAST_ROWS
    hps = tiles["att_heads"]
    hb = N_HEADS // hps
    if k_past is None:
        keep = min(past, seq)
        (kv_new,) = norm_proj(x, g_mix_pre, w_in, layer, keep, tn, row_block0=seq // keep - 1, n_row_blocks=1,
                              col_block0=COL_KA, n_col_blocks=2, out_dtype=F32)
        group, sub = tiles["att_group"], tiles["att_sub"]
        gp = group // past
        tables = _attention_tables(rel_bias, sub, chunk, past, pos0, group)
        y_a = attention(
            proj, lambda h, g: (g, h),
            proj, (), lambda h, g: (jnp.maximum(g * gp - 1, 0), hb + h),
            proj, lambda h, g: (g, hb + h),
            proj, lambda h, g: (jnp.maximum(g * gp - 1, 0), 2 * hb + h),
            proj, lambda h, g: (g, 2 * hb + h),
            tables, seq // group, group, sub, past, hps)
    else:
        keep = seq
        (kv_new,) = norm_proj(x, g_mix_pre, w_in, layer, rows, tn, col_block0=COL_KA, n_col_blocks=2,
                              out_dtype=F32)
        group = sub = seq
        tables = _attention_tables(rel_bias, sub, chunk, past, pos0, group)
        n_l = k_past.shape[0]
        kp = k_past.reshape(n_l, batch * past, W_MIX)
        vp = v_past.reshape(n_l, batch * past, W_MIX)
        y_a = attention(
            proj, lambda h, g: (g, h),
            kp, (None,), lambda h, g: (cache_layer, g, h),
            proj, lambda h, g: (g, hb + h),
            vp, lambda h, g: (cache_layer, g, h),
            proj, lambda h, g: (g, 2 * hb + h),
            tables, batch, group, sub, past, hps)

    y_r, ret_s = retention(proj, ret_norm_w, ret_s0, cache_layer, batch, seq, tiles["ret_chunk"], pos0)
    y_g, hgrn_s = hgrn2(proj, fg, log_lb, log1m_lb, hgrn_norm_w, hgrn_s0, cache_layer, batch, seq, chunk,
                          tiles["hgrn_cps"])

    x = merge(x, g_mix_pre, y_a, y_r, y_g, w_gate, w_branch, w_out, g_mix_post, layer,
              tiles["tm_merge"], tiles["tn_merge"])
    x, new_conv = ffn(x, conv_s0, cache_layer, g_ffn_pre, w_up, w_val, conv_w, conv_b, w_down, g_ffn_post, layer,
                      batch, seq, tiles["tm_ffn"], tiles["tf_ffn"])
    k_new = kv_new[:, :W_MIX].reshape(batch, keep, N_HEADS, HEAD_DIM)
    v_new = kv_new[:, W_MIX:].reshape(batch, keep, N_HEADS, HEAD_DIM)
    return x, k_new, v_new, ret_s, hgrn_s, new_conv


PROMPT_TILES = dict(tm_proj=1024, att_group=1024, att_sub=256, att_heads=1, ret_chunk=256, hgrn_cps=2, tm_merge=1024, tn_merge=256, tm_ffn=1024, tf_ffn=512)


def _sample_tiles(rows, seq):
    return dict(tm_proj=rows, att_heads=N_HEADS, ret_chunk=seq, hgrn_cps=1, tm_merge=rows, tn_merge=256, tm_ffn=rows, tf_ffn=512)


def kernel(x_prompt, x_sample, cache_attn_k, cache_attn_v, state_ret, state_hgrn, state_ffn_conv,
           norm_mix_pre, norm_mix_post, norm_ffn_pre, norm_ffn_post, w_in, attn_rel_bias, ret_norm_w,
           hgrn_lower_bound, hgrn_norm_w, w_gate, w_branch, w_out, ffn_w_up, ffn_w_val, ffn_conv_w,
           ffn_conv_b, ffn_w_down):
    depth = w_in.shape[0]
    bp, tp, d = x_prompt.shape
    bs, ts, _ = x_sample.shape

    lb_cum = jnp.cumsum(jax.nn.softmax(hgrn_lower_bound.astype(F32), axis=0), axis=0)
    lb_layers = lb_cum - lb_cum[0:1]
    log_lb = jnp.log(lb_layers)
    log1m_lb = jnp.log1p(-lb_layers)

    mats = (w_in,) + tuple(w.astype(BF16) for w in (w_gate, w_branch, w_out, ffn_w_up, ffn_w_val)) + (ffn_w_down,)

    zs = jnp.zeros((1, bp, N_HEADS, HEAD_DIM, HEAD_DIM), F32)
    zc = jnp.zeros((1, bp, CONV_W - 1, D_FF), F32)
    caches_p = (None, None, zs, zs, zc)
    caches_s = (cache_attn_k, cache_attn_v, state_ret, state_hgrn, state_ffn_conv)

    xp = x_prompt.reshape(bp * tp, d)
    xs = x_sample.reshape(bs * ts, d)
    outs_p = [[] for _ in range(5)]
    outs_s = [[] for _ in range(5)]
    for l in range(depth):
        vecs = (norm_mix_pre[l][None], norm_mix_post[l][None], norm_ffn_pre[l][None], norm_ffn_post[l][None],
                attn_rel_bias[l], ret_norm_w[l], log_lb[l][None], log1m_lb[l][None], hgrn_norm_w[l],
                ffn_conv_w[l], ffn_conv_b[l][None])
        res_p = _layer(xp, bp, tp, 0, l, caches_p, 0, vecs, mats, PROMPT_TILES)
        res_s = _layer(xs, bs, ts, PAST_LEN, l, caches_s, l, vecs, mats, _sample_tiles(bs * ts, ts))
        xp, xs = res_p[0], res_s[0]
        for lst, val in zip(outs_p, res_p[1:]):
            lst.append(val)
        for lst, val in zip(outs_s, res_s[1:]):
            lst.append(val)

    return (xp.reshape(bp, tp, d), xs.reshape(bs, ts, d),
            *(jnp.stack(o) for o in outs_p), *(jnp.stack(o) for o in outs_s))
```

```python
import functools
import math

import numpy as np
import jax
import jax.numpy as jnp
from jax import lax
from jax.experimental import pallas as pl
from jax.experimental.pallas import tpu as pltpu

F32 = jnp.float32
BF16 = jnp.bfloat16

D_MODEL = 2048
CHUNK = 64
BAND_CHUNKS = 8
ATT_PAST_ROWS = BAND_CHUNKS * CHUNK
W_MIX = D_MODEL // 2
N_HEADS = 8
HEAD_DIM = W_MIX // N_HEADS
REL_CLIP = 128
N_BRANCH = 3
N_IN = 11 * W_MIX
D_FF = 5632
CONV_W = 3
ROPE_BASE = 10000.0
EPS = 1e-6
PAST_LEN = 1024

COL_QA, COL_KA, COL_VA, COL_QR, COL_KR, COL_VR, COL_GR, COL_QG, COL_FG, COL_IG, COL_GG = range(11)

V7X_VMEM_BYTES = 64 * 1024 * 1024
VMEM_LIMIT = V7X_VMEM_BYTES - 4 * 1024 * 1024
SUBLANES = 8
LANES = 128
BF16_ROWS = 16
ROW_SUB = 512
PROLOGUE_ROWS = 256

NT_DIMS = (((1,), (1,)), ((), ()))
TN_DIMS = (((0,), (0,)), ((), ()))


def _params(*sem):
    return pltpu.CompilerParams(dimension_semantics=sem, vmem_limit_bytes=VMEM_LIMIT)


def _rms(x, g):
    return x * lax.rsqrt(jnp.mean(x * x, axis=-1, keepdims=True) + EPS) * g


def _dot(a, b):
    return jnp.dot(a, b, preferred_element_type=F32)


def _dot_nt(a, b):
    return lax.dot_general(a, b, NT_DIMS, preferred_element_type=F32)


def _dot_tn(a, b):
    return lax.dot_general(a, b, TN_DIMS, preferred_element_type=F32)


def _silu(x):
    return x * jax.nn.sigmoid(x)


def _row_blocks(n):
    step = min(n, PROLOGUE_ROWS)
    return [slice(r, r + step) for r in range(0, n, step)]


def _norm_proj_kernel(x_ref, g_ref, w_ref, o_ref, *rest, f32_cols):
    f32_refs, h_ref = rest[:-1], rest[-1]
    j = pl.program_id(1)

    @pl.when(j == 0)
    def _():
        for rows in _row_blocks(x_ref.shape[0]):
            h_ref[rows, :] = _rms(x_ref[rows, :], g_ref[...]).astype(BF16)

    acc = _dot(h_ref[...], w_ref[...].astype(BF16))
    o_ref[...] = acc.astype(o_ref.dtype)
    for col, ref in zip(f32_cols, f32_refs):
        @pl.when(j == col)
        def _():
            ref[...] = acc


def norm_proj(x, g, w, layer, tm, tn, f32_cols=(), row_block0=0, n_row_blocks=None, col_block0=0,
              n_col_blocks=None, out_dtype=BF16):
    d = x.shape[1]
    n_row_blocks = x.shape[0] // tm - row_block0 if n_row_blocks is None else n_row_blocks
    n_col_blocks = w.shape[2] // tn - col_block0 if n_col_blocks is None else n_col_blocks
    rows, n = n_row_blocks * tm, n_col_blocks * tn
    kernel = functools.partial(_norm_proj_kernel, f32_cols=tuple(f32_cols))
    outs = pl.pallas_call(
        kernel,
        grid=(n_row_blocks, n_col_blocks),
        in_specs=[
            pl.BlockSpec((tm, d), lambda i, j: (row_block0 + i, 0)),
            pl.BlockSpec((1, d), lambda i, j: (0, 0)),
            pl.BlockSpec((None, d, tn), lambda i, j: (layer, 0, col_block0 + j)),
        ],
        out_specs=[pl.BlockSpec((tm, tn), lambda i, j: (i, j))]
        + [pl.BlockSpec((tm, tn), lambda i, j: (i, 0)) for _ in f32_cols],
        out_shape=[jax.ShapeDtypeStruct((rows, n), out_dtype)]
        + [jax.ShapeDtypeStruct((rows, tn), F32) for _ in f32_cols],
        scratch_shapes=[pltpu.VMEM((tm, d), BF16)],
        compiler_params=_params("parallel", "arbitrary"),
        name="norm_proj",
    )(x, g, w)
    return outs


def _attention_kernel(q_ref, kp_ref, ko_ref, vp_ref, vo_ref, *rest, past, group, sub, hps):
    n_sub = group // sub
    bias_refs, (o_ref, kw_ref, vw_ref) = rest[:n_sub], rest[n_sub:]
    kw_ref[0:past, :] = kp_ref[...].astype(BF16)
    kw_ref[past:past + group, :] = ko_ref[...].astype(BF16)
    for hh in range(hps):
        hs = slice(hh * HEAD_DIM, (hh + 1) * HEAD_DIM)
        vs = slice(2 * hh * HEAD_DIM, (2 * hh + 1) * HEAD_DIM)
        vw_ref[0:past, vs] = vp_ref[:, hs].astype(BF16)
        vw_ref[past:past + group, vs] = vo_ref[:, hs].astype(BF16)
        vw_ref[:, (2 * hh + 1) * HEAD_DIM:(2 * hh + 2) * HEAD_DIM] = jnp.ones((past + group, HEAD_DIM), BF16)
    scale = HEAD_DIM ** -0.5
    span = past + sub
    steps = [(hh, s) for hh in range(hps) for s in range(n_sub)]
    scores = []
    for hh, s in steps:
        hs = slice(hh * HEAD_DIM, (hh + 1) * HEAD_DIM)
        q = (q_ref[s * sub:(s + 1) * sub, hs].astype(F32) * scale).astype(BF16)
        scores.append(_dot_nt(q, kw_ref[s * sub:s * sub + span, hs]) + bias_refs[s][hh])
    probs = [jnp.exp(sc - jnp.max(sc, axis=-1, keepdims=True)).astype(BF16) for sc in scores]
    for (hh, s), p in zip(steps, probs):
        ov = _dot(p, vw_ref[s * sub:s * sub + span, 2 * hh * HEAD_DIM:(2 * hh + 2) * HEAD_DIM])
        o_ref[s * sub:(s + 1) * sub, hh * HEAD_DIM:(hh + 1) * HEAD_DIM] = (
            ov[:, 0:HEAD_DIM] / ov[:, HEAD_DIM:2 * HEAD_DIM]).astype(o_ref.dtype)


def _toeplitz_rows(z, n_rows, width):
    m = z.shape[-1]
    flat = jnp.tile(z, (1, n_rows))[:, :n_rows * (m - 1)]
    return flat.reshape(z.shape[0], n_rows, m - 1)[:, :, :width]


def _n_first_group_variants(sub, past, pos0, group):
    return min(group // sub, max(0, -(-(past - pos0) // sub)))


def _attention_tables(rel_bias, sub, chunk, past, pos0, group):
    width = past + sub
    m = np.arange(width + sub - 1)
    idx = np.clip(past + (sub - 1) - m, -REL_CLIP, REL_CLIP) + REL_CLIP
    n_hi = int(np.sum(idx == idx[0]))
    n_lo = int(np.sum(idx == idx[-1]))
    mid = idx[n_hi:idx.size - n_lo]
    assert mid.size and np.all(np.diff(mid) == -1)
    v = jnp.concatenate([jnp.broadcast_to(rel_bias[:, idx[0]:idx[0] + 1], (N_HEADS, n_hi)),
                         jnp.flip(rel_bias[:, int(mid[-1]):int(mid[0]) + 1], axis=1),
                         jnp.broadcast_to(rel_bias[:, idx[-1]:idx[-1] + 1], (N_HEADS, n_lo))], axis=1)
    z = jnp.concatenate([v[:, sub - 1:], jnp.zeros((N_HEADS, 1), F32), v[:, :sub - 1]], axis=1)
    toe = _toeplitz_rows(z, sub, width)
    i = np.arange(sub)[:, None]
    j = np.arange(width)[None, :]
    band = (j >= (i // chunk) * chunk) & (j < (i // chunk) * chunk + past + chunk)
    masks = [band]
    for s in range(_n_first_group_variants(sub, past, pos0, group)):
        masks.append(band & (j + (pos0 - past + s * sub) >= 0))
    neg = np.where(np.stack(masks), 0.0, -np.inf).astype(np.float32)
    return toe[None] + jnp.asarray(neg)[:, None]


def attention(q_arr, q_map, kp_arr, kp_lead, kp_map, ko_arr, ko_map, vp_arr, vp_map, vo_arr, vo_map, tables,
              n_groups, group, sub, past, hps):
    rows = n_groups * group
    n_sub = group // sub
    n_var = tables.shape[0]
    kernel = functools.partial(_attention_kernel, past=past, group=group, sub=sub, hps=hps)
    width = past + sub
    lanes = hps * HEAD_DIM

    def table_spec(s):
        if s >= n_var - 1:
            return pl.BlockSpec((None, hps, sub, width), lambda h, g: (0, h, 0, 0))
        return pl.BlockSpec((None, hps, sub, width), lambda h, g: (jnp.where(g == 0, 1 + s, 0), h, 0, 0))

    return pl.pallas_call(
        kernel,
        grid=(N_HEADS // hps, n_groups),
        in_specs=[
            pl.BlockSpec((group, lanes), q_map),
            pl.BlockSpec(kp_lead + (past, lanes), kp_map),
            pl.BlockSpec((group, lanes), ko_map),
            pl.BlockSpec(kp_lead + (past, lanes), vp_map),
            pl.BlockSpec((group, lanes), vo_map),
        ] + [table_spec(s) for s in range(n_sub)],
        out_specs=pl.BlockSpec((group, lanes), lambda h, g: (g, h)),
        out_shape=jax.ShapeDtypeStruct((rows, W_MIX), BF16),
        scratch_shapes=[pltpu.VMEM((past + group, lanes), BF16),
                        pltpu.VMEM((past + group, 2 * lanes), BF16)],
        compiler_params=_params("parallel", "arbitrary"),
        name="attention",
    )(q_arr, kp_arr, ko_arr, vp_arr, vo_arr, *([tables] * n_sub))


def _retention_kernel(q_ref, k_ref, v_ref, g_ref, cos_ref, sin_ref, di_ref, dq_ref, dk_ref, ds_ref,
                      nw_ref, s0_ref, y_ref, sout_ref, s_ref):
    t = pl.program_id(1)

    @pl.when(t == 0)
    def _():
        s_ref[...] = s0_ref[...]

    cosf = cos_ref[...]
    sinf = sin_ref[...]
    nw = nw_ref[...]
    qscale = HEAD_DIM ** -0.5

    heads = [slice(h * HEAD_DIM, (h + 1) * HEAD_DIM) for h in range(N_HEADS)]
    qs, ks, qds, kds = [], [], [], []
    for h, hs in enumerate(heads):
        q = q_ref[:, hs].astype(F32)
        k = k_ref[:, hs].astype(F32)
        q = (q * cosf + pltpu.roll(q, HEAD_DIM // 2, 1) * sinf) * qscale
        k = k * cosf + pltpu.roll(k, HEAD_DIM // 2, 1) * sinf
        qs.append(q.astype(BF16))
        ks.append(k.astype(BF16))
        qds.append((q * dq_ref[h]).astype(BF16))
        kds.append((k * dk_ref[h]).astype(BF16))
    amats = [(_dot_nt(qs[h], ks[h]) * di_ref[h]).astype(BF16) for h in range(N_HEADS)]
    outs = [_dot(amats[h], v_ref[:, hs]) + _dot(qds[h], s_ref[h].astype(BF16)) for h, hs in enumerate(heads)]
    for h, hs in enumerate(heads):
        s_ref[h] = s_ref[h] * ds_ref[h] + _dot_tn(kds[h], v_ref[:, hs])
    for h, hs in enumerate(heads):
        o = outs[h]
        mu = jnp.mean(o, axis=-1, keepdims=True)
        oc = o - mu
        var = jnp.mean(oc * oc, axis=-1, keepdims=True)
        on = oc * lax.rsqrt(var + EPS) * nw
        y_ref[:, hs] = (_silu(g_ref[:, hs].astype(F32)) * on).astype(y_ref.dtype)

    @pl.when(t == pl.num_programs(1) - 1)
    def _():
        sout_ref[...] = s_ref[...]


def _retention_tables(chunk, pos):
    log_gamma = jnp.log1p(-(2.0 ** (-5.0 - jnp.arange(N_HEADS, dtype=F32))))
    idx = jnp.arange(chunk, dtype=F32)
    diff = idx[:, None] - idx[None, :]
    d_intra = jnp.where(diff[None] >= 0, jnp.exp(jnp.maximum(diff, 0.0)[None] * log_gamma[:, None, None]), 0.0)
    d_q = jnp.exp((idx + 1.0)[None, :] * log_gamma[:, None])
    d_k = jnp.exp((chunk - 1.0 - idx)[None, :] * log_gamma[:, None])
    d_s = jnp.exp(chunk * log_gamma)
    d_q = jnp.broadcast_to(d_q[:, :, None], (N_HEADS, chunk, HEAD_DIM))
    d_k = jnp.broadcast_to(d_k[:, :, None], (N_HEADS, chunk, HEAD_DIM))
    d_s = jnp.broadcast_to(d_s[:, None, None], (N_HEADS, HEAD_DIM, HEAD_DIM))
    inv = ROPE_BASE ** (-jnp.arange(0, HEAD_DIM, 2, dtype=F32) / HEAD_DIM)
    ang = pos.astype(F32)[:, None] * inv[None, :]
    cos, sin = jnp.cos(ang), jnp.sin(ang)
    cosf = jnp.concatenate([cos, cos], axis=-1)
    sinf = jnp.concatenate([-sin, sin], axis=-1)
    return d_intra, d_q, d_k, d_s, cosf, sinf


def _state_spec(layer):
    return pl.BlockSpec((None, None, N_HEADS, HEAD_DIM, HEAD_DIM), lambda b, t: (layer, b, 0, 0, 0))


def retention(proj, norm_w, s0, s0_layer, batch, seq, chunk, pos0):
    n_chunks = seq // chunk
    d_intra, d_q, d_k, d_s, cosf, sinf = _retention_tables(chunk, pos0 + jnp.arange(seq))

    def col(c):
        return pl.BlockSpec((chunk, W_MIX), lambda b, t: (b * n_chunks + t, c))

    def const(shape):
        return pl.BlockSpec(shape, lambda b, t: (0,) * len(shape))

    y, s_out = pl.pallas_call(
        _retention_kernel,
        grid=(batch, n_chunks),
        in_specs=[
            col(COL_QR), col(COL_KR), col(COL_VR), col(COL_GR),
            pl.BlockSpec((chunk, HEAD_DIM), lambda b, t: (t, 0)),
            pl.BlockSpec((chunk, HEAD_DIM), lambda b, t: (t, 0)),
            const((N_HEADS, chunk, chunk)),
            const((N_HEADS, chunk, HEAD_DIM)),
            const((N_HEADS, chunk, HEAD_DIM)),
            const((N_HEADS, HEAD_DIM, HEAD_DIM)),
            const((1, HEAD_DIM)),
            _state_spec(s0_layer),
        ],
        out_specs=[
            pl.BlockSpec((chunk, W_MIX), lambda b, t: (b * n_chunks + t, 0)),
            pl.BlockSpec((None, N_HEADS, HEAD_DIM, HEAD_DIM), lambda b, t: (b, 0, 0, 0)),
        ],
        out_shape=[
            jax.ShapeDtypeStruct((batch * seq, W_MIX), BF16),
            jax.ShapeDtypeStruct((batch, N_HEADS, HEAD_DIM, HEAD_DIM), F32),
        ],
        scratch_shapes=[pltpu.VMEM((N_HEADS, HEAD_DIM, HEAD_DIM), F32)],
        compiler_params=_params("parallel", "arbitrary"),
        name="retention",
    )(proj, proj, proj, proj, cosf, sinf, d_intra, d_q, d_k, d_s, norm_w.reshape(1, HEAD_DIM), s0)
    return y, s_out


def _hgrn_level_masks(chunk):
    levels = int(math.log2(chunk))
    n = np.arange(chunk)
    masks = []
    for lv in range(levels):
        s = chunk >> (lv + 1)
        pair = n // (2 * s)
        qside = (n // s) % 2 == 1
        masks.append((qside[:, None] & ~qside[None, :] & (pair[:, None] == pair[None, :])).astype(np.float32))
    masks.append(np.eye(chunk, dtype=np.float32))
    return np.tril(np.ones((chunk, chunk), np.float32)), np.stack(masks, axis=0), levels


def _hgrn_kernel(q_ref, f_ref, i_ref, g_ref, loglb_ref, log1mlb_ref, tril_ref, mask_ref, nw_ref, s0_ref,
                 y_ref, sout_ref, st_ref, e_ref, x_ref, *, chunk, levels, cps):
    t = pl.program_id(1)

    @pl.when(t == 0)
    def _():
        for h in range(N_HEADS):
            st_ref[h] = s0_ref[h].T

    fg = f_ref[...]
    log_sig = jnp.minimum(fg, 0.0) - jnp.log(1.0 + jnp.exp(-jnp.abs(fg)))
    a = loglb_ref[...]
    b = log1mlb_ref[...] + log_sig
    logf = jnp.maximum(a, b) + jnp.log(1.0 + jnp.exp(-jnp.abs(a - b)))
    f = jnp.exp(logf)
    kk = 1.0 - f
    hi = logf.astype(BF16)
    r1 = logf - hi.astype(F32)
    mid = r1.astype(BF16)
    lo = (r1 - mid.astype(F32)).astype(BF16)
    tril = tril_ref[...]
    chunks = [slice(c * chunk, (c + 1) * chunk) for c in range(cps)]
    for rc in chunks:
        e_ref[rc, :] = _dot(tril, hi[rc]) + _dot(tril, mid[rc]) + _dot(tril, lo[rc])
    bcum = e_ref[...]

    def bcast(r, n_rows):
        return jnp.broadcast_to(e_ref[r:r + 1, :], (n_rows, W_MIX))

    def boundary_rows(s):
        if 2 * s >= SUBLANES:
            return jnp.concatenate([bcast(p * 2 * s + s - 1, 2 * s) for p in range(cps * chunk // (2 * s))], axis=0)
        sub = lax.broadcasted_iota(jnp.int32, (SUBLANES, W_MIX), 0)
        groups = []
        for g in range(cps * chunk // SUBLANES):
            acc = bcast(g * SUBLANES + s - 1, SUBLANES)
            for p in range(1, SUBLANES // (2 * s)):
                acc = jnp.where(sub >= p * 2 * s, bcast(g * SUBLANES + p * 2 * s + s - 1, SUBLANES), acc)
            groups.append(acc)
        return jnp.concatenate(groups, axis=0)

    q = q_ref[...].astype(F32)
    row = lax.broadcasted_iota(jnp.int32, (cps * chunk, W_MIX), 0)
    x_ref[levels] = q.astype(BF16)
    x_ref[levels + 1] = kk.astype(BF16)
    for lv in range(levels):
        s = chunk >> (lv + 1)
        qside = ((row // s) % 2) == 1
        if s == 1:
            x_ref[lv] = jnp.where(qside, q * f, kk).astype(BF16)
        else:
            d = bcum - boundary_rows(s)
            x_ref[lv] = (jnp.where(qside, q, kk) * jnp.exp(jnp.where(qside, d, -d))).astype(BF16)
    b_last = jnp.concatenate([bcast(rc.stop - 1, chunk) for rc in chunks], axis=0)
    x_ref[levels + 2] = (q * jnp.exp(bcum)).astype(BF16)
    x_ref[levels + 3] = (kk * jnp.exp(b_last - bcum)).astype(BF16)
    decays = [jnp.exp(e_ref[rc.stop - 1:rc.stop, :]) for rc in chunks]
    nw = nw_ref[...]

    heads = [slice(h * HEAD_DIM, (h + 1) * HEAD_DIM) for h in range(N_HEADS)]
    amats = {}
    for c, rc in enumerate(chunks):
        for h, hs in enumerate(heads):
            amat = mask_ref[levels] * _dot_nt(x_ref[levels, rc, hs], x_ref[levels + 1, rc, hs])
            for lv in range(levels):
                x = x_ref[lv, rc, hs]
                amat = amat + mask_ref[lv] * _dot_nt(x, x)
            amats[c, h] = amat.astype(BF16)
    states = [st_ref[h] for h in range(N_HEADS)]
    outs = {}
    for c, rc in enumerate(chunks):
        for h, hs in enumerate(heads):
            outs[c, h] = (_dot(amats[c, h], i_ref[rc, hs])
                          + _dot_nt(x_ref[levels + 2, rc, hs], states[h].astype(BF16)))
        for h, hs in enumerate(heads):
            states[h] = states[h] * decays[c][:, hs] + _dot_tn(i_ref[rc, hs], x_ref[levels + 3, rc, hs])
    for h in range(N_HEADS):
        st_ref[h] = states[h]
    for c, rc in enumerate(chunks):
        for h, hs in enumerate(heads):
            y_ref[rc, hs] = (_silu(g_ref[rc, hs].astype(F32)) * _rms(outs[c, h], nw)).astype(y_ref.dtype)

    @pl.when(t == pl.num_programs(1) - 1)
    def _():
        for h in range(N_HEADS):
            sout_ref[h] = st_ref[h].T


def hgrn2(proj, fg, log_lb, log1m_lb, norm_w, s0, s0_layer, batch, seq, chunk, cps):
    n_steps = seq // (chunk * cps)
    rows = chunk * cps
    tril_np, mask_np, levels = _hgrn_level_masks(chunk)
    tril = jnp.asarray(tril_np, dtype=BF16)
    masks = jnp.asarray(mask_np)

    def col(c):
        return pl.BlockSpec((rows, W_MIX), lambda b, t: (b * n_steps + t, c))

    def const(shape):
        return pl.BlockSpec(shape, lambda b, t: (0,) * len(shape))

    kernel = functools.partial(_hgrn_kernel, chunk=chunk, levels=levels, cps=cps)
    y, s_out = pl.pallas_call(
        kernel,
        grid=(batch, n_steps),
        in_specs=[
            col(COL_QG), col(0), col(COL_IG), col(COL_GG),
            const((1, W_MIX)), const((1, W_MIX)),
            const((chunk, chunk)),
            const((levels + 1, chunk, chunk)),
            const((1, HEAD_DIM)),
            _state_spec(s0_layer),
        ],
        out_specs=[
            pl.BlockSpec((rows, W_MIX), lambda b, t: (b * n_steps + t, 0)),
            pl.BlockSpec((None, N_HEADS, HEAD_DIM, HEAD_DIM), lambda b, t: (b, 0, 0, 0)),
        ],
        out_shape=[
            jax.ShapeDtypeStruct((batch * seq, W_MIX), BF16),
            jax.ShapeDtypeStruct((batch, N_HEADS, HEAD_DIM, HEAD_DIM), F32),
        ],
        scratch_shapes=[
            pltpu.VMEM((N_HEADS, HEAD_DIM, HEAD_DIM), F32),
            pltpu.VMEM((rows, W_MIX), F32),
            pltpu.VMEM((levels + 4, rows, W_MIX), BF16),
        ],
        compiler_params=_params("parallel", "arbitrary"),
        name="hgrn2",
    )(proj, fg, proj, proj, log_lb, log1m_lb, tril, masks, norm_w.reshape(1, HEAD_DIM), s0)
    return y, s_out


def _merge_kernel(x_ref, gpre_ref, ya_ref, yr_ref, yg_ref, wg0_ref, wg1_ref, wg2_ref, wb_ref, wo_ref,
                  gpost_ref, o_ref, h_ref, *, tm, rs):
    j = pl.program_id(1)

    @pl.when(j == 0)
    def _():
        for rows in _row_blocks(tm):
            h_ref[rows, :] = _rms(x_ref[rows, :], gpre_ref[...]).astype(BF16)
        o_ref[...] = jnp.zeros_like(o_ref)

    for r in range(tm // rs):
        rows = slice(r * rs, (r + 1) * rs)
        h = h_ref[rows, :]
        merged = None
        for n, (y_ref, wg_ref) in enumerate(((ya_ref, wg0_ref), (yr_ref, wg1_ref), (yg_ref, wg2_ref))):
            gate = jax.nn.sigmoid(_dot(h, wg_ref[...]))
            term = gate * _dot(y_ref[rows, :], wb_ref[n])
            merged = term if merged is None else merged + term
        o_ref[rows, :] += _dot(merged.astype(BF16), wo_ref[...])

    @pl.when(j == pl.num_programs(1) - 1)
    def _():
        for rows in _row_blocks(tm):
            o_ref[rows, :] = x_ref[rows, :] + _rms(o_ref[rows, :], gpost_ref[...])


def merge(x, g_pre, ya, yr, yg, w_gate, w_branch, w_out, g_post, layer, tm, tn):
    rows, d = x.shape
    nj = d // tn

    def gate_spec(n):
        return pl.BlockSpec((None, d, tn), lambda i, j: (layer, 0, n * nj + j))

    kernel = functools.partial(_merge_kernel, tm=tm, rs=min(tm, ROW_SUB))
    single = pl.Buffered(1)
    return pl.pallas_call(
        kernel,
        grid=(rows // tm, nj),
        in_specs=[
            pl.BlockSpec((tm, d), lambda i, j: (i, 0)),
            pl.BlockSpec((1, d), lambda i, j: (0, 0)),
            pl.BlockSpec((tm, W_MIX), lambda i, j: (i, 0), pipeline_mode=single),
            pl.BlockSpec((tm, W_MIX), lambda i, j: (i, 0), pipeline_mode=single),
            pl.BlockSpec((tm, W_MIX), lambda i, j: (i, 0), pipeline_mode=single),
            gate_spec(0), gate_spec(1), gate_spec(2),
            pl.BlockSpec((None, N_BRANCH, W_MIX, tn), lambda i, j: (layer, 0, 0, j)),
            pl.BlockSpec((None, tn, d), lambda i, j: (layer, j, 0)),
            pl.BlockSpec((1, d), lambda i, j: (0, 0)),
        ],
        out_specs=pl.BlockSpec((tm, d), lambda i, j: (i, 0)),
        out_shape=jax.ShapeDtypeStruct((rows, d), F32),
        scratch_shapes=[pltpu.VMEM((tm, d), BF16)],
        compiler_params=_params("parallel", "arbitrary"),
        name="merge",
    )(x, g_pre, ya, yr, yg, w_gate, w_gate, w_gate, w_branch, w_out, g_post)


HALO = BF16_ROWS


def _ffn_kernel(x_ref, xprev_ref, cst_ref, gpre_ref, wup_ref, wval_ref, cw_ref, cb_ref, wdn_ref, gpost_ref,
                o_ref, uplast_ref, h_ref, up_ref, *, tm, seq):
    i = pl.program_id(0)
    j = pl.program_id(1)
    seqs_per_tile = max(tm // seq, 1)
    tiles_per_seq = max(seq // tm, 1)

    @pl.when(j == 0)
    def _():
        g = gpre_ref[...]
        h_ref[0:HALO, :] = _rms(xprev_ref[...], g).astype(BF16)
        for rows in _row_blocks(tm):
            h_ref[HALO + rows.start:HALO + rows.stop, :] = _rms(x_ref[rows, :], g).astype(BF16)
        o_ref[...] = jnp.zeros_like(o_ref)

    cw = cw_ref[...]
    wdn = wdn_ref[...].astype(BF16)
    rs = min(tm, ROW_SUB)
    n_sub = tm // rs

    def up_block(r):
        lo = 0 if r == 0 else HALO + r * rs
        hi = HALO + (r + 1) * rs
        up_ref[lo:hi, :] = _dot(h_ref[lo:hi, :], wup_ref[...])

    up_block(0)
    if seqs_per_tile == 1:
        first = i % tiles_per_seq == 0
        halo_rows = slice(HALO - (CONV_W - 1), HALO)
        up_ref[halo_rows, :] = jnp.where(first, cst_ref[0], up_ref[halo_rows, :])
    for r in range(n_sub):
        if r + 1 < n_sub:
            up_block(r + 1)
        r0 = HALO + r * rs
        tap2 = up_ref[r0 - 2:r0 - 2 + rs, :]
        tap1 = up_ref[r0 - 1:r0 - 1 + rs, :]
        tap0 = up_ref[r0:r0 + rs, :]
        if seqs_per_tile > 1:
            tf = tap0.shape[1]
            st = cst_ref[...]
            st0 = jnp.broadcast_to(st[:, 0:1, :], (seqs_per_tile, seq, tf)).reshape(tm, tf)
            st1 = jnp.broadcast_to(st[:, 1:2, :], (seqs_per_tile, seq, tf)).reshape(tm, tf)
            pos = lax.broadcasted_iota(jnp.int32, (tm, tf), 0) % seq
            tap1 = jnp.where(pos == 0, st1, tap1)
            tap2 = jnp.where(pos == 0, st0, jnp.where(pos == 1, st1, tap2))
        conv = cb_ref[...] + tap2 * cw[0:1, :] + tap1 * cw[1:2, :] + tap0 * cw[2:3, :]
        val = _dot(h_ref[r0:r0 + rs, :], wval_ref[...])
        act = 0.5 * conv * (1.0 + lax.erf(conv * math.sqrt(0.5))) * val
        o_ref[r * rs:(r + 1) * rs, :] += _dot(act.astype(BF16), wdn)

    keep = uplast_ref.shape[0]
    uplast_ref[...] = up_ref[HALO + tm - keep:HALO + tm, :]

    @pl.when(j == pl.num_programs(1) - 1)
    def _():
        for rows in _row_blocks(tm):
            o_ref[rows, :] = x_ref[rows, :] + _rms(o_ref[rows, :], gpost_ref[...])


def ffn(x, conv_state, cs_layer, g_pre, w_up, w_val, conv_w, conv_b, w_down, g_post, layer, batch, seq, tm, tf):
    rows, d = x.shape
    dff = w_up.shape[2]
    n_tiles = rows // tm
    seqs_per_tile = max(tm // seq, 1)
    tiles_per_seq = max(seq // tm, 1)
    assert rows == batch * seq and (tm % seq == 0 or seq % tm == 0)
    keep = tm if seqs_per_tile > 1 else SUBLANES
    halo_blocks = tm // HALO
    kernel = functools.partial(_ffn_kernel, tm=tm, seq=seq)
    out, uplast = pl.pallas_call(
        kernel,
        grid=(n_tiles, dff // tf),
        in_specs=[
            pl.BlockSpec((tm, d), lambda i, j: (i, 0)),
            pl.BlockSpec((HALO, d), lambda i, j: (jnp.maximum(i * halo_blocks - 1, 0), 0)),
            pl.BlockSpec((None, seqs_per_tile, CONV_W - 1, tf),
                         lambda i, j: (cs_layer, i * seqs_per_tile // tiles_per_seq, 0, j)),
            pl.BlockSpec((1, d), lambda i, j: (0, 0)),
            pl.BlockSpec((None, d, tf), lambda i, j: (layer, 0, j)),
            pl.BlockSpec((None, d, tf), lambda i, j: (layer, 0, j)),
            pl.BlockSpec((CONV_W, tf), lambda i, j: (0, j)),
            pl.BlockSpec((1, tf), lambda i, j: (0, j)),
            pl.BlockSpec((None, tf, d), lambda i, j: (layer, j, 0)),
            pl.BlockSpec((1, d), lambda i, j: (0, 0)),
        ],
        out_specs=[
            pl.BlockSpec((tm, d), lambda i, j: (i, 0)),
            pl.BlockSpec((keep, tf), lambda i, j: (i, j)),
        ],
        out_shape=[
            jax.ShapeDtypeStruct((rows, d), F32),
            jax.ShapeDtypeStruct((n_tiles * keep, dff), F32),
        ],
        scratch_shapes=[
            pltpu.VMEM((HALO + tm, d), BF16),
            pltpu.VMEM((HALO + tm, tf), F32),
        ],
        compiler_params=_params("parallel", "arbitrary"),
        name="ffn",
    )(x, x, conv_state, g_pre, w_up, w_val, conv_w, conv_b, w_down, g_post)
    new_conv = uplast.reshape(batch, -1, dff)[:, -(CONV_W - 1):, :]
    return out, new_conv


def _layer(x, batch, seq, pos0, layer, caches, cache_layer, vecs, mats, tiles):
    k_past, v_past, ret_s0, hgrn_s0, conv_s0 = caches
    (g_mix_pre, g_mix_post, g_ffn_pre, g_ffn_post, rel_bias, ret_norm_w, log_lb, log1m_lb, hgrn_norm_w,
     conv_w, conv_b) = vecs
    w_in, w_gate, w_branch, w_out, w_up, w_val, w_down = mats
    rows = batch * seq
    tn = W_MIX
    f32_cols = (COL_FG,) if k_past is None else (COL_FG, COL_KA, COL_VA)
    proj, fg, *kv_f32 = norm_proj(x, g_mix_pre, w_in, layer, tiles["tm_proj"], tn, f32_cols=f32_cols)

    chunk = min(CHUNK, seq)
    past = ATT_PAST_ROWS
    hps = tiles["att_heads"]
    hb = N_HEADS // hps
    if k_past is None:
        keep = min(past, seq)
        (kv_new,) = norm_proj(x, g_mix_pre, w_in, layer, keep, tn, row_block0=seq // keep - 1, n_row_blocks=1,
                              col_block0=COL_KA, n_col_blocks=2, out_dtype=F32)
        group, sub = tiles["att_group"], tiles["att_sub"]
        gp = group // past
        tables = _attention_tables(rel_bias, sub, chunk, past, pos0, group)
        y_a = attention(
            proj, lambda h, g: (g, h),
            proj, (), lambda h, g: (jnp.maximum(g * gp - 1, 0), hb + h),
            proj, lambda h, g: (g, hb + h),
            proj, lambda h, g: (jnp.maximum(g * gp - 1, 0), 2 * hb + h),
            proj, lambda h, g: (g, 2 * hb + h),
            tables, seq // group, group, sub, past, hps)
    else:
        keep = seq
        kv_new = jnp.concatenate(kv_f32, axis=1)
        group = sub = seq
        tables = _attention_tables(rel_bias, sub, chunk, past, pos0, group)
        n_l = k_past.shape[0]
        kp = k_past.reshape(n_l, batch * past, W_MIX)
        vp = v_past.reshape(n_l, batch * past, W_MIX)
        y_a = attention(
            proj, lambda h, g: (g, h),
            kp, (None,), lambda h, g: (cache_layer, g, h),
            proj, lambda h, g: (g, hb + h),
            vp, lambda h, g: (cache_layer, g, h),
            proj, lambda h, g: (g, 2 * hb + h),
            tables, batch, group, sub, past, hps)

    y_r, ret_s = retention(proj, ret_norm_w, ret_s0, cache_layer, batch, seq, tiles["ret_chunk"], pos0)
    y_g, hgrn_s = hgrn2(proj, fg, log_lb, log1m_lb, hgrn_norm_w, hgrn_s0, cache_layer, batch, seq, chunk,
                          tiles["hgrn_cps"])

    x = merge(x, g_mix_pre, y_a, y_r, y_g, w_gate, w_branch, w_out, g_mix_post, layer,
              tiles["tm_merge"], tiles["tn_merge"])
    x, new_conv = ffn(x, conv_s0, cache_layer, g_ffn_pre, w_up, w_val, conv_w, conv_b, w_down, g_ffn_post, layer,
                      batch, seq, tiles["tm_ffn"], tiles["tf_ffn"])
    k_new = kv_new[:, :W_MIX].reshape(batch, keep, N_HEADS, HEAD_DIM)
    v_new = kv_new[:, W_MIX:].reshape(batch, keep, N_HEADS, HEAD_DIM)
    return x, k_new, v_new, ret_s, hgrn_s, new_conv


PROMPT_TILES = dict(tm_proj=1024, att_group=1024, att_sub=256, att_heads=4, ret_chunk=256, hgrn_cps=8, tm_merge=1024, tn_merge=256, tm_ffn=1024, tf_ffn=512)


def _sample_tiles(rows, seq):
    return dict(tm_proj=rows, att_heads=N_HEADS, ret_chunk=seq, hgrn_cps=1, tm_merge=rows, tn_merge=256, tm_ffn=rows, tf_ffn=512)


def kernel(x_prompt, x_sample, cache_attn_k, cache_attn_v, state_ret, state_hgrn, state_ffn_conv,
           norm_mix_pre, norm_mix_post, norm_ffn_pre, norm_ffn_post, w_in, attn_rel_bias, ret_norm_w,
           hgrn_lower_bound, hgrn_norm_w, w_gate, w_branch, w_out, ffn_w_up, ffn_w_val, ffn_conv_w,
           ffn_conv_b, ffn_w_down):
    depth = w_in.shape[0]
    bp, tp, d = x_prompt.shape
    bs, ts, _ = x_sample.shape

    lb_cum = jnp.cumsum(jax.nn.softmax(hgrn_lower_bound.astype(F32), axis=0), axis=0)
    lb_layers = lb_cum - lb_cum[0:1]
    log_lb = jnp.log(lb_layers)
    log1m_lb = jnp.log1p(-lb_layers)

    bf = lambda w: w.astype(BF16)
    mats = (w_in, bf(w_gate), bf(w_branch), bf(w_out), bf(ffn_w_up), bf(ffn_w_val), ffn_w_down)

    zs = jnp.zeros((1, bp, N_HEADS, HEAD_DIM, HEAD_DIM), F32)
    zc = jnp.zeros((1, bp, CONV_W - 1, D_FF), F32)
    caches_p = (None, None, zs, zs, zc)
    caches_s = (cache_attn_k, cache_attn_v, state_ret, state_hgrn, state_ffn_conv)

    xp = x_prompt.reshape(bp * tp, d)
    xs = x_sample.reshape(bs * ts, d)
    outs_p = [[] for _ in range(5)]
    outs_s = [[] for _ in range(5)]
    for l in range(depth):
        vecs = (norm_mix_pre[l][None], norm_mix_post[l][None], norm_ffn_pre[l][None], norm_ffn_post[l][None],
                attn_rel_bias[l], ret_norm_w[l], log_lb[l][None], log1m_lb[l][None], hgrn_norm_w[l],
                ffn_conv_w[l], ffn_conv_b[l][None])
        res_p = _layer(xp, bp, tp, 0, l, caches_p, 0, vecs, mats, PROMPT_TILES)
        res_s = _layer(xs, bs, ts, PAST_LEN, l, caches_s, l, vecs, mats, _sample_tiles(bs * ts, ts))
        xp, xs = res_p[0], res_s[0]
        for lst, val in zip(outs_p, res_p[1:]):
            lst.append(val)
        for lst, val in zip(outs_s, res_s[1:]):
            lst.append(val)

    return (xp.reshape(bp, tp, d), xs.reshape(bs, ts, d),
            *(jnp.stack(o) for o in outs_p), *(jnp.stack(o) for o in outs_s))
```

```python
import functools
import math

import numpy as np
import jax
import jax.numpy as jnp
from jax import lax
from jax.experimental import pallas as pl
from jax.experimental.pallas import tpu as pltpu

F32 = jnp.float32
BF16 = jnp.bfloat16

D_MODEL = 2048
CHUNK = 64
BAND_CHUNKS = 8
ATT_PAST_ROWS = BAND_CHUNKS * CHUNK
W_MIX = D_MODEL // 2
N_HEADS = 8
HEAD_DIM = W_MIX // N_HEADS
REL_CLIP = 128
N_BRANCH = 3
N_IN = 11 * W_MIX
D_FF = 5632
CONV_W = 3
ROPE_BASE = 10000.0
EPS = 1e-6
PAST_LEN = 1024

COL_QA, COL_KA, COL_VA, COL_QR, COL_KR, COL_VR, COL_GR, COL_QG, COL_FG, COL_IG, COL_GG = range(11)

V7X_VMEM_BYTES = 64 * 1024 * 1024
VMEM_LIMIT = V7X_VMEM_BYTES - 4 * 1024 * 1024
SUBLANES = 8
LANES = 128
BF16_ROWS = 16
ROW_SUB = 512
PROLOGUE_ROWS = 256

NT_DIMS = (((1,), (1,)), ((), ()))
TN_DIMS = (((0,), (0,)), ((), ()))


def _params(*sem):
    return pltpu.CompilerParams(dimension_semantics=sem, vmem_limit_bytes=VMEM_LIMIT)


def _rms(x, g):
    return x * lax.rsqrt(jnp.mean(x * x, axis=-1, keepdims=True) + EPS) * g


def _dot(a, b):
    return jnp.dot(a, b, preferred_element_type=F32)


def _dot_nt(a, b):
    return lax.dot_general(a, b, NT_DIMS, preferred_element_type=F32)


def _dot_tn(a, b):
    return lax.dot_general(a, b, TN_DIMS, preferred_element_type=F32)


def _silu(x):
    return x * jax.nn.sigmoid(x)


def _row_blocks(n):
    step = min(n, PROLOGUE_ROWS)
    return [slice(r, r + step) for r in range(0, n, step)]


def _norm_proj_kernel(x_ref, g_ref, w_ref, o_ref, *rest, f32_cols):
    f32_refs, h_ref = rest[:-1], rest[-1]
    j = pl.program_id(1)

    @pl.when(j == 0)
    def _():
        for rows in _row_blocks(x_ref.shape[0]):
            h_ref[rows, :] = _rms(x_ref[rows, :], g_ref[...]).astype(BF16)

    acc = _dot(h_ref[...], w_ref[...].astype(BF16))
    o_ref[...] = acc.astype(o_ref.dtype)
    for col, ref in zip(f32_cols, f32_refs):
        @pl.when(j == col)
        def _():
            ref[...] = acc


def norm_proj(x, g, w, layer, tm, tn, f32_cols=(), row_block0=0, n_row_blocks=None, col_block0=0,
              n_col_blocks=None, out_dtype=BF16):
    d = x.shape[1]
    n_row_blocks = x.shape[0] // tm - row_block0 if n_row_blocks is None else n_row_blocks
    n_col_blocks = w.shape[2] // tn - col_block0 if n_col_blocks is None else n_col_blocks
    rows, n = n_row_blocks * tm, n_col_blocks * tn
    kernel = functools.partial(_norm_proj_kernel, f32_cols=tuple(f32_cols))
    outs = pl.pallas_call(
        kernel,
        grid=(n_row_blocks, n_col_blocks),
        in_specs=[
            pl.BlockSpec((tm, d), lambda i, j: (row_block0 + i, 0)),
            pl.BlockSpec((1, d), lambda i, j: (0, 0)),
            pl.BlockSpec((None, d, tn), lambda i, j: (layer, 0, col_block0 + j)),
        ],
        out_specs=[pl.BlockSpec((tm, tn), lambda i, j: (i, j))]
        + [pl.BlockSpec((tm, tn), lambda i, j: (i, 0)) for _ in f32_cols],
        out_shape=[jax.ShapeDtypeStruct((rows, n), out_dtype)]
        + [jax.ShapeDtypeStruct((rows, tn), F32) for _ in f32_cols],
        scratch_shapes=[pltpu.VMEM((tm, d), BF16)],
        compiler_params=_params("parallel", "arbitrary"),
        name="norm_proj",
    )(x, g, w)
    return outs


def _attention_kernel(q_ref, kp_ref, ko_ref, vp_ref, vo_ref, *rest, past, group, sub, hps):
    n_sub = group // sub
    n_first = len(rest) - 5
    z_ref, band_ref, first_refs, (o_ref, kw_ref, vw_ref) = rest[0], rest[1], rest[2:2 + n_first], rest[2 + n_first:]
    span = past + sub
    kw_ref[0:past, :] = kp_ref[...].astype(BF16)
    kw_ref[past:past + group, :] = ko_ref[...].astype(BF16)
    for hh in range(hps):
        hs = slice(hh * HEAD_DIM, (hh + 1) * HEAD_DIM)
        vs = slice(2 * hh * HEAD_DIM, (2 * hh + 1) * HEAD_DIM)
        vw_ref[0:past, vs] = vp_ref[:, hs].astype(BF16)
        vw_ref[past:past + group, vs] = vo_ref[:, hs].astype(BF16)
        vw_ref[:, (2 * hh + 1) * HEAD_DIM:(2 * hh + 2) * HEAD_DIM] = jnp.ones((past + group, HEAD_DIM), BF16)
    scale = HEAD_DIM ** -0.5
    period = z_ref.shape[-1]
    band = band_ref[...]
    bias = [pltpu.roll(jnp.broadcast_to(z_ref[hh], (sub, period)), 0, 1, stride=1, stride_axis=0)[:, :span] + band
            for hh in range(hps)]
    steps = [(hh, s) for hh in range(hps) for s in range(n_sub)]
    scores = []
    for hh, s in steps:
        hs = slice(hh * HEAD_DIM, (hh + 1) * HEAD_DIM)
        q = (q_ref[s * sub:(s + 1) * sub, hs].astype(F32) * scale).astype(BF16)
        sc = _dot_nt(q, kw_ref[s * sub:s * sub + span, hs]) + bias[hh]
        scores.append(sc + first_refs[s][...] if s < n_first else sc)
    probs = [jnp.exp(sc - jnp.max(sc, axis=-1, keepdims=True)).astype(BF16) for sc in scores]
    for (hh, s), p in zip(steps, probs):
        ov = _dot(p, vw_ref[s * sub:s * sub + span, 2 * hh * HEAD_DIM:(2 * hh + 2) * HEAD_DIM])
        o_ref[s * sub:(s + 1) * sub, hh * HEAD_DIM:(hh + 1) * HEAD_DIM] = (
            ov[:, 0:HEAD_DIM] / ov[:, HEAD_DIM:2 * HEAD_DIM]).astype(o_ref.dtype)


def _n_first_group_variants(sub, past, pos0, group):
    return min(group // sub, max(0, -(-(past - pos0) // sub)))


def _attention_tables(rel_bias, sub, chunk, past, pos0, group):
    width = past + sub
    m = np.arange(width + sub - 1)
    idx = np.clip(past + (sub - 1) - m, -REL_CLIP, REL_CLIP) + REL_CLIP
    n_hi = int(np.sum(idx == idx[0]))
    n_lo = int(np.sum(idx == idx[-1]))
    mid = idx[n_hi:idx.size - n_lo]
    assert mid.size and np.all(np.diff(mid) == -1)
    v = jnp.concatenate([jnp.broadcast_to(rel_bias[:, idx[0]:idx[0] + 1], (N_HEADS, n_hi)),
                         jnp.flip(rel_bias[:, int(mid[-1]):int(mid[0]) + 1], axis=1),
                         jnp.broadcast_to(rel_bias[:, idx[-1]:idx[-1] + 1], (N_HEADS, n_lo))], axis=1)
    period = -(-(width + sub) // LANES) * LANES
    z = jnp.concatenate([v[:, sub - 1:], jnp.zeros((N_HEADS, period - v.shape[1]), F32), v[:, :sub - 1]], axis=1)
    i = np.arange(sub)[:, None]
    j = np.arange(width)[None, :]
    band = (j >= (i // chunk) * chunk) & (j < (i // chunk) * chunk + past + chunk)
    first = [np.ones_like(band)]
    for s in range(_n_first_group_variants(sub, past, pos0, group)):
        first.append(np.broadcast_to(j + (pos0 - past + s * sub) >= 0, band.shape))
    as_mask = lambda ok: jnp.asarray(np.where(ok, 0.0, -np.inf).astype(np.float32))
    return z[:, None, :], as_mask(band), as_mask(np.stack(first))


def attention(q_arr, q_map, kp_arr, kp_lead, kp_map, ko_arr, ko_map, vp_arr, vp_map, vo_arr, vo_map, tables,
              n_groups, group, sub, past, hps):
    z, band, first = tables
    rows = n_groups * group
    n_sub = group // sub
    n_first = first.shape[0] - 1
    kernel = functools.partial(_attention_kernel, past=past, group=group, sub=sub, hps=hps)
    width = past + sub
    lanes = hps * HEAD_DIM

    def first_spec(s):
        return pl.BlockSpec((None, sub, width), lambda h, g: (jnp.where(g == 0, 1 + s, 0), 0, 0))

    return pl.pallas_call(
        kernel,
        grid=(N_HEADS // hps, n_groups),
        in_specs=[
            pl.BlockSpec((group, lanes), q_map),
            pl.BlockSpec(kp_lead + (past, lanes), kp_map),
            pl.BlockSpec((group, lanes), ko_map),
            pl.BlockSpec(kp_lead + (past, lanes), vp_map),
            pl.BlockSpec((group, lanes), vo_map),
            pl.BlockSpec((hps, 1, z.shape[-1]), lambda h, g: (h, 0, 0)),
            pl.BlockSpec((sub, width), lambda h, g: (0, 0)),
        ] + [first_spec(s) for s in range(n_first)],
        out_specs=pl.BlockSpec((group, lanes), lambda h, g: (g, h)),
        out_shape=jax.ShapeDtypeStruct((rows, W_MIX), BF16),
        scratch_shapes=[pltpu.VMEM((past + group, lanes), BF16),
                        pltpu.VMEM((past + group, 2 * lanes), BF16)],
        compiler_params=_params("parallel", "arbitrary"),
        name="attention",
    )(q_arr, kp_arr, ko_arr, vp_arr, vo_arr, z, band, *([first] * n_first))


def _retention_kernel(q_ref, k_ref, v_ref, g_ref, cos_ref, sin_ref, di_ref, dq_ref, dk_ref, ds_ref,
                      nw_ref, s0_ref, y_ref, sout_ref, s_ref):
    t = pl.program_id(1)

    @pl.when(t == 0)
    def _():
        s_ref[...] = s0_ref[...]

    cosf = cos_ref[...]
    sinf = sin_ref[...]
    nw = nw_ref[...]
    qscale = HEAD_DIM ** -0.5

    heads = [slice(h * HEAD_DIM, (h + 1) * HEAD_DIM) for h in range(N_HEADS)]
    qs, ks, qds, kds = [], [], [], []
    for h, hs in enumerate(heads):
        q = q_ref[:, hs].astype(F32)
        k = k_ref[:, hs].astype(F32)
        q = (q * cosf + pltpu.roll(q, HEAD_DIM // 2, 1) * sinf) * qscale
        k = k * cosf + pltpu.roll(k, HEAD_DIM // 2, 1) * sinf
        qs.append(q.astype(BF16))
        ks.append(k.astype(BF16))
        qds.append((q * dq_ref[h]).astype(BF16))
        kds.append((k * dk_ref[h]).astype(BF16))
    amats = [(_dot_nt(qs[h], ks[h]) * di_ref[h]).astype(BF16) for h in range(N_HEADS)]
    outs = [_dot(amats[h], v_ref[:, hs]) + _dot(qds[h], s_ref[h].astype(BF16)) for h, hs in enumerate(heads)]
    for h, hs in enumerate(heads):
        s_ref[h] = s_ref[h] * ds_ref[h] + _dot_tn(kds[h], v_ref[:, hs])
    for h, hs in enumerate(heads):
        o = outs[h]
        mu = jnp.mean(o, axis=-1, keepdims=True)
        oc = o - mu
        var = jnp.mean(oc * oc, axis=-1, keepdims=True)
        on = oc * lax.rsqrt(var + EPS) * nw
        y_ref[:, hs] = (_silu(g_ref[:, hs].astype(F32)) * on).astype(y_ref.dtype)

    @pl.when(t == pl.num_programs(1) - 1)
    def _():
        sout_ref[...] = s_ref[...]


def _retention_tables(chunk, pos):
    log_gamma = jnp.log1p(-(2.0 ** (-5.0 - jnp.arange(N_HEADS, dtype=F32))))
    idx = jnp.arange(chunk, dtype=F32)
    diff = idx[:, None] - idx[None, :]
    d_intra = jnp.where(diff[None] >= 0, jnp.exp(jnp.maximum(diff, 0.0)[None] * log_gamma[:, None, None]), 0.0)
    d_q = jnp.exp((idx + 1.0)[None, :] * log_gamma[:, None])
    d_k = jnp.exp((chunk - 1.0 - idx)[None, :] * log_gamma[:, None])
    d_s = jnp.exp(chunk * log_gamma)
    d_q = jnp.broadcast_to(d_q[:, :, None], (N_HEADS, chunk, HEAD_DIM))
    d_k = jnp.broadcast_to(d_k[:, :, None], (N_HEADS, chunk, HEAD_DIM))
    d_s = jnp.broadcast_to(d_s[:, None, None], (N_HEADS, HEAD_DIM, HEAD_DIM))
    inv = ROPE_BASE ** (-jnp.arange(0, HEAD_DIM, 2, dtype=F32) / HEAD_DIM)
    ang = pos.astype(F32)[:, None] * inv[None, :]
    cos, sin = jnp.cos(ang), jnp.sin(ang)
    cosf = jnp.concatenate([cos, cos], axis=-1)
    sinf = jnp.concatenate([-sin, sin], axis=-1)
    return d_intra, d_q, d_k, d_s, cosf, sinf


def _state_spec(layer):
    return pl.BlockSpec((None, None, N_HEADS, HEAD_DIM, HEAD_DIM), lambda b, t: (layer, b, 0, 0, 0))


def retention(proj, norm_w, s0, s0_layer, batch, seq, chunk, pos0):
    n_chunks = seq // chunk
    d_intra, d_q, d_k, d_s, cosf, sinf = _retention_tables(chunk, pos0 + jnp.arange(seq))

    def col(c):
        return pl.BlockSpec((chunk, W_MIX), lambda b, t: (b * n_chunks + t, c))

    def const(shape):
        return pl.BlockSpec(shape, lambda b, t: (0,) * len(shape))

    y, s_out = pl.pallas_call(
        _retention_kernel,
        grid=(batch, n_chunks),
        in_specs=[
            col(COL_QR), col(COL_KR), col(COL_VR), col(COL_GR),
            pl.BlockSpec((chunk, HEAD_DIM), lambda b, t: (t, 0)),
            pl.BlockSpec((chunk, HEAD_DIM), lambda b, t: (t, 0)),
            const((N_HEADS, chunk, chunk)),
            const((N_HEADS, chunk, HEAD_DIM)),
            const((N_HEADS, chunk, HEAD_DIM)),
            const((N_HEADS, HEAD_DIM, HEAD_DIM)),
            const((1, HEAD_DIM)),
            _state_spec(s0_layer),
        ],
        out_specs=[
            pl.BlockSpec((chunk, W_MIX), lambda b, t: (b * n_chunks + t, 0)),
            pl.BlockSpec((None, N_HEADS, HEAD_DIM, HEAD_DIM), lambda b, t: (b, 0, 0, 0)),
        ],
        out_shape=[
            jax.ShapeDtypeStruct((batch * seq, W_MIX), BF16),
            jax.ShapeDtypeStruct((batch, N_HEADS, HEAD_DIM, HEAD_DIM), F32),
        ],
        scratch_shapes=[pltpu.VMEM((N_HEADS, HEAD_DIM, HEAD_DIM), F32)],
        compiler_params=_params("parallel", "arbitrary"),
        name="retention",
    )(proj, proj, proj, proj, cosf, sinf, d_intra, d_q, d_k, d_s, norm_w.reshape(1, HEAD_DIM), s0)
    return y, s_out


def _hgrn_level_masks(chunk):
    levels = int(math.log2(chunk))
    n = np.arange(chunk)
    masks = []
    for lv in range(levels):
        s = chunk >> (lv + 1)
        pair = n // (2 * s)
        qside = (n // s) % 2 == 1
        masks.append((qside[:, None] & ~qside[None, :] & (pair[:, None] == pair[None, :])).astype(np.float32))
    masks.append(np.eye(chunk, dtype=np.float32))
    return np.tril(np.ones((chunk, chunk), np.float32)), np.stack(masks, axis=0), levels


def _hgrn_kernel(q_ref, f_ref, i_ref, g_ref, loglb_ref, log1mlb_ref, tril_ref, mask_ref, nw_ref, s0_ref,
                 y_ref, sout_ref, st_ref, e_ref, x_ref, *, chunk, levels, cps):
    t = pl.program_id(1)

    @pl.when(t == 0)
    def _():
        for h in range(N_HEADS):
            st_ref[h] = s0_ref[h].T

    fg = f_ref[...]
    log_sig = jnp.minimum(fg, 0.0) - jnp.log(1.0 + jnp.exp(-jnp.abs(fg)))
    a = loglb_ref[...]
    b = log1mlb_ref[...] + log_sig
    logf = jnp.maximum(a, b) + jnp.log(1.0 + jnp.exp(-jnp.abs(a - b)))
    f = jnp.exp(logf)
    kk = 1.0 - f
    hi = logf.astype(BF16)
    r1 = logf - hi.astype(F32)
    mid = r1.astype(BF16)
    lo = (r1 - mid.astype(F32)).astype(BF16)
    tril = tril_ref[...]
    chunks = [slice(c * chunk, (c + 1) * chunk) for c in range(cps)]
    for rc in chunks:
        e_ref[rc, :] = _dot(tril, hi[rc]) + _dot(tril, mid[rc]) + _dot(tril, lo[rc])
    bcum = e_ref[...]

    def bcast(r, n_rows):
        return jnp.broadcast_to(e_ref[r:r + 1, :], (n_rows, W_MIX))

    def boundary_rows(s):
        if 2 * s >= SUBLANES:
            return jnp.concatenate([bcast(p * 2 * s + s - 1, 2 * s) for p in range(cps * chunk // (2 * s))], axis=0)
        sub = lax.broadcasted_iota(jnp.int32, (SUBLANES, W_MIX), 0)
        groups = []
        for g in range(cps * chunk // SUBLANES):
            acc = bcast(g * SUBLANES + s - 1, SUBLANES)
            for p in range(1, SUBLANES // (2 * s)):
                acc = jnp.where(sub >= p * 2 * s, bcast(g * SUBLANES + p * 2 * s + s - 1, SUBLANES), acc)
            groups.append(acc)
        return jnp.concatenate(groups, axis=0)

    q = q_ref[...].astype(F32)
    row = lax.broadcasted_iota(jnp.int32, (cps * chunk, W_MIX), 0)
    x_ref[levels] = q.astype(BF16)
    x_ref[levels + 1] = kk.astype(BF16)
    for lv in range(levels):
        s = chunk >> (lv + 1)
        qside = ((row // s) % 2) == 1
        if s == 1:
            x_ref[lv] = jnp.where(qside, q * f, kk).astype(BF16)
        else:
            d = bcum - boundary_rows(s)
            x_ref[lv] = (jnp.where(qside, q, kk) * jnp.exp(jnp.where(qside, d, -d))).astype(BF16)
    b_last = jnp.concatenate([bcast(rc.stop - 1, chunk) for rc in chunks], axis=0)
    x_ref[levels + 2] = (q * jnp.exp(bcum)).astype(BF16)
    x_ref[levels + 3] = (kk * jnp.exp(b_last - bcum)).astype(BF16)
    decays = [jnp.exp(e_ref[rc.stop - 1:rc.stop, :]) for rc in chunks]
    nw = nw_ref[...]

    heads = [slice(h * HEAD_DIM, (h + 1) * HEAD_DIM) for h in range(N_HEADS)]
    amats = {}
    for c, rc in enumerate(chunks):
        for h, hs in enumerate(heads):
            amat = mask_ref[levels] * _dot_nt(x_ref[levels, rc, hs], x_ref[levels + 1, rc, hs])
            for lv in range(levels):
                x = x_ref[lv, rc, hs]
                amat = amat + mask_ref[lv] * _dot_nt(x, x)
            amats[c, h] = amat.astype(BF16)
    states = [st_ref[h] for h in range(N_HEADS)]
    outs = {}
    for c, rc in enumerate(chunks):
        for h, hs in enumerate(heads):
            outs[c, h] = (_dot(amats[c, h], i_ref[rc, hs])
                          + _dot_nt(x_ref[levels + 2, rc, hs], states[h].astype(BF16)))
        for h, hs in enumerate(heads):
            states[h] = states[h] * decays[c][:, hs] + _dot_tn(i_ref[rc, hs], x_ref[levels + 3, rc, hs])
    for h in range(N_HEADS):
        st_ref[h] = states[h]
    for c, rc in enumerate(chunks):
        for h, hs in enumerate(heads):
            y_ref[rc, hs] = (_silu(g_ref[rc, hs].astype(F32)) * _rms(outs[c, h], nw)).astype(y_ref.dtype)

    @pl.when(t == pl.num_programs(1) - 1)
    def _():
        for h in range(N_HEADS):
            sout_ref[h] = st_ref[h].T


def hgrn2(proj, fg, log_lb, log1m_lb, norm_w, s0, s0_layer, batch, seq, chunk, cps):
    n_steps = seq // (chunk * cps)
    rows = chunk * cps
    tril_np, mask_np, levels = _hgrn_level_masks(chunk)
    tril = jnp.asarray(tril_np, dtype=BF16)
    masks = jnp.asarray(mask_np)

    def col(c):
        return pl.BlockSpec((rows, W_MIX), lambda b, t: (b * n_steps + t, c))

    def const(shape):
        return pl.BlockSpec(shape, lambda b, t: (0,) * len(shape))

    kernel = functools.partial(_hgrn_kernel, chunk=chunk, levels=levels, cps=cps)
    y, s_out = pl.pallas_call(
        kernel,
        grid=(batch, n_steps),
        in_specs=[
            col(COL_QG), col(0), col(COL_IG), col(COL_GG),
            const((1, W_MIX)), const((1, W_MIX)),
            const((chunk, chunk)),
            const((levels + 1, chunk, chunk)),
            const((1, HEAD_DIM)),
            _state_spec(s0_layer),
        ],
        out_specs=[
            pl.BlockSpec((rows, W_MIX), lambda b, t: (b * n_steps + t, 0)),
            pl.BlockSpec((None, N_HEADS, HEAD_DIM, HEAD_DIM), lambda b, t: (b, 0, 0, 0)),
        ],
        out_shape=[
            jax.ShapeDtypeStruct((batch * seq, W_MIX), BF16),
            jax.ShapeDtypeStruct((batch, N_HEADS, HEAD_DIM, HEAD_DIM), F32),
        ],
        scratch_shapes=[
            pltpu.VMEM((N_HEADS, HEAD_DIM, HEAD_DIM), F32),
            pltpu.VMEM((rows, W_MIX), F32),
            pltpu.VMEM((levels + 4, rows, W_MIX), BF16),
        ],
        compiler_params=_params("parallel", "arbitrary"),
        name="hgrn2",
    )(proj, fg, proj, proj, log_lb, log1m_lb, tril, masks, norm_w.reshape(1, HEAD_DIM), s0)
    return y, s_out


def _merge_kernel(x_ref, gpre_ref, ya_ref, yr_ref, yg_ref, wg0_ref, wg1_ref, wg2_ref, wb_ref, wo_ref,
                  gpost_ref, o_ref, h_ref, *, tm, rs):
    j = pl.program_id(1)

    @pl.when(j == 0)
    def _():
        for rows in _row_blocks(tm):
            h_ref[rows, :] = _rms(x_ref[rows, :], gpre_ref[...]).astype(BF16)
        o_ref[...] = jnp.zeros_like(o_ref)

    for r in range(tm // rs):
        rows = slice(r * rs, (r + 1) * rs)
        h = h_ref[rows, :]
        merged = None
        for n, (y_ref, wg_ref) in enumerate(((ya_ref, wg0_ref), (yr_ref, wg1_ref), (yg_ref, wg2_ref))):
            gate = jax.nn.sigmoid(_dot(h, wg_ref[...]))
            term = gate * _dot(y_ref[rows, :], wb_ref[n])
            merged = term if merged is None else merged + term
        o_ref[rows, :] += _dot(merged.astype(BF16), wo_ref[...])

    @pl.when(j == pl.num_programs(1) - 1)
    def _():
        for rows in _row_blocks(tm):
            o_ref[rows, :] = x_ref[rows, :] + _rms(o_ref[rows, :], gpost_ref[...])


def merge(x, g_pre, ya, yr, yg, w_gate, w_branch, w_out, g_post, layer, tm, tn):
    rows, d = x.shape
    nj = d // tn

    def gate_spec(n):
        return pl.BlockSpec((None, d, tn), lambda i, j: (layer, 0, n * nj + j))

    kernel = functools.partial(_merge_kernel, tm=tm, rs=min(tm, ROW_SUB))
    single = pl.Buffered(1)
    return pl.pallas_call(
        kernel,
        grid=(rows // tm, nj),
        in_specs=[
            pl.BlockSpec((tm, d), lambda i, j: (i, 0)),
            pl.BlockSpec((1, d), lambda i, j: (0, 0)),
            pl.BlockSpec((tm, W_MIX), lambda i, j: (i, 0), pipeline_mode=single),
            pl.BlockSpec((tm, W_MIX), lambda i, j: (i, 0), pipeline_mode=single),
            pl.BlockSpec((tm, W_MIX), lambda i, j: (i, 0), pipeline_mode=single),
            gate_spec(0), gate_spec(1), gate_spec(2),
            pl.BlockSpec((None, N_BRANCH, W_MIX, tn), lambda i, j: (layer, 0, 0, j)),
            pl.BlockSpec((None, tn, d), lambda i, j: (layer, j, 0)),
            pl.BlockSpec((1, d), lambda i, j: (0, 0)),
        ],
        out_specs=pl.BlockSpec((tm, d), lambda i, j: (i, 0)),
        out_shape=jax.ShapeDtypeStruct((rows, d), F32),
        scratch_shapes=[pltpu.VMEM((tm, d), BF16)],
        compiler_params=_params("parallel", "arbitrary"),
        name="merge",
    )(x, g_pre, ya, yr, yg, w_gate, w_gate, w_gate, w_branch, w_out, g_post)


HALO = BF16_ROWS


def _ffn_kernel(x_ref, xprev_ref, cst_ref, gpre_ref, wup_ref, wval_ref, cw_ref, cb_ref, wdn_ref, gpost_ref,
                o_ref, uplast_ref, h_ref, up_ref, *, tm, seq):
    i = pl.program_id(0)
    j = pl.program_id(1)
    seqs_per_tile = max(tm // seq, 1)
    tiles_per_seq = max(seq // tm, 1)

    @pl.when(j == 0)
    def _():
        g = gpre_ref[...]
        h_ref[0:HALO, :] = _rms(xprev_ref[...], g).astype(BF16)
        for rows in _row_blocks(tm):
            h_ref[HALO + rows.start:HALO + rows.stop, :] = _rms(x_ref[rows, :], g).astype(BF16)
        o_ref[...] = jnp.zeros_like(o_ref)

    cw = cw_ref[...]
    wdn = wdn_ref[...].astype(BF16)
    rs = min(tm, ROW_SUB)
    n_sub = tm // rs

    def up_block(r):
        lo = 0 if r == 0 else HALO + r * rs
        hi = HALO + (r + 1) * rs
        up_ref[lo:hi, :] = _dot(h_ref[lo:hi, :], wup_ref[...])

    up_block(0)
    if seqs_per_tile == 1:
        first = i % tiles_per_seq == 0
        halo_rows = slice(HALO - (CONV_W - 1), HALO)
        up_ref[halo_rows, :] = jnp.where(first, cst_ref[0], up_ref[halo_rows, :])
    for r in range(n_sub):
        if r + 1 < n_sub:
            up_block(r + 1)
        r0 = HALO + r * rs
        tap2 = up_ref[r0 - 2:r0 - 2 + rs, :]
        tap1 = up_ref[r0 - 1:r0 - 1 + rs, :]
        tap0 = up_ref[r0:r0 + rs, :]
        if seqs_per_tile > 1:
            tf = tap0.shape[1]
            st = cst_ref[...]
            st0 = jnp.broadcast_to(st[:, 0:1, :], (seqs_per_tile, seq, tf)).reshape(tm, tf)
            st1 = jnp.broadcast_to(st[:, 1:2, :], (seqs_per_tile, seq, tf)).reshape(tm, tf)
            pos = lax.broadcasted_iota(jnp.int32, (tm, tf), 0) % seq
            tap1 = jnp.where(pos == 0, st1, tap1)
            tap2 = jnp.where(pos == 0, st0, jnp.where(pos == 1, st1, tap2))
        conv = cb_ref[...] + tap2 * cw[0:1, :] + tap1 * cw[1:2, :] + tap0 * cw[2:3, :]
        val = _dot(h_ref[r0:r0 + rs, :], wval_ref[...])
        act = 0.5 * conv * (1.0 + lax.erf(conv * math.sqrt(0.5))) * val
        o_ref[r * rs:(r + 1) * rs, :] += _dot(act.astype(BF16), wdn)

    keep = uplast_ref.shape[0]
    uplast_ref[...] = up_ref[HALO + tm - keep:HALO + tm, :]

    @pl.when(j == pl.num_programs(1) - 1)
    def _():
        for rows in _row_blocks(tm):
            o_ref[rows, :] = x_ref[rows, :] + _rms(o_ref[rows, :], gpost_ref[...])


def ffn(x, conv_state, cs_layer, g_pre, w_up, w_val, conv_w, conv_b, w_down, g_post, layer, batch, seq, tm, tf):
    rows, d = x.shape
    dff = w_up.shape[2]
    n_tiles = rows // tm
    seqs_per_tile = max(tm // seq, 1)
    tiles_per_seq = max(seq // tm, 1)
    assert rows == batch * seq and (tm % seq == 0 or seq % tm == 0)
    keep = tm if seqs_per_tile > 1 else SUBLANES
    halo_blocks = tm // HALO
    kernel = functools.partial(_ffn_kernel, tm=tm, seq=seq)
    out, uplast = pl.pallas_call(
        kernel,
        grid=(n_tiles, dff // tf),
        in_specs=[
            pl.BlockSpec((tm, d), lambda i, j: (i, 0)),
            pl.BlockSpec((HALO, d), lambda i, j: (jnp.maximum(i * halo_blocks - 1, 0), 0)),
            pl.BlockSpec((None, seqs_per_tile, CONV_W - 1, tf),
                         lambda i, j: (cs_layer, i * seqs_per_tile // tiles_per_seq, 0, j)),
            pl.BlockSpec((1, d), lambda i, j: (0, 0)),
            pl.BlockSpec((None, d, tf), lambda i, j: (layer, 0, j)),
            pl.BlockSpec((None, d, tf), lambda i, j: (layer, 0, j)),
            pl.BlockSpec((CONV_W, tf), lambda i, j: (0, j)),
            pl.BlockSpec((1, tf), lambda i, j: (0, j)),
            pl.BlockSpec((None, tf, d), lambda i, j: (layer, j, 0)),
            pl.BlockSpec((1, d), lambda i, j: (0, 0)),
        ],
        out_specs=[
            pl.BlockSpec((tm, d), lambda i, j: (i, 0)),
            pl.BlockSpec((keep, tf), lambda i, j: (i, j)),
        ],
        out_shape=[
            jax.ShapeDtypeStruct((rows, d), F32),
            jax.ShapeDtypeStruct((n_tiles * keep, dff), F32),
        ],
        scratch_shapes=[
            pltpu.VMEM((HALO + tm, d), BF16),
            pltpu.VMEM((HALO + tm, tf), F32),
        ],
        compiler_params=_params("parallel", "arbitrary"),
        name="ffn",
    )(x, x, conv_state, g_pre, w_up, w_val, conv_w, conv_b, w_down, g_post)
    new_conv = uplast.reshape(batch, -1, dff)[:, -(CONV_W - 1):, :]
    return out, new_conv


def _layer(x, batch, seq, pos0, layer, caches, cache_layer, vecs, mats, tiles):
    k_past, v_past, ret_s0, hgrn_s0, conv_s0 = caches
    (g_mix_pre, g_mix_post, g_ffn_pre, g_ffn_post, rel_bias, ret_norm_w, log_lb, log1m_lb, hgrn_norm_w,
     conv_w, conv_b) = vecs
    w_in, w_gate, w_branch, w_out, w_up, w_val, w_down = mats
    rows = batch * seq
    tn = W_MIX
    f32_cols = (COL_FG,) if k_past is None else (COL_FG, COL_KA, COL_VA)
    proj, fg, *kv_f32 = norm_proj(x, g_mix_pre, w_in, layer, tiles["tm_proj"], tn, f32_cols=f32_cols)

    chunk = min(CHUNK, seq)
    past = ATT_PAST_ROWS
    hps = tiles["att_heads"]
    hb = N_HEADS // hps
    if k_past is None:
        keep = min(past, seq)
        (kv_new,) = norm_proj(x, g_mix_pre, w_in, layer, keep, tn, row_block0=seq // keep - 1, n_row_blocks=1,
                              col_block0=COL_KA, n_col_blocks=2, out_dtype=F32)
        group, sub = tiles["att_group"], tiles["att_sub"]
        gp = group // past
        tables = _attention_tables(rel_bias, sub, chunk, past, pos0, group)
        y_a = attention(
            proj, lambda h, g: (g, h),
            proj, (), lambda h, g: (jnp.maximum(g * gp - 1, 0), hb + h),
            proj, lambda h, g: (g, hb + h),
            proj, lambda h, g: (jnp.maximum(g * gp - 1, 0), 2 * hb + h),
            proj, lambda h, g: (g, 2 * hb + h),
            tables, seq // group, group, sub, past, hps)
    else:
        keep = seq
        kv_new = jnp.concatenate(kv_f32, axis=1)
        group = sub = seq
        tables = _attention_tables(rel_bias, sub, chunk, past, pos0, group)
        n_l = k_past.shape[0]
        kp = k_past.reshape(n_l, batch * past, W_MIX)
        vp = v_past.reshape(n_l, batch * past, W_MIX)
        y_a = attention(
            proj, lambda h, g: (g, h),
            kp, (None,), lambda h, g: (cache_layer, g, h),
            proj, lambda h, g: (g, hb + h),
            vp, lambda h, g: (cache_layer, g, h),
            proj, lambda h, g: (g, 2 * hb + h),
            tables, batch, group, sub, past, hps)

    y_r, ret_s = retention(proj, ret_norm_w, ret_s0, cache_layer, batch, seq, tiles["ret_chunk"], pos0)
    y_g, hgrn_s = hgrn2(proj, fg, log_lb, log1m_lb, hgrn_norm_w, hgrn_s0, cache_layer, batch, seq, chunk,
                          tiles["hgrn_cps"])

    x = merge(x, g_mix_pre, y_a, y_r, y_g, w_gate, w_branch, w_out, g_mix_post, layer,
              tiles["tm_merge"], tiles["tn_merge"])
    x, new_conv = ffn(x, conv_s0, cache_layer, g_ffn_pre, w_up, w_val, conv_w, conv_b, w_down, g_ffn_post, layer,
                      batch, seq, tiles["tm_ffn"], tiles["tf_ffn"])
    k_new = kv_new[:, :W_MIX].reshape(batch, keep, N_HEADS, HEAD_DIM)
    v_new = kv_new[:, W_MIX:].reshape(batch, keep, N_HEADS, HEAD_DIM)
    return x, k_new, v_new, ret_s, hgrn_s, new_conv


PROMPT_TILES = dict(tm_proj=1024, att_group=1024, att_sub=256, att_heads=4, ret_chunk=256, hgrn_cps=8, tm_merge=1024, tn_merge=256, tm_ffn=1024, tf_ffn=512)


def _sample_tiles(rows, seq):
    return dict(tm_proj=rows, att_heads=N_HEADS, ret_chunk=seq, hgrn_cps=1, tm_merge=rows, tn_merge=256, tm_ffn=rows, tf_ffn=512)


def kernel(x_prompt, x_sample, cache_attn_k, cache_attn_v, state_ret, state_hgrn, state_ffn_conv,
           norm_mix_pre, norm_mix_post, norm_ffn_pre, norm_ffn_post, w_in, attn_rel_bias, ret_norm_w,
           hgrn_lower_bound, hgrn_norm_w, w_gate, w_branch, w_out, ffn_w_up, ffn_w_val, ffn_conv_w,
           ffn_conv_b, ffn_w_down):
    depth = w_in.shape[0]
    bp, tp, d = x_prompt.shape
    bs, ts, _ = x_sample.shape

    lb_cum = jnp.cumsum(jax.nn.softmax(hgrn_lower_bound.astype(F32), axis=0), axis=0)
    lb_layers = lb_cum - lb_cum[0:1]
    log_lb = jnp.log(lb_layers)
    log1m_lb = jnp.log1p(-lb_layers)

    bf = lambda w: w.astype(BF16)
    mats = (w_in, bf(w_gate), bf(w_branch), bf(w_out), bf(ffn_w_up), bf(ffn_w_val), ffn_w_down)

    zs = jnp.zeros((1, bp, N_HEADS, HEAD_DIM, HEAD_DIM), F32)
    zc = jnp.zeros((1, bp, CONV_W - 1, D_FF), F32)
    caches_p = (None, None, zs, zs, zc)
    caches_s = (cache_attn_k, cache_attn_v, state_ret, state_hgrn, state_ffn_conv)

    xp = x_prompt.reshape(bp * tp, d)
    xs = x_sample.reshape(bs * ts, d)
    outs_p = [[] for _ in range(5)]
    outs_s = [[] for _ in range(5)]
    for l in range(depth):
        vecs = (norm_mix_pre[l][None], norm_mix_post[l][None], norm_ffn_pre[l][None], norm_ffn_post[l][None],
                attn_rel_bias[l], ret_norm_w[l], log_lb[l][None], log1m_lb[l][None], hgrn_norm_w[l],
                ffn_conv_w[l], ffn_conv_b[l][None])
        res_p = _layer(xp, bp, tp, 0, l, caches_p, 0, vecs, mats, PROMPT_TILES)
        res_s = _layer(xs, bs, ts, PAST_LEN, l, caches_s, l, vecs, mats, _sample_tiles(bs * ts, ts))
        xp, xs = res_p[0], res_s[0]
        for lst, val in zip(outs_p, res_p[1:]):
            lst.append(val)
        for lst, val in zip(outs_s, res_s[1:]):
            lst.append(val)

    return (xp.reshape(bp, tp, d), xs.reshape(bs, ts, d),
            *(jnp.stack(o) for o in outs_p), *(jnp.stack(o) for o in outs_s))
```

```python
import functools
import math

import numpy as np
import jax
import jax.numpy as jnp
from jax import lax
from jax.experimental import pallas as pl
from jax.experimental.pallas import tpu as pltpu

F32 = jnp.float32
BF16 = jnp.bfloat16

D_MODEL = 2048
CHUNK = 64
BAND_CHUNKS = 8
ATT_PAST_ROWS = BAND_CHUNKS * CHUNK
W_MIX = D_MODEL // 2
N_HEADS = 8
HEAD_DIM = W_MIX // N_HEADS
REL_CLIP = 128
N_BRANCH = 3
N_IN = 11 * W_MIX
D_FF = 5632
CONV_W = 3
ROPE_BASE = 10000.0
EPS = 1e-6
PAST_LEN = 1024

COL_QA, COL_KA, COL_VA, COL_QR, COL_KR, COL_VR, COL_GR, COL_QG, COL_FG, COL_IG, COL_GG = range(11)

V7X_VMEM_BYTES = 64 * 1024 * 1024
VMEM_LIMIT = V7X_VMEM_BYTES - 4 * 1024 * 1024
SUBLANES = 8
LANES = 128
BF16_ROWS = 16
ROW_SUB = 512
PROLOGUE_ROWS = 256

NT_DIMS = (((1,), (1,)), ((), ()))
TN_DIMS = (((0,), (0,)), ((), ()))


def _params(*sem):
    return pltpu.CompilerParams(dimension_semantics=sem, vmem_limit_bytes=VMEM_LIMIT)


def _rms(x, g):
    return x * lax.rsqrt(jnp.mean(x * x, axis=-1, keepdims=True) + EPS) * g


def _dot(a, b):
    return jnp.dot(a, b, preferred_element_type=F32)


def _dot_nt(a, b):
    return lax.dot_general(a, b, NT_DIMS, preferred_element_type=F32)


def _dot_tn(a, b):
    return lax.dot_general(a, b, TN_DIMS, preferred_element_type=F32)


def _silu(x):
    return x * jax.nn.sigmoid(x)


def _row_blocks(n):
    step = min(n, PROLOGUE_ROWS)
    return [slice(r, r + step) for r in range(0, n, step)]


def _norm_proj_kernel(x_ref, g_ref, w_ref, o_ref, *rest, f32_cols):
    f32_refs, h_ref = rest[:-1], rest[-1]
    j = pl.program_id(1)

    @pl.when(j == 0)
    def _():
        for rows in _row_blocks(x_ref.shape[0]):
            h_ref[rows, :] = _rms(x_ref[rows, :], g_ref[...]).astype(BF16)

    acc = _dot(h_ref[...], w_ref[...].astype(BF16))
    o_ref[...] = acc.astype(o_ref.dtype)
    for col, ref in zip(f32_cols, f32_refs):
        @pl.when(j == col)
        def _():
            ref[...] = acc


def norm_proj(x, g, w, layer, tm, tn, f32_cols=(), row_block0=0, n_row_blocks=None, col_block0=0,
              n_col_blocks=None, out_dtype=BF16):
    d = x.shape[1]
    n_row_blocks = x.shape[0] // tm - row_block0 if n_row_blocks is None else n_row_blocks
    n_col_blocks = w.shape[2] // tn - col_block0 if n_col_blocks is None else n_col_blocks
    rows, n = n_row_blocks * tm, n_col_blocks * tn
    kernel = functools.partial(_norm_proj_kernel, f32_cols=tuple(f32_cols))
    outs = pl.pallas_call(
        kernel,
        grid=(n_row_blocks, n_col_blocks),
        in_specs=[
            pl.BlockSpec((tm, d), lambda i, j: (row_block0 + i, 0)),
            pl.BlockSpec((1, d), lambda i, j: (0, 0)),
            pl.BlockSpec((None, d, tn), lambda i, j: (layer, 0, col_block0 + j)),
        ],
        out_specs=[pl.BlockSpec((tm, tn), lambda i, j: (i, j))]
        + [pl.BlockSpec((tm, tn), lambda i, j: (i, 0)) for _ in f32_cols],
        out_shape=[jax.ShapeDtypeStruct((rows, n), out_dtype)]
        + [jax.ShapeDtypeStruct((rows, tn), F32) for _ in f32_cols],
        scratch_shapes=[pltpu.VMEM((tm, d), BF16)],
        compiler_params=_params("parallel", "arbitrary"),
        name="norm_proj",
    )(x, g, w)
    return outs


def _attention_kernel(q_ref, kp_ref, ko_ref, vp_ref, vo_ref, *rest, past, group, sub, hps, past_by_row):
    n_sub = group // sub
    n_first = len(rest) - 5
    z_ref, band_ref, first_refs, (o_ref, kw_ref, vw_ref) = rest[0], rest[1], rest[2:2 + n_first], rest[2 + n_first:]
    span = past + sub
    if not past_by_row:
        kw_ref[0:past, :] = kp_ref[...].astype(BF16)
    kw_ref[past:past + group, :] = ko_ref[...].astype(BF16)
    for hh in range(hps):
        hs = slice(hh * HEAD_DIM, (hh + 1) * HEAD_DIM)
        vs = slice(2 * hh * HEAD_DIM, (2 * hh + 1) * HEAD_DIM)
        if past_by_row:
            head_rows = pl.ds(hh, past, stride=hps)
            kw_ref[0:past, hs] = kp_ref[head_rows, :].astype(BF16)
            vw_ref[0:past, vs] = vp_ref[head_rows, :].astype(BF16)
        else:
            vw_ref[0:past, vs] = vp_ref[:, hs].astype(BF16)
        vw_ref[past:past + group, vs] = vo_ref[:, hs].astype(BF16)
        vw_ref[:, (2 * hh + 1) * HEAD_DIM:(2 * hh + 2) * HEAD_DIM] = jnp.ones((past + group, HEAD_DIM), BF16)
    scale = HEAD_DIM ** -0.5
    period = z_ref.shape[-1]
    band = band_ref[...]
    bias = [pltpu.roll(jnp.broadcast_to(z_ref[hh], (sub, period)), 0, 1, stride=1, stride_axis=0)[:, :span] + band
            for hh in range(hps)]
    steps = [(hh, s) for hh in range(hps) for s in range(n_sub)]
    scores = []
    for hh, s in steps:
        hs = slice(hh * HEAD_DIM, (hh + 1) * HEAD_DIM)
        q = (q_ref[s * sub:(s + 1) * sub, hs].astype(F32) * scale).astype(BF16)
        sc = _dot_nt(q, kw_ref[s * sub:s * sub + span, hs]) + bias[hh]
        scores.append(sc + first_refs[s][...] if s < n_first else sc)
    probs = [jnp.exp(sc - jnp.max(sc, axis=-1, keepdims=True)).astype(BF16) for sc in scores]
    for (hh, s), p in zip(steps, probs):
        ov = _dot(p, vw_ref[s * sub:s * sub + span, 2 * hh * HEAD_DIM:(2 * hh + 2) * HEAD_DIM])
        o_ref[s * sub:(s + 1) * sub, hh * HEAD_DIM:(hh + 1) * HEAD_DIM] = (
            ov[:, 0:HEAD_DIM] / ov[:, HEAD_DIM:2 * HEAD_DIM]).astype(o_ref.dtype)


def _n_first_group_variants(sub, past, pos0, group):
    return min(group // sub, max(0, -(-(past - pos0) // sub)))


def _attention_tables(rel_bias, sub, chunk, past, pos0, group):
    width = past + sub
    m = np.arange(width + sub - 1)
    idx = np.clip(past + (sub - 1) - m, -REL_CLIP, REL_CLIP) + REL_CLIP
    n_hi = int(np.sum(idx == idx[0]))
    n_lo = int(np.sum(idx == idx[-1]))
    mid = idx[n_hi:idx.size - n_lo]
    assert mid.size and np.all(np.diff(mid) == -1)
    v = jnp.concatenate([jnp.broadcast_to(rel_bias[:, idx[0]:idx[0] + 1], (N_HEADS, n_hi)),
                         jnp.flip(rel_bias[:, int(mid[-1]):int(mid[0]) + 1], axis=1),
                         jnp.broadcast_to(rel_bias[:, idx[-1]:idx[-1] + 1], (N_HEADS, n_lo))], axis=1)
    period = -(-(width + sub) // LANES) * LANES
    z = jnp.concatenate([v[:, sub - 1:], jnp.zeros((N_HEADS, period - v.shape[1]), F32), v[:, :sub - 1]], axis=1)
    i = np.arange(sub)[:, None]
    j = np.arange(width)[None, :]
    band = (j >= (i // chunk) * chunk) & (j < (i // chunk) * chunk + past + chunk)
    first = [np.ones_like(band)]
    for s in range(_n_first_group_variants(sub, past, pos0, group)):
        first.append(np.broadcast_to(j + (pos0 - past + s * sub) >= 0, band.shape))
    as_mask = lambda ok: jnp.asarray(np.where(ok, 0.0, -np.inf).astype(np.float32))
    return z[:, None, :], as_mask(band), as_mask(np.stack(first))


def attention(q_arr, q_map, kp_arr, kp_lead, kp_map, ko_arr, ko_map, vp_arr, vp_map, vo_arr, vo_map, tables,
              n_groups, group, sub, past, hps, past_by_row=False):
    z, band, first = tables
    rows = n_groups * group
    n_sub = group // sub
    n_first = first.shape[0] - 1
    kernel = functools.partial(_attention_kernel, past=past, group=group, sub=sub, hps=hps, past_by_row=past_by_row)
    past_block = kp_lead + ((past * hps, HEAD_DIM) if past_by_row else (past, hps * HEAD_DIM))
    width = past + sub
    lanes = hps * HEAD_DIM

    def first_spec(s):
        return pl.BlockSpec((None, sub, width), lambda h, g: (jnp.where(g == 0, 1 + s, 0), 0, 0))

    return pl.pallas_call(
        kernel,
        grid=(N_HEADS // hps, n_groups),
        in_specs=[
            pl.BlockSpec((group, lanes), q_map),
            pl.BlockSpec(past_block, kp_map),
            pl.BlockSpec((group, lanes), ko_map),
            pl.BlockSpec(past_block, vp_map),
            pl.BlockSpec((group, lanes), vo_map),
            pl.BlockSpec((hps, 1, z.shape[-1]), lambda h, g: (h, 0, 0)),
            pl.BlockSpec((sub, width), lambda h, g: (0, 0)),
        ] + [first_spec(s) for s in range(n_first)],
        out_specs=pl.BlockSpec((group, lanes), lambda h, g: (g, h)),
        out_shape=jax.ShapeDtypeStruct((rows, W_MIX), BF16),
        scratch_shapes=[pltpu.VMEM((past + group, lanes), BF16),
                        pltpu.VMEM((past + group, 2 * lanes), BF16)],
        compiler_params=_params("parallel", "arbitrary"),
        name="attention",
    )(q_arr, kp_arr, ko_arr, vp_arr, vo_arr, z, band, *([first] * n_first))


def _retention_kernel(q_ref, k_ref, v_ref, g_ref, cos_ref, sin_ref, di_ref, dq_ref, dk_ref, ds_ref,
                      nw_ref, s0_ref, y_ref, sout_ref, s_ref):
    t = pl.program_id(1)

    @pl.when(t == 0)
    def _():
        s_ref[...] = s0_ref[...]

    cosf = cos_ref[...]
    sinf = sin_ref[...]
    nw = nw_ref[...]
    qscale = HEAD_DIM ** -0.5

    heads = [slice(h * HEAD_DIM, (h + 1) * HEAD_DIM) for h in range(N_HEADS)]
    qs, ks, qds, kds = [], [], [], []
    for h, hs in enumerate(heads):
        q = q_ref[:, hs].astype(F32)
        k = k_ref[:, hs].astype(F32)
        q = (q * cosf + pltpu.roll(q, HEAD_DIM // 2, 1) * sinf) * qscale
        k = k * cosf + pltpu.roll(k, HEAD_DIM // 2, 1) * sinf
        qs.append(q.astype(BF16))
        ks.append(k.astype(BF16))
        qds.append((q * dq_ref[h]).astype(BF16))
        kds.append((k * dk_ref[h]).astype(BF16))
    amats = [(_dot_nt(qs[h], ks[h]) * di_ref[h]).astype(BF16) for h in range(N_HEADS)]
    outs = [_dot(amats[h], v_ref[:, hs]) + _dot(qds[h], s_ref[h].astype(BF16)) for h, hs in enumerate(heads)]
    for h, hs in enumerate(heads):
        s_ref[h] = s_ref[h] * ds_ref[h] + _dot_tn(kds[h], v_ref[:, hs])
    for h, hs in enumerate(heads):
        o = outs[h]
        mu = jnp.mean(o, axis=-1, keepdims=True)
        oc = o - mu
        var = jnp.mean(oc * oc, axis=-1, keepdims=True)
        on = oc * lax.rsqrt(var + EPS) * nw
        y_ref[:, hs] = (_silu(g_ref[:, hs].astype(F32)) * on).astype(y_ref.dtype)

    @pl.when(t == pl.num_programs(1) - 1)
    def _():
        sout_ref[...] = s_ref[...]


def _retention_tables(chunk, pos):
    log_gamma = jnp.log1p(-(2.0 ** (-5.0 - jnp.arange(N_HEADS, dtype=F32))))
    idx = jnp.arange(chunk, dtype=F32)
    diff = idx[:, None] - idx[None, :]
    d_intra = jnp.where(diff[None] >= 0, jnp.exp(jnp.maximum(diff, 0.0)[None] * log_gamma[:, None, None]), 0.0)
    d_q = jnp.exp((idx + 1.0)[None, :] * log_gamma[:, None])
    d_k = jnp.exp((chunk - 1.0 - idx)[None, :] * log_gamma[:, None])
    d_s = jnp.exp(chunk * log_gamma)
    d_q = jnp.broadcast_to(d_q[:, :, None], (N_HEADS, chunk, HEAD_DIM))
    d_k = jnp.broadcast_to(d_k[:, :, None], (N_HEADS, chunk, HEAD_DIM))
    d_s = jnp.broadcast_to(d_s[:, None, None], (N_HEADS, HEAD_DIM, HEAD_DIM))
    inv = ROPE_BASE ** (-jnp.arange(0, HEAD_DIM, 2, dtype=F32) / HEAD_DIM)
    ang = pos.astype(F32)[:, None] * inv[None, :]
    cos, sin = jnp.cos(ang), jnp.sin(ang)
    cosf = jnp.concatenate([cos, cos], axis=-1)
    sinf = jnp.concatenate([-sin, sin], axis=-1)
    return d_intra, d_q, d_k, d_s, cosf, sinf


def _state_spec(layer):
    return pl.BlockSpec((None, None, N_HEADS, HEAD_DIM, HEAD_DIM), lambda b, t: (layer, b, 0, 0, 0))


def retention(proj, norm_w, s0, s0_layer, batch, seq, chunk, pos0):
    n_chunks = seq // chunk
    d_intra, d_q, d_k, d_s, cosf, sinf = _retention_tables(chunk, pos0 + jnp.arange(seq))

    def col(c):
        return pl.BlockSpec((chunk, W_MIX), lambda b, t: (b * n_chunks + t, c))

    def const(shape):
        return pl.BlockSpec(shape, lambda b, t: (0,) * len(shape))

    y, s_out = pl.pallas_call(
        _retention_kernel,
        grid=(batch, n_chunks),
        in_specs=[
            col(COL_QR), col(COL_KR), col(COL_VR), col(COL_GR),
            pl.BlockSpec((chunk, HEAD_DIM), lambda b, t: (t, 0)),
            pl.BlockSpec((chunk, HEAD_DIM), lambda b, t: (t, 0)),
            const((N_HEADS, chunk, chunk)),
            const((N_HEADS, chunk, HEAD_DIM)),
            const((N_HEADS, chunk, HEAD_DIM)),
            const((N_HEADS, HEAD_DIM, HEAD_DIM)),
            const((1, HEAD_DIM)),
            _state_spec(s0_layer),
        ],
        out_specs=[
            pl.BlockSpec((chunk, W_MIX), lambda b, t: (b * n_chunks + t, 0)),
            pl.BlockSpec((None, N_HEADS, HEAD_DIM, HEAD_DIM), lambda b, t: (b, 0, 0, 0)),
        ],
        out_shape=[
            jax.ShapeDtypeStruct((batch * seq, W_MIX), BF16),
            jax.ShapeDtypeStruct((batch, N_HEADS, HEAD_DIM, HEAD_DIM), F32),
        ],
        scratch_shapes=[pltpu.VMEM((N_HEADS, HEAD_DIM, HEAD_DIM), F32)],
        compiler_params=_params("parallel", "arbitrary"),
        name="retention",
    )(proj, proj, proj, proj, cosf, sinf, d_intra, d_q, d_k, d_s, norm_w.reshape(1, HEAD_DIM), s0)
    return y, s_out


def _hgrn_level_masks(chunk):
    levels = int(math.log2(chunk))
    n = np.arange(chunk)
    masks = []
    for lv in range(levels):
        s = chunk >> (lv + 1)
        pair = n // (2 * s)
        qside = (n // s) % 2 == 1
        masks.append((qside[:, None] & ~qside[None, :] & (pair[:, None] == pair[None, :])).astype(np.float32))
    masks.append(np.eye(chunk, dtype=np.float32))
    return np.tril(np.ones((chunk, chunk), np.float32)), np.stack(masks, axis=0), levels


def _hgrn_kernel(q_ref, f_ref, i_ref, g_ref, loglb_ref, log1mlb_ref, tril_ref, mask_ref, nw_ref, s0_ref,
                 y_ref, sout_ref, st_ref, e_ref, x_ref, *, chunk, levels, cps):
    t = pl.program_id(1)

    @pl.when(t == 0)
    def _():
        for h in range(N_HEADS):
            st_ref[h] = s0_ref[h].T

    fg = f_ref[...]
    log_sig = jnp.minimum(fg, 0.0) - jnp.log(1.0 + jnp.exp(-jnp.abs(fg)))
    a = loglb_ref[...]
    b = log1mlb_ref[...] + log_sig
    logf = jnp.maximum(a, b) + jnp.log(1.0 + jnp.exp(-jnp.abs(a - b)))
    f = jnp.exp(logf)
    kk = 1.0 - f
    hi = logf.astype(BF16)
    r1 = logf - hi.astype(F32)
    mid = r1.astype(BF16)
    lo = (r1 - mid.astype(F32)).astype(BF16)
    tril = tril_ref[...]
    chunks = [slice(c * chunk, (c + 1) * chunk) for c in range(cps)]
    for rc in chunks:
        e_ref[rc, :] = _dot(tril, hi[rc]) + _dot(tril, mid[rc]) + _dot(tril, lo[rc])
    bcum = e_ref[...]

    def bcast(r, n_rows):
        return jnp.broadcast_to(e_ref[r:r + 1, :], (n_rows, W_MIX))

    def boundary_rows(s):
        if 2 * s >= SUBLANES:
            return jnp.concatenate([bcast(p * 2 * s + s - 1, 2 * s) for p in range(cps * chunk // (2 * s))], axis=0)
        sub = lax.broadcasted_iota(jnp.int32, (SUBLANES, W_MIX), 0)
        groups = []
        for g in range(cps * chunk // SUBLANES):
            acc = bcast(g * SUBLANES + s - 1, SUBLANES)
            for p in range(1, SUBLANES // (2 * s)):
                acc = jnp.where(sub >= p * 2 * s, bcast(g * SUBLANES + p * 2 * s + s - 1, SUBLANES), acc)
            groups.append(acc)
        return jnp.concatenate(groups, axis=0)

    q = q_ref[...].astype(F32)
    row = lax.broadcasted_iota(jnp.int32, (cps * chunk, W_MIX), 0)
    x_ref[levels] = q.astype(BF16)
    x_ref[levels + 1] = kk.astype(BF16)
    for lv in range(levels):
        s = chunk >> (lv + 1)
        qside = ((row // s) % 2) == 1
        if s == 1:
            x_ref[lv] = jnp.where(qside, q * f, kk).astype(BF16)
        else:
            d = bcum - boundary_rows(s)
            x_ref[lv] = (jnp.where(qside, q, kk) * jnp.exp(jnp.where(qside, d, -d))).astype(BF16)
    b_last = jnp.concatenate([bcast(rc.stop - 1, chunk) for rc in chunks], axis=0)
    x_ref[levels + 2] = (q * jnp.exp(bcum)).astype(BF16)
    x_ref[levels + 3] = (kk * jnp.exp(b_last - bcum)).astype(BF16)
    decays = [jnp.exp(e_ref[rc.stop - 1:rc.stop, :]) for rc in chunks]
    nw = nw_ref[...]

    heads = [slice(h * HEAD_DIM, (h + 1) * HEAD_DIM) for h in range(N_HEADS)]
    amats = {}
    for c, rc in enumerate(chunks):
        for h, hs in enumerate(heads):
            amat = mask_ref[levels] * _dot_nt(x_ref[levels, rc, hs], x_ref[levels + 1, rc, hs])
            for lv in range(levels):
                x = x_ref[lv, rc, hs]
                amat = amat + mask_ref[lv] * _dot_nt(x, x)
            amats[c, h] = amat.astype(BF16)
    states = [st_ref[h] for h in range(N_HEADS)]
    outs = {}
    for c, rc in enumerate(chunks):
        for h, hs in enumerate(heads):
            outs[c, h] = (_dot(amats[c, h], i_ref[rc, hs])
                          + _dot_nt(x_ref[levels + 2, rc, hs], states[h].astype(BF16)))
        for h, hs in enumerate(heads):
            states[h] = states[h] * decays[c][:, hs] + _dot_tn(i_ref[rc, hs], x_ref[levels + 3, rc, hs])
    for h in range(N_HEADS):
        st_ref[h] = states[h]
    for c, rc in enumerate(chunks):
        for h, hs in enumerate(heads):
            y_ref[rc, hs] = (_silu(g_ref[rc, hs].astype(F32)) * _rms(outs[c, h], nw)).astype(y_ref.dtype)

    @pl.when(t == pl.num_programs(1) - 1)
    def _():
        for h in range(N_HEADS):
            sout_ref[h] = st_ref[h].T


def hgrn2(proj, fg, log_lb, log1m_lb, norm_w, s0, s0_layer, batch, seq, chunk, cps):
    n_steps = seq // (chunk * cps)
    rows = chunk * cps
    tril_np, mask_np, levels = _hgrn_level_masks(chunk)
    tril = jnp.asarray(tril_np, dtype=BF16)
    masks = jnp.asarray(mask_np)

    def col(c):
        return pl.BlockSpec((rows, W_MIX), lambda b, t: (b * n_steps + t, c))

    def const(shape):
        return pl.BlockSpec(shape, lambda b, t: (0,) * len(shape))

    kernel = functools.partial(_hgrn_kernel, chunk=chunk, levels=levels, cps=cps)
    y, s_out = pl.pallas_call(
        kernel,
        grid=(batch, n_steps),
        in_specs=[
            col(COL_QG), col(0), col(COL_IG), col(COL_GG),
            const((1, W_MIX)), const((1, W_MIX)),
            const((chunk, chunk)),
            const((levels + 1, chunk, chunk)),
            const((1, HEAD_DIM)),
            _state_spec(s0_layer),
        ],
        out_specs=[
            pl.BlockSpec((rows, W_MIX), lambda b, t: (b * n_steps + t, 0)),
            pl.BlockSpec((None, N_HEADS, HEAD_DIM, HEAD_DIM), lambda b, t: (b, 0, 0, 0)),
        ],
        out_shape=[
            jax.ShapeDtypeStruct((batch * seq, W_MIX), BF16),
            jax.ShapeDtypeStruct((batch, N_HEADS, HEAD_DIM, HEAD_DIM), F32),
        ],
        scratch_shapes=[
            pltpu.VMEM((N_HEADS, HEAD_DIM, HEAD_DIM), F32),
            pltpu.VMEM((rows, W_MIX), F32),
            pltpu.VMEM((levels + 4, rows, W_MIX), BF16),
        ],
        compiler_params=_params("parallel", "arbitrary"),
        name="hgrn2",
    )(proj, fg, proj, proj, log_lb, log1m_lb, tril, masks, norm_w.reshape(1, HEAD_DIM), s0)
    return y, s_out


def _merge_kernel(x_ref, gpre_ref, ya_ref, yr_ref, yg_ref, wg0_ref, wg1_ref, wg2_ref, wb_ref, wo_ref,
                  gpost_ref, o_ref, h_ref, *, tm, rs):
    j = pl.program_id(1)

    @pl.when(j == 0)
    def _():
        for rows in _row_blocks(tm):
            h_ref[rows, :] = _rms(x_ref[rows, :], gpre_ref[...]).astype(BF16)
        o_ref[...] = jnp.zeros_like(o_ref)

    for r in range(tm // rs):
        rows = slice(r * rs, (r + 1) * rs)
        h = h_ref[rows, :]
        merged = None
        for n, (y_ref, wg_ref) in enumerate(((ya_ref, wg0_ref), (yr_ref, wg1_ref), (yg_ref, wg2_ref))):
            gate = jax.nn.sigmoid(_dot(h, wg_ref[...]))
            term = gate * _dot(y_ref[rows, :], wb_ref[n])
            merged = term if merged is None else merged + term
        o_ref[rows, :] += _dot(merged.astype(BF16), wo_ref[...])

    @pl.when(j == pl.num_programs(1) - 1)
    def _():
        for rows in _row_blocks(tm):
            o_ref[rows, :] = x_ref[rows, :] + _rms(o_ref[rows, :], gpost_ref[...])


def merge(x, g_pre, ya, yr, yg, w_gate, w_branch, w_out, g_post, layer, tm, tn):
    rows, d = x.shape
    nj = d // tn

    def gate_spec(n):
        return pl.BlockSpec((None, d, tn), lambda i, j: (layer, 0, n * nj + j))

    kernel = functools.partial(_merge_kernel, tm=tm, rs=min(tm, ROW_SUB))
    single = pl.Buffered(1)
    return pl.pallas_call(
        kernel,
        grid=(rows // tm, nj),
        in_specs=[
            pl.BlockSpec((tm, d), lambda i, j: (i, 0)),
            pl.BlockSpec((1, d), lambda i, j: (0, 0)),
            pl.BlockSpec((tm, W_MIX), lambda i, j: (i, 0), pipeline_mode=single),
            pl.BlockSpec((tm, W_MIX), lambda i, j: (i, 0), pipeline_mode=single),
            pl.BlockSpec((tm, W_MIX), lambda i, j: (i, 0), pipeline_mode=single),
            gate_spec(0), gate_spec(1), gate_spec(2),
            pl.BlockSpec((None, N_BRANCH, W_MIX, tn), lambda i, j: (layer, 0, 0, j)),
            pl.BlockSpec((None, tn, d), lambda i, j: (layer, j, 0)),
            pl.BlockSpec((1, d), lambda i, j: (0, 0)),
        ],
        out_specs=pl.BlockSpec((tm, d), lambda i, j: (i, 0)),
        out_shape=jax.ShapeDtypeStruct((rows, d), F32),
        scratch_shapes=[pltpu.VMEM((tm, d), BF16)],
        compiler_params=_params("parallel", "arbitrary"),
        name="merge",
    )(x, g_pre, ya, yr, yg, w_gate, w_gate, w_gate, w_branch, w_out, g_post)


HALO = BF16_ROWS


def _ffn_kernel(x_ref, xprev_ref, cst_ref, gpre_ref, wup_ref, wval_ref, cw_ref, cb_ref, wdn_ref, gpost_ref,
                o_ref, uplast_ref, h_ref, up_ref, *, tm, seq):
    i = pl.program_id(0)
    j = pl.program_id(1)
    seqs_per_tile = max(tm // seq, 1)
    tiles_per_seq = max(seq // tm, 1)

    @pl.when(j == 0)
    def _():
        g = gpre_ref[...]
        h_ref[0:HALO, :] = _rms(xprev_ref[...], g).astype(BF16)
        for rows in _row_blocks(tm):
            h_ref[HALO + rows.start:HALO + rows.stop, :] = _rms(x_ref[rows, :], g).astype(BF16)
        o_ref[...] = jnp.zeros_like(o_ref)

    cw = cw_ref[...]
    wdn = wdn_ref[...].astype(BF16)
    rs = min(tm, ROW_SUB)
    n_sub = tm // rs

    def up_block(r):
        lo = 0 if r == 0 else HALO + r * rs
        hi = HALO + (r + 1) * rs
        up_ref[lo:hi, :] = _dot(h_ref[lo:hi, :], wup_ref[...])

    up_block(0)
    if seqs_per_tile == 1:
        first = i % tiles_per_seq == 0
        halo_rows = slice(HALO - (CONV_W - 1), HALO)
        up_ref[halo_rows, :] = jnp.where(first, cst_ref[0], up_ref[halo_rows, :])
    for r in range(n_sub):
        if r + 1 < n_sub:
            up_block(r + 1)
        r0 = HALO + r * rs
        tap2 = up_ref[r0 - 2:r0 - 2 + rs, :]
        tap1 = up_ref[r0 - 1:r0 - 1 + rs, :]
        tap0 = up_ref[r0:r0 + rs, :]
        if seqs_per_tile > 1:
            tf = tap0.shape[1]
            st = cst_ref[...]
            st0 = jnp.broadcast_to(st[:, 0:1, :], (seqs_per_tile, seq, tf)).reshape(tm, tf)
            st1 = jnp.broadcast_to(st[:, 1:2, :], (seqs_per_tile, seq, tf)).reshape(tm, tf)
            pos = lax.broadcasted_iota(jnp.int32, (tm, tf), 0) % seq
            tap1 = jnp.where(pos == 0, st1, tap1)
            tap2 = jnp.where(pos == 0, st0, jnp.where(pos == 1, st1, tap2))
        conv = cb_ref[...] + tap2 * cw[0:1, :] + tap1 * cw[1:2, :] + tap0 * cw[2:3, :]
        val = _dot(h_ref[r0:r0 + rs, :], wval_ref[...])
        act = 0.5 * conv * (1.0 + lax.erf(conv * math.sqrt(0.5))) * val
        o_ref[r * rs:(r + 1) * rs, :] += _dot(act.astype(BF16), wdn)

    keep = uplast_ref.shape[0]
    uplast_ref[...] = up_ref[HALO + tm - keep:HALO + tm, :]

    @pl.when(j == pl.num_programs(1) - 1)
    def _():
        for rows in _row_blocks(tm):
            o_ref[rows, :] = x_ref[rows, :] + _rms(o_ref[rows, :], gpost_ref[...])


def ffn(x, conv_state, cs_layer, g_pre, w_up, w_val, conv_w, conv_b, w_down, g_post, layer, batch, seq, tm, tf):
    rows, d = x.shape
    dff = w_up.shape[2]
    n_tiles = rows // tm
    seqs_per_tile = max(tm // seq, 1)
    tiles_per_seq = max(seq // tm, 1)
    assert rows == batch * seq and (tm % seq == 0 or seq % tm == 0)
    keep = tm if seqs_per_tile > 1 else SUBLANES
    halo_blocks = tm // HALO
    kernel = functools.partial(_ffn_kernel, tm=tm, seq=seq)
    out, uplast = pl.pallas_call(
        kernel,
        grid=(n_tiles, dff // tf),
        in_specs=[
            pl.BlockSpec((tm, d), lambda i, j: (i, 0)),
            pl.BlockSpec((HALO, d), lambda i, j: (jnp.maximum(i * halo_blocks - 1, 0), 0)),
            pl.BlockSpec((None, seqs_per_tile, CONV_W - 1, tf),
                         lambda i, j: (cs_layer, i * seqs_per_tile // tiles_per_seq, 0, j)),
            pl.BlockSpec((1, d), lambda i, j: (0, 0)),
            pl.BlockSpec((None, d, tf), lambda i, j: (layer, 0, j)),
            pl.BlockSpec((None, d, tf), lambda i, j: (layer, 0, j)),
            pl.BlockSpec((CONV_W, tf), lambda i, j: (0, j)),
            pl.BlockSpec((1, tf), lambda i, j: (0, j)),
            pl.BlockSpec((None, tf, d), lambda i, j: (layer, j, 0)),
            pl.BlockSpec((1, d), lambda i, j: (0, 0)),
        ],
        out_specs=[
            pl.BlockSpec((tm, d), lambda i, j: (i, 0)),
            pl.BlockSpec((keep, tf), lambda i, j: (i, j)),
        ],
        out_shape=[
            jax.ShapeDtypeStruct((rows, d), F32),
            jax.ShapeDtypeStruct((n_tiles * keep, dff), F32),
        ],
        scratch_shapes=[
            pltpu.VMEM((HALO + tm, d), BF16),
            pltpu.VMEM((HALO + tm, tf), F32),
        ],
        compiler_params=_params("parallel", "arbitrary"),
        name="ffn",
    )(x, x, conv_state, g_pre, w_up, w_val, conv_w, conv_b, w_down, g_post)
    new_conv = uplast.reshape(batch, -1, dff)[:, -(CONV_W - 1):, :]
    return out, new_conv


def _layer(x, batch, seq, pos0, layer, caches, cache_layer, vecs, mats, tiles):
    k_past, v_past, ret_s0, hgrn_s0, conv_s0 = caches
    (g_mix_pre, g_mix_post, g_ffn_pre, g_ffn_post, rel_bias, ret_norm_w, log_lb, log1m_lb, hgrn_norm_w,
     conv_w, conv_b) = vecs
    w_in, w_gate, w_branch, w_out, w_up, w_val, w_down = mats
    rows = batch * seq
    tn = W_MIX
    f32_cols = (COL_FG,) if k_past is None else (COL_FG, COL_KA, COL_VA)
    proj, fg, *kv_f32 = norm_proj(x, g_mix_pre, w_in, layer, tiles["tm_proj"], tn, f32_cols=f32_cols)

    chunk = min(CHUNK, seq)
    past = ATT_PAST_ROWS
    hps = tiles["att_heads"]
    hb = N_HEADS // hps
    if k_past is None:
        keep = min(past, seq)
        (kv_new,) = norm_proj(x, g_mix_pre, w_in, layer, keep, tn, row_block0=seq // keep - 1, n_row_blocks=1,
                              col_block0=COL_KA, n_col_blocks=2, out_dtype=F32)
        group, sub = tiles["att_group"], tiles["att_sub"]
        gp = group // past
        tables = _attention_tables(rel_bias, sub, chunk, past, pos0, group)
        y_a = attention(
            proj, lambda h, g: (g, h),
            proj, (), lambda h, g: (jnp.maximum(g * gp - 1, 0), hb + h),
            proj, lambda h, g: (g, hb + h),
            proj, lambda h, g: (jnp.maximum(g * gp - 1, 0), 2 * hb + h),
            proj, lambda h, g: (g, 2 * hb + h),
            tables, seq // group, group, sub, past, hps)
    else:
        keep = seq
        kv_new = jnp.concatenate(kv_f32, axis=1)
        group = sub = seq
        tables = _attention_tables(rel_bias, sub, chunk, past, pos0, group)
        n_l = k_past.shape[0]
        assert hps == N_HEADS
        kp = k_past.reshape(n_l, batch, past * N_HEADS, HEAD_DIM)
        vp = v_past.reshape(n_l, batch, past * N_HEADS, HEAD_DIM)
        y_a = attention(
            proj, lambda h, g: (g, h),
            kp, (None, None), lambda h, g: (cache_layer, g, 0, 0),
            proj, lambda h, g: (g, hb + h),
            vp, lambda h, g: (cache_layer, g, 0, 0),
            proj, lambda h, g: (g, 2 * hb + h),
            tables, batch, group, sub, past, hps, past_by_row=True)

    y_r, ret_s = retention(proj, ret_norm_w, ret_s0, cache_layer, batch, seq, tiles["ret_chunk"], pos0)
    y_g, hgrn_s = hgrn2(proj, fg, log_lb, log1m_lb, hgrn_norm_w, hgrn_s0, cache_layer, batch, seq, chunk,
                          tiles["hgrn_cps"])

    x = merge(x, g_mix_pre, y_a, y_r, y_g, w_gate, w_branch, w_out, g_mix_post, layer,
              tiles["tm_merge"], tiles["tn_merge"])
    x, new_conv = ffn(x, conv_s0, cache_layer, g_ffn_pre, w_up, w_val, conv_w, conv_b, w_down, g_ffn_post, layer,
                      batch, seq, tiles["tm_ffn"], tiles["tf_ffn"])
    k_new = kv_new[:, :W_MIX].reshape(batch, keep, N_HEADS, HEAD_DIM)
    v_new = kv_new[:, W_MIX:].reshape(batch, keep, N_HEADS, HEAD_DIM)
    return x, k_new, v_new, ret_s, hgrn_s, new_conv


PROMPT_TILES = dict(tm_proj=1024, att_group=1024, att_sub=256, att_heads=4, ret_chunk=256, hgrn_cps=8, tm_merge=1024, tn_merge=256, tm_ffn=1024, tf_ffn=512)


def _sample_tiles(rows, seq):
    return dict(tm_proj=rows, att_heads=N_HEADS, ret_chunk=seq, hgrn_cps=1, tm_merge=rows, tn_merge=256, tm_ffn=rows, tf_ffn=512)


def kernel(x_prompt, x_sample, cache_attn_k, cache_attn_v, state_ret, state_hgrn, state_ffn_conv,
           norm_mix_pre, norm_mix_post, norm_ffn_pre, norm_ffn_post, w_in, attn_rel_bias, ret_norm_w,
           hgrn_lower_bound, hgrn_norm_w, w_gate, w_branch, w_out, ffn_w_up, ffn_w_val, ffn_conv_w,
           ffn_conv_b, ffn_w_down):
    depth = w_in.shape[0]
    bp, tp, d = x_prompt.shape
    bs, ts, _ = x_sample.shape

    lb_cum = jnp.cumsum(jax.nn.softmax(hgrn_lower_bound.astype(F32), axis=0), axis=0)
    lb_layers = lb_cum - lb_cum[0:1]
    log_lb = jnp.log(lb_layers)
    log1m_lb = jnp.log1p(-lb_layers)

    bf = lambda w: w.astype(BF16)
    mats = (w_in, bf(w_gate), bf(w_branch), bf(w_out), bf(ffn_w_up), bf(ffn_w_val), ffn_w_down)

    zs = jnp.zeros((1, bp, N_HEADS, HEAD_DIM, HEAD_DIM), F32)
    zc = jnp.zeros((1, bp, CONV_W - 1, D_FF), F32)
    caches_p = (None, None, zs, zs, zc)
    caches_s = (cache_attn_k, cache_attn_v, state_ret, state_hgrn, state_ffn_conv)

    xp = x_prompt.reshape(bp * tp, d)
    xs = x_sample.reshape(bs * ts, d)
    outs_p = [[] for _ in range(5)]
    outs_s = [[] for _ in range(5)]
    for l in range(depth):
        vecs = (norm_mix_pre[l][None], norm_mix_post[l][None], norm_ffn_pre[l][None], norm_ffn_post[l][None],
                attn_rel_bias[l], ret_norm_w[l], log_lb[l][None], log1m_lb[l][None], hgrn_norm_w[l],
                ffn_conv_w[l], ffn_conv_b[l][None])
        res_p = _layer(xp, bp, tp, 0, l, caches_p, 0, vecs, mats, PROMPT_TILES)
        res_s = _layer(xs, bs, ts, PAST_LEN, l, caches_s, l, vecs, mats, _sample_tiles(bs * ts, ts))
        xp, xs = res_p[0], res_s[0]
        for lst, val in zip(outs_p, res_p[1:]):
            lst.append(val)
        for lst, val in zip(outs_s, res_s[1:]):
            lst.append(val)

    return (xp.reshape(bp, tp, d), xs.reshape(bs, ts, d),
            *(jnp.stack(o) for o in outs_p), *(jnp.stack(o) for o in outs_s))
```

```python
import functools
import math

import numpy as np
import jax
import jax.numpy as jnp
from jax import lax
from jax.experimental import pallas as pl
from jax.experimental.pallas import tpu as pltpu

F32 = jnp.float32
BF16 = jnp.bfloat16

D_MODEL = 2048
CHUNK = 64
BAND_CHUNKS = 8
ATT_PAST_ROWS = BAND_CHUNKS * CHUNK
W_MIX = D_MODEL // 2
N_HEADS = 8
HEAD_DIM = W_MIX // N_HEADS
REL_CLIP = 128
N_BRANCH = 3
N_IN = 11 * W_MIX
D_FF = 5632
CONV_W = 3
ROPE_BASE = 10000.0
EPS = 1e-6
PAST_LEN = 1024

COL_QA, COL_KA, COL_VA, COL_QR, COL_KR, COL_VR, COL_GR, COL_QG, COL_FG, COL_IG, COL_GG = range(11)

V7X_VMEM_BYTES = 64 * 1024 * 1024
VMEM_LIMIT = V7X_VMEM_BYTES - 4 * 1024 * 1024
SUBLANES = 8
LANES = 128
BF16_ROWS = 16
ROW_SUB = 512
PROLOGUE_ROWS = 256

NT_DIMS = (((1,), (1,)), ((), ()))
TN_DIMS = (((0,), (0,)), ((), ()))


def _params(*sem):
    return pltpu.CompilerParams(dimension_semantics=sem, vmem_limit_bytes=VMEM_LIMIT)


def _rms(x, g):
    return x * lax.rsqrt(jnp.mean(x * x, axis=-1, keepdims=True) + EPS) * g


def _dot(a, b):
    return jnp.dot(a, b, preferred_element_type=F32)


def _dot_nt(a, b):
    return lax.dot_general(a, b, NT_DIMS, preferred_element_type=F32)


def _dot_tn(a, b):
    return lax.dot_general(a, b, TN_DIMS, preferred_element_type=F32)


def _silu(x):
    return x * jax.nn.sigmoid(x)


def _row_blocks(n):
    step = min(n, PROLOGUE_ROWS)
    return [slice(r, r + step) for r in range(0, n, step)]


def _norm_proj_kernel(x_ref, g_ref, w_ref, o_ref, *rest, f32_cols):
    f32_refs, h_ref = rest[:-1], rest[-1]
    j = pl.program_id(1)

    @pl.when(j == 0)
    def _():
        for rows in _row_blocks(x_ref.shape[0]):
            h_ref[rows, :] = _rms(x_ref[rows, :], g_ref[...]).astype(BF16)

    acc = _dot(h_ref[...], w_ref[...].astype(BF16))
    o_ref[...] = acc.astype(o_ref.dtype)
    for col, ref in zip(f32_cols, f32_refs):
        @pl.when(j == col)
        def _():
            ref[...] = acc


def norm_proj(x, g, w, layer, tm, tn, f32_cols=(), row_block0=0, n_row_blocks=None, col_block0=0,
              n_col_blocks=None, out_dtype=BF16):
    d = x.shape[1]
    n_row_blocks = x.shape[0] // tm - row_block0 if n_row_blocks is None else n_row_blocks
    n_col_blocks = w.shape[2] // tn - col_block0 if n_col_blocks is None else n_col_blocks
    rows, n = n_row_blocks * tm, n_col_blocks * tn
    kernel = functools.partial(_norm_proj_kernel, f32_cols=tuple(f32_cols))
    outs = pl.pallas_call(
        kernel,
        grid=(n_row_blocks, n_col_blocks),
        in_specs=[
            pl.BlockSpec((tm, d), lambda i, j: (row_block0 + i, 0)),
            pl.BlockSpec((1, d), lambda i, j: (0, 0)),
            pl.BlockSpec((None, d, tn), lambda i, j: (layer, 0, col_block0 + j)),
        ],
        out_specs=[pl.BlockSpec((tm, tn), lambda i, j: (i, j))]
        + [pl.BlockSpec((tm, tn), lambda i, j: (i, 0)) for _ in f32_cols],
        out_shape=[jax.ShapeDtypeStruct((rows, n), out_dtype)]
        + [jax.ShapeDtypeStruct((rows, tn), F32) for _ in f32_cols],
        scratch_shapes=[pltpu.VMEM((tm, d), BF16)],
        compiler_params=_params("parallel", "arbitrary"),
        name="norm_proj",
    )(x, g, w)
    return outs


def _attention_kernel(q_ref, kp_ref, ko_ref, vp_ref, vo_ref, *rest, past, group, sub, hps, past_by_row):
    n_sub = group // sub
    n_first = len(rest) - 5
    z_ref, band_ref, first_refs, (o_ref, kw_ref, vw_ref) = rest[0], rest[1], rest[2:2 + n_first], rest[2 + n_first:]
    span = past + sub
    if not past_by_row:
        kw_ref[0:past, :] = kp_ref[...].astype(BF16)
    kw_ref[past:past + group, :] = ko_ref[...].astype(BF16)
    for hh in range(hps):
        hs = slice(hh * HEAD_DIM, (hh + 1) * HEAD_DIM)
        vs = slice(2 * hh * HEAD_DIM, (2 * hh + 1) * HEAD_DIM)
        if past_by_row:
            head_rows = pl.ds(hh, past, stride=hps)
            kw_ref[0:past, hs] = kp_ref[head_rows, :].astype(BF16)
            vw_ref[0:past, vs] = vp_ref[head_rows, :].astype(BF16)
        else:
            vw_ref[0:past, vs] = vp_ref[:, hs].astype(BF16)
        vw_ref[past:past + group, vs] = vo_ref[:, hs].astype(BF16)
        vw_ref[:, (2 * hh + 1) * HEAD_DIM:(2 * hh + 2) * HEAD_DIM] = jnp.ones((past + group, HEAD_DIM), BF16)
    scale = HEAD_DIM ** -0.5
    period = z_ref.shape[-1]
    band = band_ref[...]
    bias = [pltpu.roll(jnp.broadcast_to(z_ref[hh], (sub, period)), 0, 1, stride=1, stride_axis=0)[:, :span] + band
            for hh in range(hps)]
    steps = [(hh, s) for hh in range(hps) for s in range(n_sub)]
    scores = []
    for hh, s in steps:
        hs = slice(hh * HEAD_DIM, (hh + 1) * HEAD_DIM)
        q = (q_ref[s * sub:(s + 1) * sub, hs].astype(F32) * scale).astype(BF16)
        sc = _dot_nt(q, kw_ref[s * sub:s * sub + span, hs]) + bias[hh]
        scores.append(sc + first_refs[s][...] if s < n_first else sc)
    probs = [jnp.exp(sc - jnp.max(sc, axis=-1, keepdims=True)).astype(BF16) for sc in scores]
    for (hh, s), p in zip(steps, probs):
        ov = _dot(p, vw_ref[s * sub:s * sub + span, 2 * hh * HEAD_DIM:(2 * hh + 2) * HEAD_DIM])
        o_ref[s * sub:(s + 1) * sub, hh * HEAD_DIM:(hh + 1) * HEAD_DIM] = (
            ov[:, 0:HEAD_DIM] / ov[:, HEAD_DIM:2 * HEAD_DIM]).astype(o_ref.dtype)


def _n_first_group_variants(sub, past, pos0, group):
    return min(group // sub, max(0, -(-(past - pos0) // sub)))


def _attention_tables(rel_bias, sub, chunk, past, pos0, group):
    width = past + sub
    m = np.arange(width + sub - 1)
    idx = np.clip(past + (sub - 1) - m, -REL_CLIP, REL_CLIP) + REL_CLIP
    n_hi = int(np.sum(idx == idx[0]))
    n_lo = int(np.sum(idx == idx[-1]))
    mid = idx[n_hi:idx.size - n_lo]
    assert mid.size and np.all(np.diff(mid) == -1)
    v = jnp.concatenate([jnp.broadcast_to(rel_bias[:, idx[0]:idx[0] + 1], (N_HEADS, n_hi)),
                         jnp.flip(rel_bias[:, int(mid[-1]):int(mid[0]) + 1], axis=1),
                         jnp.broadcast_to(rel_bias[:, idx[-1]:idx[-1] + 1], (N_HEADS, n_lo))], axis=1)
    period = -(-(width + sub) // LANES) * LANES
    z = jnp.concatenate([v[:, sub - 1:], jnp.zeros((N_HEADS, period - v.shape[1]), F32), v[:, :sub - 1]], axis=1)
    i = np.arange(sub)[:, None]
    j = np.arange(width)[None, :]
    band = (j >= (i // chunk) * chunk) & (j < (i // chunk) * chunk + past + chunk)
    first = [np.ones_like(band)]
    for s in range(_n_first_group_variants(sub, past, pos0, group)):
        first.append(np.broadcast_to(j + (pos0 - past + s * sub) >= 0, band.shape))
    as_mask = lambda ok: jnp.asarray(np.where(ok, 0.0, -np.inf).astype(np.float32))
    return z[:, None, :], as_mask(band), as_mask(np.stack(first))


def attention(q_arr, q_map, kp_arr, kp_lead, kp_map, ko_arr, ko_map, vp_arr, vp_map, vo_arr, vo_map, tables,
              n_groups, group, sub, past, hps, past_by_row=False):
    z, band, first = tables
    rows = n_groups * group
    n_sub = group // sub
    n_first = first.shape[0] - 1
    kernel = functools.partial(_attention_kernel, past=past, group=group, sub=sub, hps=hps, past_by_row=past_by_row)
    past_block = kp_lead + ((past * hps, HEAD_DIM) if past_by_row else (past, hps * HEAD_DIM))
    width = past + sub
    lanes = hps * HEAD_DIM

    def first_spec(s):
        return pl.BlockSpec((None, sub, width), lambda h, g: (jnp.where(g == 0, 1 + s, 0), 0, 0))

    return pl.pallas_call(
        kernel,
        grid=(N_HEADS // hps, n_groups),
        in_specs=[
            pl.BlockSpec((group, lanes), q_map),
            pl.BlockSpec(past_block, kp_map),
            pl.BlockSpec((group, lanes), ko_map),
            pl.BlockSpec(past_block, vp_map),
            pl.BlockSpec((group, lanes), vo_map),
            pl.BlockSpec((hps, 1, z.shape[-1]), lambda h, g: (h, 0, 0)),
            pl.BlockSpec((sub, width), lambda h, g: (0, 0)),
        ] + [first_spec(s) for s in range(n_first)],
        out_specs=pl.BlockSpec((group, lanes), lambda h, g: (g, h)),
        out_shape=jax.ShapeDtypeStruct((rows, W_MIX), BF16),
        scratch_shapes=[pltpu.VMEM((past + group, lanes), BF16),
                        pltpu.VMEM((past + group, 2 * lanes), BF16)],
        compiler_params=_params("parallel", "arbitrary"),
        name="attention",
    )(q_arr, kp_arr, ko_arr, vp_arr, vo_arr, z, band, *([first] * n_first))


def _retention_kernel(q_ref, k_ref, v_ref, g_ref, cos_ref, sin_ref, di_ref, dq_ref, dk_ref, ds_ref,
                      nw_ref, s0_ref, y_ref, sout_ref, s_ref, *, chunk, cps):
    t = pl.program_id(1)

    @pl.when(t == 0)
    def _():
        s_ref[...] = s0_ref[...]

    nw = nw_ref[...]
    qscale = HEAD_DIM ** -0.5

    heads = [slice(h * HEAD_DIM, (h + 1) * HEAD_DIM) for h in range(N_HEADS)]
    chunks = [slice(c * chunk, (c + 1) * chunk) for c in range(cps)]
    qs, ks, qds, kds = {}, {}, {}, {}
    for c, rc in enumerate(chunks):
        cosf = cos_ref[rc, :]
        sinf = sin_ref[rc, :]
        for h, hs in enumerate(heads):
            q = q_ref[rc, hs].astype(F32)
            k = k_ref[rc, hs].astype(F32)
            q = (q * cosf + pltpu.roll(q, HEAD_DIM // 2, 1) * sinf) * qscale
            k = k * cosf + pltpu.roll(k, HEAD_DIM // 2, 1) * sinf
            qs[c, h] = q.astype(BF16)
            ks[c, h] = k.astype(BF16)
            qds[c, h] = (q * dq_ref[h]).astype(BF16)
            kds[c, h] = (k * dk_ref[h]).astype(BF16)
    amats = {ch: (_dot_nt(qs[ch], ks[ch]) * di_ref[ch[1]]).astype(BF16) for ch in qs}
    states = [s_ref[h] for h in range(N_HEADS)]
    outs = {}
    for c, rc in enumerate(chunks):
        for h, hs in enumerate(heads):
            outs[c, h] = _dot(amats[c, h], v_ref[rc, hs]) + _dot(qds[c, h], states[h].astype(BF16))
        for h, hs in enumerate(heads):
            states[h] = states[h] * ds_ref[h] + _dot_tn(kds[c, h], v_ref[rc, hs])
    for h in range(N_HEADS):
        s_ref[h] = states[h]
    for c, rc in enumerate(chunks):
        for h, hs in enumerate(heads):
            o = outs[c, h]
            mu = jnp.mean(o, axis=-1, keepdims=True)
            oc = o - mu
            var = jnp.mean(oc * oc, axis=-1, keepdims=True)
            on = oc * lax.rsqrt(var + EPS) * nw
            y_ref[rc, hs] = (_silu(g_ref[rc, hs].astype(F32)) * on).astype(y_ref.dtype)

    @pl.when(t == pl.num_programs(1) - 1)
    def _():
        sout_ref[...] = s_ref[...]


def _retention_tables(chunk, pos):
    log_gamma = jnp.log1p(-(2.0 ** (-5.0 - jnp.arange(N_HEADS, dtype=F32))))
    idx = jnp.arange(chunk, dtype=F32)
    diff = idx[:, None] - idx[None, :]
    d_intra = jnp.where(diff[None] >= 0, jnp.exp(jnp.maximum(diff, 0.0)[None] * log_gamma[:, None, None]), 0.0)
    d_q = jnp.exp((idx + 1.0)[None, :] * log_gamma[:, None])
    d_k = jnp.exp((chunk - 1.0 - idx)[None, :] * log_gamma[:, None])
    d_s = jnp.exp(chunk * log_gamma)
    d_q = jnp.broadcast_to(d_q[:, :, None], (N_HEADS, chunk, HEAD_DIM))
    d_k = jnp.broadcast_to(d_k[:, :, None], (N_HEADS, chunk, HEAD_DIM))
    d_s = jnp.broadcast_to(d_s[:, None, None], (N_HEADS, HEAD_DIM, HEAD_DIM))
    inv = ROPE_BASE ** (-jnp.arange(0, HEAD_DIM, 2, dtype=F32) / HEAD_DIM)
    ang = pos.astype(F32)[:, None] * inv[None, :]
    cos, sin = jnp.cos(ang), jnp.sin(ang)
    cosf = jnp.concatenate([cos, cos], axis=-1)
    sinf = jnp.concatenate([-sin, sin], axis=-1)
    return d_intra, d_q, d_k, d_s, cosf, sinf


def _state_spec(layer):
    return pl.BlockSpec((None, None, N_HEADS, HEAD_DIM, HEAD_DIM), lambda b, t: (layer, b, 0, 0, 0))


def retention(proj, norm_w, s0, s0_layer, batch, seq, chunk, cps, pos0):
    n_steps = seq // (chunk * cps)
    rows = chunk * cps
    d_intra, d_q, d_k, d_s, cosf, sinf = _retention_tables(chunk, pos0 + jnp.arange(seq))

    def col(c):
        return pl.BlockSpec((rows, W_MIX), lambda b, t: (b * n_steps + t, c))

    def const(shape):
        return pl.BlockSpec(shape, lambda b, t: (0,) * len(shape))

    y, s_out = pl.pallas_call(
        functools.partial(_retention_kernel, chunk=chunk, cps=cps),
        grid=(batch, n_steps),
        in_specs=[
            col(COL_QR), col(COL_KR), col(COL_VR), col(COL_GR),
            pl.BlockSpec((rows, HEAD_DIM), lambda b, t: (t, 0)),
            pl.BlockSpec((rows, HEAD_DIM), lambda b, t: (t, 0)),
            const((N_HEADS, chunk, chunk)),
            const((N_HEADS, chunk, HEAD_DIM)),
            const((N_HEADS, chunk, HEAD_DIM)),
            const((N_HEADS, HEAD_DIM, HEAD_DIM)),
            const((1, HEAD_DIM)),
            _state_spec(s0_layer),
        ],
        out_specs=[
            pl.BlockSpec((rows, W_MIX), lambda b, t: (b * n_steps + t, 0)),
            pl.BlockSpec((None, N_HEADS, HEAD_DIM, HEAD_DIM), lambda b, t: (b, 0, 0, 0)),
        ],
        out_shape=[
            jax.ShapeDtypeStruct((batch * seq, W_MIX), BF16),
            jax.ShapeDtypeStruct((batch, N_HEADS, HEAD_DIM, HEAD_DIM), F32),
        ],
        scratch_shapes=[pltpu.VMEM((N_HEADS, HEAD_DIM, HEAD_DIM), F32)],
        compiler_params=_params("parallel", "arbitrary"),
        name="retention",
    )(proj, proj, proj, proj, cosf, sinf, d_intra, d_q, d_k, d_s, norm_w.reshape(1, HEAD_DIM), s0)
    return y, s_out


def _hgrn_level_masks(chunk):
    levels = int(math.log2(chunk))
    n = np.arange(chunk)
    masks = []
    for lv in range(levels):
        s = chunk >> (lv + 1)
        pair = n // (2 * s)
        qside = (n // s) % 2 == 1
        masks.append((qside[:, None] & ~qside[None, :] & (pair[:, None] == pair[None, :])).astype(np.float32))
    masks.append(np.eye(chunk, dtype=np.float32))
    return np.tril(np.ones((chunk, chunk), np.float32)), np.stack(masks, axis=0), levels


def _hgrn_kernel(q_ref, f_ref, i_ref, g_ref, loglb_ref, log1mlb_ref, tril_ref, mask_ref, nw_ref, s0_ref,
                 y_ref, sout_ref, st_ref, e_ref, x_ref, *, chunk, levels, cps):
    t = pl.program_id(1)

    @pl.when(t == 0)
    def _():
        for h in range(N_HEADS):
            st_ref[h] = s0_ref[h].T

    fg = f_ref[...]
    log_sig = jnp.minimum(fg, 0.0) - jnp.log(1.0 + jnp.exp(-jnp.abs(fg)))
    a = loglb_ref[...]
    b = log1mlb_ref[...] + log_sig
    logf = jnp.maximum(a, b) + jnp.log(1.0 + jnp.exp(-jnp.abs(a - b)))
    f = jnp.exp(logf)
    kk = 1.0 - f
    hi = logf.astype(BF16)
    r1 = logf - hi.astype(F32)
    mid = r1.astype(BF16)
    lo = (r1 - mid.astype(F32)).astype(BF16)
    tril = tril_ref[...]
    chunks = [slice(c * chunk, (c + 1) * chunk) for c in range(cps)]
    for rc in chunks:
        e_ref[rc, :] = _dot(tril, hi[rc]) + _dot(tril, mid[rc]) + _dot(tril, lo[rc])
    bcum = e_ref[...]

    def bcast(r, n_rows):
        return jnp.broadcast_to(e_ref[r:r + 1, :], (n_rows, W_MIX))

    def boundary_rows(s):
        if 2 * s >= SUBLANES:
            return jnp.concatenate([bcast(p * 2 * s + s - 1, 2 * s) for p in range(cps * chunk // (2 * s))], axis=0)
        sub = lax.broadcasted_iota(jnp.int32, (SUBLANES, W_MIX), 0)
        groups = []
        for g in range(cps * chunk // SUBLANES):
            acc = bcast(g * SUBLANES + s - 1, SUBLANES)
            for p in range(1, SUBLANES // (2 * s)):
                acc = jnp.where(sub >= p * 2 * s, bcast(g * SUBLANES + p * 2 * s + s - 1, SUBLANES), acc)
            groups.append(acc)
        return jnp.concatenate(groups, axis=0)

    q = q_ref[...].astype(F32)
    row = lax.broadcasted_iota(jnp.int32, (cps * chunk, W_MIX), 0)
    x_ref[levels] = q.astype(BF16)
    x_ref[levels + 1] = kk.astype(BF16)
    for lv in range(levels):
        s = chunk >> (lv + 1)
        qside = ((row // s) % 2) == 1
        if s == 1:
            x_ref[lv] = jnp.where(qside, q * f, kk).astype(BF16)
        else:
            d = bcum - boundary_rows(s)
            x_ref[lv] = (jnp.where(qside, q, kk) * jnp.exp(jnp.where(qside, d, -d))).astype(BF16)
    b_last = jnp.concatenate([bcast(rc.stop - 1, chunk) for rc in chunks], axis=0)
    x_ref[levels + 2] = (q * jnp.exp(bcum)).astype(BF16)
    x_ref[levels + 3] = (kk * jnp.exp(b_last - bcum)).astype(BF16)
    decays = [jnp.exp(e_ref[rc.stop - 1:rc.stop, :]) for rc in chunks]
    nw = nw_ref[...]

    heads = [slice(h * HEAD_DIM, (h + 1) * HEAD_DIM) for h in range(N_HEADS)]
    amats = {}
    for c, rc in enumerate(chunks):
        for h, hs in enumerate(heads):
            amat = mask_ref[levels] * _dot_nt(x_ref[levels, rc, hs], x_ref[levels + 1, rc, hs])
            for lv in range(levels):
                x = x_ref[lv, rc, hs]
                amat = amat + mask_ref[lv] * _dot_nt(x, x)
            amats[c, h] = amat.astype(BF16)
    states = [st_ref[h] for h in range(N_HEADS)]
    outs = {}
    for c, rc in enumerate(chunks):
        for h, hs in enumerate(heads):
            outs[c, h] = (_dot(amats[c, h], i_ref[rc, hs])
                          + _dot_nt(x_ref[levels + 2, rc, hs], states[h].astype(BF16)))
        for h, hs in enumerate(heads):
            states[h] = states[h] * decays[c][:, hs] + _dot_tn(i_ref[rc, hs], x_ref[levels + 3, rc, hs])
    for h in range(N_HEADS):
        st_ref[h] = states[h]
    for c, rc in enumerate(chunks):
        for h, hs in enumerate(heads):
            y_ref[rc, hs] = (_silu(g_ref[rc, hs].astype(F32)) * _rms(outs[c, h], nw)).astype(y_ref.dtype)

    @pl.when(t == pl.num_programs(1) - 1)
    def _():
        for h in range(N_HEADS):
            sout_ref[h] = st_ref[h].T


def hgrn2(proj, fg, log_lb, log1m_lb, norm_w, s0, s0_layer, batch, seq, chunk, cps):
    n_steps = seq // (chunk * cps)
    rows = chunk * cps
    tril_np, mask_np, levels = _hgrn_level_masks(chunk)
    tril = jnp.asarray(tril_np, dtype=BF16)
    masks = jnp.asarray(mask_np)

    def col(c):
        return pl.BlockSpec((rows, W_MIX), lambda b, t: (b * n_steps + t, c))

    def const(shape):
        return pl.BlockSpec(shape, lambda b, t: (0,) * len(shape))

    kernel = functools.partial(_hgrn_kernel, chunk=chunk, levels=levels, cps=cps)
    y, s_out = pl.pallas_call(
        kernel,
        grid=(batch, n_steps),
        in_specs=[
            col(COL_QG), col(0), col(COL_IG), col(COL_GG),
            const((1, W_MIX)), const((1, W_MIX)),
            const((chunk, chunk)),
            const((levels + 1, chunk, chunk)),
            const((1, HEAD_DIM)),
            _state_spec(s0_layer),
        ],
        out_specs=[
            pl.BlockSpec((rows, W_MIX), lambda b, t: (b * n_steps + t, 0)),
            pl.BlockSpec((None, N_HEADS, HEAD_DIM, HEAD_DIM), lambda b, t: (b, 0, 0, 0)),
        ],
        out_shape=[
            jax.ShapeDtypeStruct((batch * seq, W_MIX), BF16),
            jax.ShapeDtypeStruct((batch, N_HEADS, HEAD_DIM, HEAD_DIM), F32),
        ],
        scratch_shapes=[
            pltpu.VMEM((N_HEADS, HEAD_DIM, HEAD_DIM), F32),
            pltpu.VMEM((rows, W_MIX), F32),
            pltpu.VMEM((levels + 4, rows, W_MIX), BF16),
        ],
        compiler_params=_params("parallel", "arbitrary"),
        name="hgrn2",
    )(proj, fg, proj, proj, log_lb, log1m_lb, tril, masks, norm_w.reshape(1, HEAD_DIM), s0)
    return y, s_out


def _merge_kernel(x_ref, gpre_ref, ya_ref, yr_ref, yg_ref, wg0_ref, wg1_ref, wg2_ref, wb_ref, wo_ref,
                  gpost_ref, o_ref, h_ref, *, tm, rs):
    j = pl.program_id(1)

    @pl.when(j == 0)
    def _():
        for rows in _row_blocks(tm):
            h_ref[rows, :] = _rms(x_ref[rows, :], gpre_ref[...]).astype(BF16)
        o_ref[...] = jnp.zeros_like(o_ref)

    for r in range(tm // rs):
        rows = slice(r * rs, (r + 1) * rs)
        h = h_ref[rows, :]
        merged = None
        for n, (y_ref, wg_ref) in enumerate(((ya_ref, wg0_ref), (yr_ref, wg1_ref), (yg_ref, wg2_ref))):
            gate = jax.nn.sigmoid(_dot(h, wg_ref[...]))
            term = gate * _dot(y_ref[rows, :], wb_ref[n])
            merged = term if merged is None else merged + term
        o_ref[rows, :] += _dot(merged.astype(BF16), wo_ref[...])

    @pl.when(j == pl.num_programs(1) - 1)
    def _():
        for rows in _row_blocks(tm):
            o_ref[rows, :] = x_ref[rows, :] + _rms(o_ref[rows, :], gpost_ref[...])


def merge(x, g_pre, ya, yr, yg, w_gate, w_branch, w_out, g_post, layer, tm, tn):
    rows, d = x.shape
    nj = d // tn

    def gate_spec(n):
        return pl.BlockSpec((None, d, tn), lambda i, j: (layer, 0, n * nj + j))

    kernel = functools.partial(_merge_kernel, tm=tm, rs=min(tm, ROW_SUB))
    single = pl.Buffered(1)
    return pl.pallas_call(
        kernel,
        grid=(rows // tm, nj),
        in_specs=[
            pl.BlockSpec((tm, d), lambda i, j: (i, 0)),
            pl.BlockSpec((1, d), lambda i, j: (0, 0)),
            pl.BlockSpec((tm, W_MIX), lambda i, j: (i, 0), pipeline_mode=single),
            pl.BlockSpec((tm, W_MIX), lambda i, j: (i, 0), pipeline_mode=single),
            pl.BlockSpec((tm, W_MIX), lambda i, j: (i, 0), pipeline_mode=single),
            gate_spec(0), gate_spec(1), gate_spec(2),
            pl.BlockSpec((None, N_BRANCH, W_MIX, tn), lambda i, j: (layer, 0, 0, j)),
            pl.BlockSpec((None, tn, d), lambda i, j: (layer, j, 0)),
            pl.BlockSpec((1, d), lambda i, j: (0, 0)),
        ],
        out_specs=pl.BlockSpec((tm, d), lambda i, j: (i, 0)),
        out_shape=jax.ShapeDtypeStruct((rows, d), F32),
        scratch_shapes=[pltpu.VMEM((tm, d), BF16)],
        compiler_params=_params("parallel", "arbitrary"),
        name="merge",
    )(x, g_pre, ya, yr, yg, w_gate, w_gate, w_gate, w_branch, w_out, g_post)


HALO = BF16_ROWS


def _ffn_kernel(x_ref, xprev_ref, cst_ref, gpre_ref, wup_ref, wval_ref, cw_ref, cb_ref, wdn_ref, gpost_ref,
                o_ref, uplast_ref, h_ref, up_ref, *, tm, seq):
    i = pl.program_id(0)
    j = pl.program_id(1)
    seqs_per_tile = max(tm // seq, 1)
    tiles_per_seq = max(seq // tm, 1)

    @pl.when(j == 0)
    def _():
        g = gpre_ref[...]
        h_ref[0:HALO, :] = _rms(xprev_ref[...], g).astype(BF16)
        for rows in _row_blocks(tm):
            h_ref[HALO + rows.start:HALO + rows.stop, :] = _rms(x_ref[rows, :], g).astype(BF16)
        o_ref[...] = jnp.zeros_like(o_ref)

    cw = cw_ref[...]
    wdn = wdn_ref[...].astype(BF16)
    rs = min(tm, ROW_SUB)
    n_sub = tm // rs

    def up_block(r):
        lo = 0 if r == 0 else HALO + r * rs
        hi = HALO + (r + 1) * rs
        up_ref[lo:hi, :] = _dot(h_ref[lo:hi, :], wup_ref[...])

    up_block(0)
    if seqs_per_tile == 1:
        first = i % tiles_per_seq == 0
        halo_rows = slice(HALO - (CONV_W - 1), HALO)
        up_ref[halo_rows, :] = jnp.where(first, cst_ref[0], up_ref[halo_rows, :])
    for r in range(n_sub):
        if r + 1 < n_sub:
            up_block(r + 1)
        r0 = HALO + r * rs
        tap2 = up_ref[r0 - 2:r0 - 2 + rs, :]
        tap1 = up_ref[r0 - 1:r0 - 1 + rs, :]
        tap0 = up_ref[r0:r0 + rs, :]
        if seqs_per_tile > 1:
            tf = tap0.shape[1]
            st = cst_ref[...]
            st0 = jnp.broadcast_to(st[:, 0:1, :], (seqs_per_tile, seq, tf)).reshape(tm, tf)
            st1 = jnp.broadcast_to(st[:, 1:2, :], (seqs_per_tile, seq, tf)).reshape(tm, tf)
            pos = lax.broadcasted_iota(jnp.int32, (tm, tf), 0) % seq
            tap1 = jnp.where(pos == 0, st1, tap1)
            tap2 = jnp.where(pos == 0, st0, jnp.where(pos == 1, st1, tap2))
        conv = cb_ref[...] + tap2 * cw[0:1, :] + tap1 * cw[1:2, :] + tap0 * cw[2:3, :]
        val = _dot(h_ref[r0:r0 + rs, :], wval_ref[...])
        act = 0.5 * conv * (1.0 + lax.erf(conv * math.sqrt(0.5))) * val
        o_ref[r * rs:(r + 1) * rs, :] += _dot(act.astype(BF16), wdn)

    keep = uplast_ref.shape[0]
    uplast_ref[...] = up_ref[HALO + tm - keep:HALO + tm, :]

    @pl.when(j == pl.num_programs(1) - 1)
    def _():
        for rows in _row_blocks(tm):
            o_ref[rows, :] = x_ref[rows, :] + _rms(o_ref[rows, :], gpost_ref[...])


def ffn(x, conv_state, cs_layer, g_pre, w_up, w_val, conv_w, conv_b, w_down, g_post, layer, batch, seq, tm, tf):
    rows, d = x.shape
    dff = w_up.shape[2]
    n_tiles = rows // tm
    seqs_per_tile = max(tm // seq, 1)
    tiles_per_seq = max(seq // tm, 1)
    assert rows == batch * seq and (tm % seq == 0 or seq % tm == 0)
    keep = tm if seqs_per_tile > 1 else SUBLANES
    halo_blocks = tm // HALO
    kernel = functools.partial(_ffn_kernel, tm=tm, seq=seq)
    out, uplast = pl.pallas_call(
        kernel,
        grid=(n_tiles, dff // tf),
        in_specs=[
            pl.BlockSpec((tm, d), lambda i, j: (i, 0)),
            pl.BlockSpec((HALO, d), lambda i, j: (jnp.maximum(i * halo_blocks - 1, 0), 0)),
            pl.BlockSpec((None, seqs_per_tile, CONV_W - 1, tf),
                         lambda i, j: (cs_layer, i * seqs_per_tile // tiles_per_seq, 0, j)),
            pl.BlockSpec((1, d), lambda i, j: (0, 0)),
            pl.BlockSpec((None, d, tf), lambda i, j: (layer, 0, j)),
            pl.BlockSpec((None, d, tf), lambda i, j: (layer, 0, j)),
            pl.BlockSpec((CONV_W, tf), lambda i, j: (0, j)),
            pl.BlockSpec((1, tf), lambda i, j: (0, j)),
            pl.BlockSpec((None, tf, d), lambda i, j: (layer, j, 0)),
            pl.BlockSpec((1, d), lambda i, j: (0, 0)),
        ],
        out_specs=[
            pl.BlockSpec((tm, d), lambda i, j: (i, 0)),
            pl.BlockSpec((keep, tf), lambda i, j: (i, j)),
        ],
        out_shape=[
            jax.ShapeDtypeStruct((rows, d), F32),
            jax.ShapeDtypeStruct((n_tiles * keep, dff), F32),
        ],
        scratch_shapes=[
            pltpu.VMEM((HALO + tm, d), BF16),
            pltpu.VMEM((HALO + tm, tf), F32),
        ],
        compiler_params=_params("parallel", "arbitrary"),
        name="ffn",
    )(x, x, conv_state, g_pre, w_up, w_val, conv_w, conv_b, w_down, g_post)
    new_conv = uplast.reshape(batch, -1, dff)[:, -(CONV_W - 1):, :]
    return out, new_conv


def _layer(x, batch, seq, pos0, layer, caches, cache_layer, vecs, mats, tiles):
    k_past, v_past, ret_s0, hgrn_s0, conv_s0 = caches
    (g_mix_pre, g_mix_post, g_ffn_pre, g_ffn_post, rel_bias, ret_norm_w, log_lb, log1m_lb, hgrn_norm_w,
     conv_w, conv_b) = vecs
    w_in, w_gate, w_branch, w_out, w_up, w_val, w_down = mats
    rows = batch * seq
    tn = W_MIX
    f32_cols = (COL_FG,) if k_past is None else (COL_FG, COL_KA, COL_VA)
    proj, fg, *kv_f32 = norm_proj(x, g_mix_pre, w_in, layer, tiles["tm_proj"], tn, f32_cols=f32_cols)

    chunk = min(CHUNK, seq)
    past = ATT_PAST_ROWS
    hps = tiles["att_heads"]
    hb = N_HEADS // hps
    if k_past is None:
        keep = min(past, seq)
        (kv_new,) = norm_proj(x, g_mix_pre, w_in, layer, keep, tn, row_block0=seq // keep - 1, n_row_blocks=1,
                              col_block0=COL_KA, n_col_blocks=2, out_dtype=F32)
        group, sub = tiles["att_group"], tiles["att_sub"]
        gp = group // past
        tables = _attention_tables(rel_bias, sub, chunk, past, pos0, group)
        y_a = attention(
            proj, lambda h, g: (g, h),
            proj, (), lambda h, g: (jnp.maximum(g * gp - 1, 0), hb + h),
            proj, lambda h, g: (g, hb + h),
            proj, lambda h, g: (jnp.maximum(g * gp - 1, 0), 2 * hb + h),
            proj, lambda h, g: (g, 2 * hb + h),
            tables, seq // group, group, sub, past, hps)
    else:
        keep = seq
        kv_new = jnp.concatenate(kv_f32, axis=1)
        group = sub = seq
        tables = _attention_tables(rel_bias, sub, chunk, past, pos0, group)
        n_l = k_past.shape[0]
        assert hps == N_HEADS
        kp = k_past.reshape(n_l, batch, past * N_HEADS, HEAD_DIM)
        vp = v_past.reshape(n_l, batch, past * N_HEADS, HEAD_DIM)
        y_a = attention(
            proj, lambda h, g: (g, h),
            kp, (None, None), lambda h, g: (cache_layer, g, 0, 0),
            proj, lambda h, g: (g, hb + h),
            vp, lambda h, g: (cache_layer, g, 0, 0),
            proj, lambda h, g: (g, 2 * hb + h),
            tables, batch, group, sub, past, hps, past_by_row=True)

    y_r, ret_s = retention(proj, ret_norm_w, ret_s0, cache_layer, batch, seq, tiles["ret_chunk"], tiles["ret_cps"], pos0)
    y_g, hgrn_s = hgrn2(proj, fg, log_lb, log1m_lb, hgrn_norm_w, hgrn_s0, cache_layer, batch, seq, chunk,
                          tiles["hgrn_cps"])

    x = merge(x, g_mix_pre, y_a, y_r, y_g, w_gate, w_branch, w_out, g_mix_post, layer,
              tiles["tm_merge"], tiles["tn_merge"])
    x, new_conv = ffn(x, conv_s0, cache_layer, g_ffn_pre, w_up, w_val, conv_w, conv_b, w_down, g_ffn_post, layer,
                      batch, seq, tiles["tm_ffn"], tiles["tf_ffn"])
    k_new = kv_new[:, :W_MIX].reshape(batch, keep, N_HEADS, HEAD_DIM)
    v_new = kv_new[:, W_MIX:].reshape(batch, keep, N_HEADS, HEAD_DIM)
    return x, k_new, v_new, ret_s, hgrn_s, new_conv


PROMPT_TILES = dict(tm_proj=1024, att_group=1024, att_sub=256, att_heads=4, ret_chunk=256, ret_cps=2, hgrn_cps=8, tm_merge=1024, tn_merge=256, tm_ffn=1024, tf_ffn=512)


def _sample_tiles(rows, seq):
    return dict(tm_proj=rows, att_heads=N_HEADS, ret_chunk=seq, ret_cps=1, hgrn_cps=1, tm_merge=rows, tn_merge=256, tm_ffn=rows, tf_ffn=512)


def kernel(x_prompt, x_sample, cache_attn_k, cache_attn_v, state_ret, state_hgrn, state_ffn_conv,
           norm_mix_pre, norm_mix_post, norm_ffn_pre, norm_ffn_post, w_in, attn_rel_bias, ret_norm_w,
           hgrn_lower_bound, hgrn_norm_w, w_gate, w_branch, w_out, ffn_w_up, ffn_w_val, ffn_conv_w,
           ffn_conv_b, ffn_w_down):
    depth = w_in.shape[0]
    bp, tp, d = x_prompt.shape
    bs, ts, _ = x_sample.shape

    lb_cum = jnp.cumsum(jax.nn.softmax(hgrn_lower_bound.astype(F32), axis=0), axis=0)
    lb_layers = lb_cum - lb_cum[0:1]
    log_lb = jnp.log(lb_layers)
    log1m_lb = jnp.log1p(-lb_layers)

    bf = lambda w: w.astype(BF16)
    mats = (w_in, bf(w_gate), bf(w_branch), bf(w_out), bf(ffn_w_up), bf(ffn_w_val), ffn_w_down)

    zs = jnp.zeros((1, bp, N_HEADS, HEAD_DIM, HEAD_DIM), F32)
    zc = jnp.zeros((1, bp, CONV_W - 1, D_FF), F32)
    caches_p = (None, None, zs, zs, zc)
    caches_s = (cache_attn_k, cache_attn_v, state_ret, state_hgrn, state_ffn_conv)

    xp = x_prompt.reshape(bp * tp, d)
    xs = x_sample.reshape(bs * ts, d)
    outs_p = [[] for _ in range(5)]
    outs_s = [[] for _ in range(5)]
    for l in range(depth):
        vecs = (norm_mix_pre[l][None], norm_mix_post[l][None], norm_ffn_pre[l][None], norm_ffn_post[l][None],
                attn_rel_bias[l], ret_norm_w[l], log_lb[l][None], log1m_lb[l][None], hgrn_norm_w[l],
                ffn_conv_w[l], ffn_conv_b[l][None])
        res_p = _layer(xp, bp, tp, 0, l, caches_p, 0, vecs, mats, PROMPT_TILES)
        res_s = _layer(xs, bs, ts, PAST_LEN, l, caches_s, l, vecs, mats, _sample_tiles(bs * ts, ts))
        xp, xs = res_p[0], res_s[0]
        for lst, val in zip(outs_p, res_p[1:]):
            lst.append(val)
        for lst, val in zip(outs_s, res_s[1:]):
            lst.append(val)

    return (xp.reshape(bp, tp, d), xs.reshape(bs, ts, d),
            *(jnp.stack(o) for o in outs_p), *(jnp.stack(o) for o in outs_s))
```

```python
import functools
import math

import numpy as np
import jax
import jax.numpy as jnp
from jax import lax
from jax.experimental import pallas as pl
from jax.experimental.pallas import tpu as pltpu

F32 = jnp.float32
BF16 = jnp.bfloat16

D_MODEL = 2048
CHUNK = 64
BAND_CHUNKS = 8
ATT_PAST_ROWS = BAND_CHUNKS * CHUNK
W_MIX = D_MODEL // 2
N_HEADS = 8
HEAD_DIM = W_MIX // N_HEADS
REL_CLIP = 128
N_BRANCH = 3
N_IN = 11 * W_MIX
D_FF = 5632
CONV_W = 3
ROPE_BASE = 10000.0
EPS = 1e-6
PAST_LEN = 1024

COL_QA, COL_KA, COL_VA, COL_QR, COL_KR, COL_VR, COL_GR, COL_QG, COL_FG, COL_IG, COL_GG = range(11)

V7X_VMEM_BYTES = 64 * 1024 * 1024
VMEM_LIMIT = V7X_VMEM_BYTES - 4 * 1024 * 1024
SUBLANES = 8
LANES = 128
BF16_ROWS = 16
ROW_SUB = 512
PROLOGUE_ROWS = 256

NT_DIMS = (((1,), (1,)), ((), ()))
TN_DIMS = (((0,), (0,)), ((), ()))


def _params(*sem):
    return pltpu.CompilerParams(dimension_semantics=sem, vmem_limit_bytes=VMEM_LIMIT)


def _rms(x, g):
    return x * lax.rsqrt(jnp.mean(x * x, axis=-1, keepdims=True) + EPS) * g


def _dot(a, b):
    return jnp.dot(a, b, preferred_element_type=F32)


def _dot_nt(a, b):
    return lax.dot_general(a, b, NT_DIMS, preferred_element_type=F32)


def _dot_tn(a, b):
    return lax.dot_general(a, b, TN_DIMS, preferred_element_type=F32)


def _silu(x):
    return x * jax.nn.sigmoid(x)


def _row_blocks(n):
    step = min(n, PROLOGUE_ROWS)
    return [slice(r, r + step) for r in range(0, n, step)]


def _norm_proj_kernel(x_ref, g_ref, w_ref, o_ref, *rest, f32_cols):
    f32_refs, h_ref = rest[:-1], rest[-1]
    j = pl.program_id(1)

    @pl.when(j == 0)
    def _():
        for rows in _row_blocks(x_ref.shape[0]):
            h_ref[rows, :] = _rms(x_ref[rows, :], g_ref[...]).astype(BF16)

    acc = _dot(h_ref[...], w_ref[...].astype(BF16))
    o_ref[...] = acc.astype(o_ref.dtype)
    for col, ref in zip(f32_cols, f32_refs):
        @pl.when(j == col)
        def _():
            ref[...] = acc


def norm_proj(x, g, w, layer, tm, tn, f32_cols=(), row_block0=0, n_row_blocks=None, col_block0=0,
              n_col_blocks=None, out_dtype=BF16):
    d = x.shape[1]
    n_row_blocks = x.shape[0] // tm - row_block0 if n_row_blocks is None else n_row_blocks
    n_col_blocks = w.shape[2] // tn - col_block0 if n_col_blocks is None else n_col_blocks
    rows, n = n_row_blocks * tm, n_col_blocks * tn
    kernel = functools.partial(_norm_proj_kernel, f32_cols=tuple(f32_cols))
    outs = pl.pallas_call(
        kernel,
        grid=(n_row_blocks, n_col_blocks),
        in_specs=[
            pl.BlockSpec((tm, d), lambda i, j: (row_block0 + i, 0)),
            pl.BlockSpec((1, d), lambda i, j: (0, 0)),
            pl.BlockSpec((None, d, tn), lambda i, j: (layer, 0, col_block0 + j)),
        ],
        out_specs=[pl.BlockSpec((tm, tn), lambda i, j: (i, j))]
        + [pl.BlockSpec((tm, tn), lambda i, j: (i, 0)) for _ in f32_cols],
        out_shape=[jax.ShapeDtypeStruct((rows, n), out_dtype)]
        + [jax.ShapeDtypeStruct((rows, tn), F32) for _ in f32_cols],
        scratch_shapes=[pltpu.VMEM((tm, d), BF16)],
        compiler_params=_params("parallel", "arbitrary"),
        name="norm_proj",
    )(x, g, w)
    return outs


def _attention_kernel(q_ref, kp_ref, ko_ref, vp_ref, vo_ref, *rest, past, group, sub, hps, past_by_row):
    n_sub = group // sub
    n_first = len(rest) - 5
    z_ref, band_ref, first_refs, (o_ref, kw_ref, vw_ref) = rest[0], rest[1], rest[2:2 + n_first], rest[2 + n_first:]
    span = past + sub
    if not past_by_row:
        kw_ref[0:past, :] = kp_ref[...].astype(BF16)
    kw_ref[past:past + group, :] = ko_ref[...].astype(BF16)
    for hh in range(hps):
        hs = slice(hh * HEAD_DIM, (hh + 1) * HEAD_DIM)
        vs = slice(2 * hh * HEAD_DIM, (2 * hh + 1) * HEAD_DIM)
        if past_by_row:
            head_rows = pl.ds(hh, past, stride=hps)
            kw_ref[0:past, hs] = kp_ref[head_rows, :].astype(BF16)
            vw_ref[0:past, vs] = vp_ref[head_rows, :].astype(BF16)
        else:
            vw_ref[0:past, vs] = vp_ref[:, hs].astype(BF16)
        vw_ref[past:past + group, vs] = vo_ref[:, hs].astype(BF16)
        vw_ref[:, (2 * hh + 1) * HEAD_DIM:(2 * hh + 2) * HEAD_DIM] = jnp.ones((past + group, HEAD_DIM), BF16)
    scale = HEAD_DIM ** -0.5
    period = z_ref.shape[-1]
    band = band_ref[...]
    bias = [pltpu.roll(jnp.broadcast_to(z_ref[hh], (sub, period)), 0, 1, stride=1, stride_axis=0)[:, :span] + band
            for hh in range(hps)]
    steps = [(hh, s) for hh in range(hps) for s in range(n_sub)]
    scores = []
    for hh, s in steps:
        hs = slice(hh * HEAD_DIM, (hh + 1) * HEAD_DIM)
        q = (q_ref[s * sub:(s + 1) * sub, hs].astype(F32) * scale).astype(BF16)
        sc = _dot_nt(q, kw_ref[s * sub:s * sub + span, hs]) + bias[hh]
        scores.append(sc + first_refs[s][...] if s < n_first else sc)
    probs = [jnp.exp(sc - jnp.max(sc, axis=-1, keepdims=True)).astype(BF16) for sc in scores]
    for (hh, s), p in zip(steps, probs):
        ov = _dot(p, vw_ref[s * sub:s * sub + span, 2 * hh * HEAD_DIM:(2 * hh + 2) * HEAD_DIM])
        o_ref[s * sub:(s + 1) * sub, hh * HEAD_DIM:(hh + 1) * HEAD_DIM] = (
            ov[:, 0:HEAD_DIM] / ov[:, HEAD_DIM:2 * HEAD_DIM]).astype(o_ref.dtype)


def _n_first_group_variants(sub, past, pos0, group):
    return min(group // sub, max(0, -(-(past - pos0) // sub)))


def _attention_tables(rel_bias, sub, chunk, past, pos0, group):
    width = past + sub
    m = np.arange(width + sub - 1)
    idx = np.clip(past + (sub - 1) - m, -REL_CLIP, REL_CLIP) + REL_CLIP
    n_hi = int(np.sum(idx == idx[0]))
    n_lo = int(np.sum(idx == idx[-1]))
    mid = idx[n_hi:idx.size - n_lo]
    assert mid.size and np.all(np.diff(mid) == -1)
    v = jnp.concatenate([jnp.broadcast_to(rel_bias[:, idx[0]:idx[0] + 1], (N_HEADS, n_hi)),
                         jnp.flip(rel_bias[:, int(mid[-1]):int(mid[0]) + 1], axis=1),
                         jnp.broadcast_to(rel_bias[:, idx[-1]:idx[-1] + 1], (N_HEADS, n_lo))], axis=1)
    period = -(-(width + sub) // LANES) * LANES
    z = jnp.concatenate([v[:, sub - 1:], jnp.zeros((N_HEADS, period - v.shape[1]), F32), v[:, :sub - 1]], axis=1)
    i = np.arange(sub)[:, None]
    j = np.arange(width)[None, :]
    band = (j >= (i // chunk) * chunk) & (j < (i // chunk) * chunk + past + chunk)
    first = [np.ones_like(band)]
    for s in range(_n_first_group_variants(sub, past, pos0, group)):
        first.append(np.broadcast_to(j + (pos0 - past + s * sub) >= 0, band.shape))
    as_mask = lambda ok: jnp.asarray(np.where(ok, 0.0, -np.inf).astype(np.float32))
    return z[:, None, :], as_mask(band), as_mask(np.stack(first))


def attention(q_arr, q_map, kp_arr, kp_lead, kp_map, ko_arr, ko_map, vp_arr, vp_map, vo_arr, vo_map, tables,
              n_groups, group, sub, past, hps, past_by_row=False):
    z, band, first = tables
    rows = n_groups * group
    n_sub = group // sub
    n_first = first.shape[0] - 1
    kernel = functools.partial(_attention_kernel, past=past, group=group, sub=sub, hps=hps, past_by_row=past_by_row)
    past_block = kp_lead + ((past * hps, HEAD_DIM) if past_by_row else (past, hps * HEAD_DIM))
    width = past + sub
    lanes = hps * HEAD_DIM

    def first_spec(s):
        return pl.BlockSpec((None, sub, width), lambda h, g: (jnp.where(g == 0, 1 + s, 0), 0, 0))

    return pl.pallas_call(
        kernel,
        grid=(N_HEADS // hps, n_groups),
        in_specs=[
            pl.BlockSpec((group, lanes), q_map),
            pl.BlockSpec(past_block, kp_map),
            pl.BlockSpec((group, lanes), ko_map),
            pl.BlockSpec(past_block, vp_map),
            pl.BlockSpec((group, lanes), vo_map),
            pl.BlockSpec((hps, 1, z.shape[-1]), lambda h, g: (h, 0, 0)),
            pl.BlockSpec((sub, width), lambda h, g: (0, 0)),
        ] + [first_spec(s) for s in range(n_first)],
        out_specs=pl.BlockSpec((group, lanes), lambda h, g: (g, h)),
        out_shape=jax.ShapeDtypeStruct((rows, W_MIX), BF16),
        scratch_shapes=[pltpu.VMEM((past + group, lanes), BF16),
                        pltpu.VMEM((past + group, 2 * lanes), BF16)],
        compiler_params=_params("parallel", "arbitrary"),
        name="attention",
    )(q_arr, kp_arr, ko_arr, vp_arr, vo_arr, z, band, *([first] * n_first))


def _retention_kernel(q_ref, k_ref, v_ref, g_ref, cos_ref, sin_ref, di_ref, dq_ref, dk_ref, ds_ref,
                      nw_ref, s0_ref, y_ref, sout_ref, s_ref, *, chunk, cps):
    t = pl.program_id(1)

    @pl.when(t == 0)
    def _():
        s_ref[...] = s0_ref[...]

    nw = nw_ref[...]
    qscale = HEAD_DIM ** -0.5

    heads = [slice(h * HEAD_DIM, (h + 1) * HEAD_DIM) for h in range(N_HEADS)]
    chunks = [slice(c * chunk, (c + 1) * chunk) for c in range(cps)]
    qs, ks, qds, kds = {}, {}, {}, {}
    for c, rc in enumerate(chunks):
        cosf = cos_ref[rc, :]
        sinf = sin_ref[rc, :]
        for h, hs in enumerate(heads):
            q = q_ref[rc, hs].astype(F32)
            k = k_ref[rc, hs].astype(F32)
            q = (q * cosf + pltpu.roll(q, HEAD_DIM // 2, 1) * sinf) * qscale
            k = k * cosf + pltpu.roll(k, HEAD_DIM // 2, 1) * sinf
            qs[c, h] = q.astype(BF16)
            ks[c, h] = k.astype(BF16)
            qds[c, h] = (q * dq_ref[h]).astype(BF16)
            kds[c, h] = (k * dk_ref[h]).astype(BF16)
    amats = {ch: (_dot_nt(qs[ch], ks[ch]) * di_ref[ch[1]]).astype(BF16) for ch in qs}
    states = [s_ref[h] for h in range(N_HEADS)]
    outs = {}
    for c, rc in enumerate(chunks):
        for h, hs in enumerate(heads):
            outs[c, h] = _dot(amats[c, h], v_ref[rc, hs]) + _dot(qds[c, h], states[h].astype(BF16))
        for h, hs in enumerate(heads):
            states[h] = states[h] * ds_ref[h] + _dot_tn(kds[c, h], v_ref[rc, hs])
    for h in range(N_HEADS):
        s_ref[h] = states[h]
    for c, rc in enumerate(chunks):
        for h, hs in enumerate(heads):
            o = outs[c, h]
            mu = jnp.mean(o, axis=-1, keepdims=True)
            oc = o - mu
            var = jnp.mean(oc * oc, axis=-1, keepdims=True)
            on = oc * lax.rsqrt(var + EPS) * nw
            y_ref[rc, hs] = (_silu(g_ref[rc, hs].astype(F32)) * on).astype(y_ref.dtype)

    @pl.when(t == pl.num_programs(1) - 1)
    def _():
        sout_ref[...] = s_ref[...]


def _retention_tables(chunk, pos):
    log_gamma = jnp.log1p(-(2.0 ** (-5.0 - jnp.arange(N_HEADS, dtype=F32))))
    idx = jnp.arange(chunk, dtype=F32)
    diff = idx[:, None] - idx[None, :]
    d_intra = jnp.where(diff[None] >= 0, jnp.exp(jnp.maximum(diff, 0.0)[None] * log_gamma[:, None, None]), 0.0)
    d_q = jnp.exp((idx + 1.0)[None, :] * log_gamma[:, None])
    d_k = jnp.exp((chunk - 1.0 - idx)[None, :] * log_gamma[:, None])
    d_s = jnp.exp(chunk * log_gamma)
    d_q = jnp.broadcast_to(d_q[:, :, None], (N_HEADS, chunk, HEAD_DIM))
    d_k = jnp.broadcast_to(d_k[:, :, None], (N_HEADS, chunk, HEAD_DIM))
    d_s = jnp.broadcast_to(d_s[:, None, None], (N_HEADS, HEAD_DIM, HEAD_DIM))
    inv = ROPE_BASE ** (-jnp.arange(0, HEAD_DIM, 2, dtype=F32) / HEAD_DIM)
    ang = pos.astype(F32)[:, None] * inv[None, :]
    cos, sin = jnp.cos(ang), jnp.sin(ang)
    cosf = jnp.concatenate([cos, cos], axis=-1)
    sinf = jnp.concatenate([-sin, sin], axis=-1)
    return d_intra, d_q, d_k, d_s, cosf, sinf


def _state_spec(layer):
    return pl.BlockSpec((None, None, N_HEADS, HEAD_DIM, HEAD_DIM), lambda b, t: (layer, b, 0, 0, 0))


def retention(proj, norm_w, s0, s0_layer, batch, seq, chunk, cps, pos0):
    n_steps = seq // (chunk * cps)
    rows = chunk * cps
    d_intra, d_q, d_k, d_s, cosf, sinf = _retention_tables(chunk, pos0 + jnp.arange(seq))

    def col(c):
        return pl.BlockSpec((rows, W_MIX), lambda b, t: (b * n_steps + t, c))

    def const(shape):
        return pl.BlockSpec(shape, lambda b, t: (0,) * len(shape))

    y, s_out = pl.pallas_call(
        functools.partial(_retention_kernel, chunk=chunk, cps=cps),
        grid=(batch, n_steps),
        in_specs=[
            col(COL_QR), col(COL_KR), col(COL_VR), col(COL_GR),
            pl.BlockSpec((rows, HEAD_DIM), lambda b, t: (t, 0)),
            pl.BlockSpec((rows, HEAD_DIM), lambda b, t: (t, 0)),
            const((N_HEADS, chunk, chunk)),
            const((N_HEADS, chunk, HEAD_DIM)),
            const((N_HEADS, chunk, HEAD_DIM)),
            const((N_HEADS, HEAD_DIM, HEAD_DIM)),
            const((1, HEAD_DIM)),
            _state_spec(s0_layer),
        ],
        out_specs=[
            pl.BlockSpec((rows, W_MIX), lambda b, t: (b * n_steps + t, 0)),
            pl.BlockSpec((None, N_HEADS, HEAD_DIM, HEAD_DIM), lambda b, t: (b, 0, 0, 0)),
        ],
        out_shape=[
            jax.ShapeDtypeStruct((batch * seq, W_MIX), BF16),
            jax.ShapeDtypeStruct((batch, N_HEADS, HEAD_DIM, HEAD_DIM), F32),
        ],
        scratch_shapes=[pltpu.VMEM((N_HEADS, HEAD_DIM, HEAD_DIM), F32)],
        compiler_params=_params("parallel", "arbitrary"),
        name="retention",
    )(proj, proj, proj, proj, cosf, sinf, d_intra, d_q, d_k, d_s, norm_w.reshape(1, HEAD_DIM), s0)
    return y, s_out


def _hgrn_level_masks(chunk):
    levels = int(math.log2(chunk))
    n = np.arange(chunk)
    masks = []
    for lv in range(levels):
        s = chunk >> (lv + 1)
        pair = n // (2 * s)
        qside = (n // s) % 2 == 1
        masks.append((qside[:, None] & ~qside[None, :] & (pair[:, None] == pair[None, :])).astype(np.float32))
    masks.append(np.eye(chunk, dtype=np.float32))
    return np.tril(np.ones((chunk, chunk), np.float32)), np.stack(masks, axis=0), levels


def _hgrn_kernel(q_ref, f_ref, i_ref, g_ref, loglb_ref, log1mlb_ref, tril_ref, mask_ref, nw_ref, s0_ref,
                 y_ref, sout_ref, st_ref, e_ref, x_ref, *, chunk, levels, cps):
    t = pl.program_id(1)

    @pl.when(t == 0)
    def _():
        for h in range(N_HEADS):
            st_ref[h] = s0_ref[h].T

    fg = f_ref[...]
    log_sig = jnp.minimum(fg, 0.0) - jnp.log(1.0 + jnp.exp(-jnp.abs(fg)))
    a = loglb_ref[...]
    b = log1mlb_ref[...] + log_sig
    logf = jnp.maximum(a, b) + jnp.log(1.0 + jnp.exp(-jnp.abs(a - b)))
    f = jnp.exp(logf)
    kk = 1.0 - f
    hi = logf.astype(BF16)
    r1 = logf - hi.astype(F32)
    mid = r1.astype(BF16)
    lo = (r1 - mid.astype(F32)).astype(BF16)
    tril = tril_ref[...]
    chunks = [slice(c * chunk, (c + 1) * chunk) for c in range(cps)]
    for rc in chunks:
        e_ref[rc, :] = _dot(tril, hi[rc]) + _dot(tril, mid[rc]) + _dot(tril, lo[rc])
    bcum = e_ref[...]

    def bcast(r, n_rows):
        return jnp.broadcast_to(e_ref[r:r + 1, :], (n_rows, W_MIX))

    def boundary_rows(s):
        if 2 * s >= SUBLANES:
            return jnp.concatenate([bcast(p * 2 * s + s - 1, 2 * s) for p in range(cps * chunk // (2 * s))], axis=0)
        sub = lax.broadcasted_iota(jnp.int32, (SUBLANES, W_MIX), 0)
        groups = []
        for g in range(cps * chunk // SUBLANES):
            acc = bcast(g * SUBLANES + s - 1, SUBLANES)
            for p in range(1, SUBLANES // (2 * s)):
                acc = jnp.where(sub >= p * 2 * s, bcast(g * SUBLANES + p * 2 * s + s - 1, SUBLANES), acc)
            groups.append(acc)
        return jnp.concatenate(groups, axis=0)

    q = q_ref[...].astype(F32)
    row = lax.broadcasted_iota(jnp.int32, (cps * chunk, W_MIX), 0)
    x_ref[levels] = q.astype(BF16)
    x_ref[levels + 1] = kk.astype(BF16)
    for lv in range(levels):
        s = chunk >> (lv + 1)
        qside = ((row // s) % 2) == 1
        if s == 1:
            x_ref[lv] = jnp.where(qside, q * f, kk).astype(BF16)
        else:
            d = bcum - boundary_rows(s)
            x_ref[lv] = (jnp.where(qside, q, kk) * jnp.exp(jnp.where(qside, d, -d))).astype(BF16)
    b_last = jnp.concatenate([bcast(rc.stop - 1, chunk) for rc in chunks], axis=0)
    x_ref[levels + 2] = (q * jnp.exp(bcum)).astype(BF16)
    x_ref[levels + 3] = (kk * jnp.exp(b_last - bcum)).astype(BF16)
    decays = [jnp.exp(e_ref[rc.stop - 1:rc.stop, :]) for rc in chunks]
    nw = nw_ref[...]

    heads = [slice(h * HEAD_DIM, (h + 1) * HEAD_DIM) for h in range(N_HEADS)]
    amats = {}
    for c, rc in enumerate(chunks):
        for h, hs in enumerate(heads):
            amat = mask_ref[levels] * _dot_nt(x_ref[levels, rc, hs], x_ref[levels + 1, rc, hs])
            for lv in range(levels):
                x = x_ref[lv, rc, hs]
                amat = amat + mask_ref[lv] * _dot_nt(x, x)
            amats[c, h] = amat.astype(BF16)
    states = [st_ref[h] for h in range(N_HEADS)]
    outs = {}
    for c, rc in enumerate(chunks):
        for h, hs in enumerate(heads):
            outs[c, h] = (_dot(amats[c, h], i_ref[rc, hs])
                          + _dot_nt(x_ref[levels + 2, rc, hs], states[h].astype(BF16)))
        for h, hs in enumerate(heads):
            states[h] = states[h] * decays[c][:, hs] + _dot_tn(i_ref[rc, hs], x_ref[levels + 3, rc, hs])
    for h in range(N_HEADS):
        st_ref[h] = states[h]
    for c, rc in enumerate(chunks):
        for h, hs in enumerate(heads):
            y_ref[rc, hs] = (_silu(g_ref[rc, hs].astype(F32)) * _rms(outs[c, h], nw)).astype(y_ref.dtype)

    @pl.when(t == pl.num_programs(1) - 1)
    def _():
        for h in range(N_HEADS):
            sout_ref[h] = st_ref[h].T


def hgrn2(proj, fg, log_lb, log1m_lb, norm_w, s0, s0_layer, batch, seq, chunk, cps):
    n_steps = seq // (chunk * cps)
    rows = chunk * cps
    tril_np, mask_np, levels = _hgrn_level_masks(chunk)
    tril = jnp.asarray(tril_np, dtype=BF16)
    masks = jnp.asarray(mask_np)

    def col(c):
        return pl.BlockSpec((rows, W_MIX), lambda b, t: (b * n_steps + t, c))

    def const(shape):
        return pl.BlockSpec(shape, lambda b, t: (0,) * len(shape))

    kernel = functools.partial(_hgrn_kernel, chunk=chunk, levels=levels, cps=cps)
    y, s_out = pl.pallas_call(
        kernel,
        grid=(batch, n_steps),
        in_specs=[
            col(COL_QG), col(0), col(COL_IG), col(COL_GG),
            const((1, W_MIX)), const((1, W_MIX)),
            const((chunk, chunk)),
            const((levels + 1, chunk, chunk)),
            const((1, HEAD_DIM)),
            _state_spec(s0_layer),
        ],
        out_specs=[
            pl.BlockSpec((rows, W_MIX), lambda b, t: (b * n_steps + t, 0)),
            pl.BlockSpec((None, N_HEADS, HEAD_DIM, HEAD_DIM), lambda b, t: (b, 0, 0, 0)),
        ],
        out_shape=[
            jax.ShapeDtypeStruct((batch * seq, W_MIX), BF16),
            jax.ShapeDtypeStruct((batch, N_HEADS, HEAD_DIM, HEAD_DIM), F32),
        ],
        scratch_shapes=[
            pltpu.VMEM((N_HEADS, HEAD_DIM, HEAD_DIM), F32),
            pltpu.VMEM((rows, W_MIX), F32),
            pltpu.VMEM((levels + 4, rows, W_MIX), BF16),
        ],
        compiler_params=_params("parallel", "arbitrary"),
        name="hgrn2",
    )(proj, fg, proj, proj, log_lb, log1m_lb, tril, masks, norm_w.reshape(1, HEAD_DIM), s0)
    return y, s_out


def _merge_kernel(x_ref, gpre_ref, ya_ref, yr_ref, yg_ref, wg0_ref, wg1_ref, wg2_ref, wb_ref, wo_ref,
                  gpost_ref, o_ref, h_ref, *, tm, rs):
    j = pl.program_id(1)

    @pl.when(j == 0)
    def _():
        for rows in _row_blocks(tm):
            h_ref[rows, :] = _rms(x_ref[rows, :], gpre_ref[...]).astype(BF16)
        o_ref[...] = jnp.zeros_like(o_ref)

    for r in range(tm // rs):
        rows = slice(r * rs, (r + 1) * rs)
        h = h_ref[rows, :]
        merged = None
        for n, (y_ref, wg_ref) in enumerate(((ya_ref, wg0_ref), (yr_ref, wg1_ref), (yg_ref, wg2_ref))):
            gate = jax.nn.sigmoid(_dot(h, wg_ref[...]))
            term = gate * _dot(y_ref[rows, :], wb_ref[n])
            merged = term if merged is None else merged + term
        o_ref[rows, :] += _dot(merged.astype(BF16), wo_ref[...])

    @pl.when(j == pl.num_programs(1) - 1)
    def _():
        for rows in _row_blocks(tm):
            o_ref[rows, :] = x_ref[rows, :] + _rms(o_ref[rows, :], gpost_ref[...])


def merge(x, g_pre, ya, yr, yg, w_gate, w_branch, w_out, g_post, layer, tm, tn):
    rows, d = x.shape
    nj = d // tn

    def gate_spec(n):
        return pl.BlockSpec((None, d, tn), lambda i, j: (layer, 0, n * nj + j))

    kernel = functools.partial(_merge_kernel, tm=tm, rs=min(tm, ROW_SUB))
    single = pl.Buffered(1)
    return pl.pallas_call(
        kernel,
        grid=(rows // tm, nj),
        in_specs=[
            pl.BlockSpec((tm, d), lambda i, j: (i, 0)),
            pl.BlockSpec((1, d), lambda i, j: (0, 0)),
            pl.BlockSpec((tm, W_MIX), lambda i, j: (i, 0), pipeline_mode=single),
            pl.BlockSpec((tm, W_MIX), lambda i, j: (i, 0), pipeline_mode=single),
            pl.BlockSpec((tm, W_MIX), lambda i, j: (i, 0), pipeline_mode=single),
            gate_spec(0), gate_spec(1), gate_spec(2),
            pl.BlockSpec((None, N_BRANCH, W_MIX, tn), lambda i, j: (layer, 0, 0, j)),
            pl.BlockSpec((None, tn, d), lambda i, j: (layer, j, 0)),
            pl.BlockSpec((1, d), lambda i, j: (0, 0)),
        ],
        out_specs=pl.BlockSpec((tm, d), lambda i, j: (i, 0)),
        out_shape=jax.ShapeDtypeStruct((rows, d), F32),
        scratch_shapes=[pltpu.VMEM((tm, d), BF16)],
        compiler_params=_params("parallel", "arbitrary"),
        name="merge",
    )(x, g_pre, ya, yr, yg, w_gate, w_gate, w_gate, w_branch, w_out, g_post)


HALO = BF16_ROWS


def _ffn_kernel(x_ref, xprev_ref, cst_ref, gpre_ref, wup_ref, wval_ref, cw_ref, cb_ref, wdn_ref, gpost_ref,
                o_ref, uplast_ref, h_ref, up_ref, *, tm, seq):
    i = pl.program_id(0)
    j = pl.program_id(1)
    seqs_per_tile = max(tm // seq, 1)
    tiles_per_seq = max(seq // tm, 1)

    @pl.when(j == 0)
    def _():
        g = gpre_ref[...]
        h_ref[0:HALO, :] = _rms(xprev_ref[...], g).astype(BF16)
        for rows in _row_blocks(tm):
            h_ref[HALO + rows.start:HALO + rows.stop, :] = _rms(x_ref[rows, :], g).astype(BF16)
        o_ref[...] = jnp.zeros_like(o_ref)

    cw = cw_ref[...]
    wdn = wdn_ref[...].astype(BF16)
    rs = min(tm, ROW_SUB)
    n_sub = tm // rs

    def up_block(r):
        lo = 0 if r == 0 else HALO + r * rs
        hi = HALO + (r + 1) * rs
        up_ref[lo:hi, :] = _dot(h_ref[lo:hi, :], wup_ref[...])

    up_block(0)
    if seqs_per_tile == 1:
        first = i % tiles_per_seq == 0
        halo_rows = slice(HALO - (CONV_W - 1), HALO)
        up_ref[halo_rows, :] = jnp.where(first, cst_ref[0], up_ref[halo_rows, :])
    for r in range(n_sub):
        if r + 1 < n_sub:
            up_block(r + 1)
        r0 = HALO + r * rs
        tap2 = up_ref[r0 - 2:r0 - 2 + rs, :]
        tap1 = up_ref[r0 - 1:r0 - 1 + rs, :]
        tap0 = up_ref[r0:r0 + rs, :]
        if seqs_per_tile > 1:
            tf = tap0.shape[1]
            st = cst_ref[...]
            st0 = jnp.broadcast_to(st[:, 0:1, :], (seqs_per_tile, seq, tf)).reshape(tm, tf)
            st1 = jnp.broadcast_to(st[:, 1:2, :], (seqs_per_tile, seq, tf)).reshape(tm, tf)
            pos = lax.broadcasted_iota(jnp.int32, (tm, tf), 0) % seq
            tap1 = jnp.where(pos == 0, st1, tap1)
            tap2 = jnp.where(pos == 0, st0, jnp.where(pos == 1, st1, tap2))
        conv = cb_ref[...] + tap2 * cw[0:1, :] + tap1 * cw[1:2, :] + tap0 * cw[2:3, :]
        val = _dot(h_ref[r0:r0 + rs, :], wval_ref[...])
        act = 0.5 * conv * (1.0 + lax.erf(conv * math.sqrt(0.5))) * val
        o_ref[r * rs:(r + 1) * rs, :] += _dot(act.astype(BF16), wdn)

    keep = uplast_ref.shape[0]
    uplast_ref[...] = up_ref[HALO + tm - keep:HALO + tm, :]

    @pl.when(j == pl.num_programs(1) - 1)
    def _():
        for rows in _row_blocks(tm):
            o_ref[rows, :] = x_ref[rows, :] + _rms(o_ref[rows, :], gpost_ref[...])


def ffn(x, conv_state, cs_layer, g_pre, w_up, w_val, conv_w, conv_b, w_down, g_post, layer, batch, seq, tm, tf):
    rows, d = x.shape
    dff = w_up.shape[2]
    n_tiles = rows // tm
    seqs_per_tile = max(tm // seq, 1)
    tiles_per_seq = max(seq // tm, 1)
    assert rows == batch * seq and (tm % seq == 0 or seq % tm == 0)
    keep = tm if seqs_per_tile > 1 else SUBLANES
    halo_blocks = tm // HALO
    kernel = functools.partial(_ffn_kernel, tm=tm, seq=seq)
    out, uplast = pl.pallas_call(
        kernel,
        grid=(n_tiles, dff // tf),
        in_specs=[
            pl.BlockSpec((tm, d), lambda i, j: (i, 0)),
            pl.BlockSpec((HALO, d), lambda i, j: (jnp.maximum(i * halo_blocks - 1, 0), 0)),
            pl.BlockSpec((None, seqs_per_tile, CONV_W - 1, tf),
                         lambda i, j: (cs_layer, i * seqs_per_tile // tiles_per_seq, 0, j)),
            pl.BlockSpec((1, d), lambda i, j: (0, 0)),
            pl.BlockSpec((None, d, tf), lambda i, j: (layer, 0, j)),
            pl.BlockSpec((None, d, tf), lambda i, j: (layer, 0, j)),
            pl.BlockSpec((CONV_W, tf), lambda i, j: (0, j)),
            pl.BlockSpec((1, tf), lambda i, j: (0, j)),
            pl.BlockSpec((None, tf, d), lambda i, j: (layer, j, 0)),
            pl.BlockSpec((1, d), lambda i, j: (0, 0)),
        ],
        out_specs=[
            pl.BlockSpec((tm, d), lambda i, j: (i, 0)),
            pl.BlockSpec((keep, tf), lambda i, j: (i, j)),
        ],
        out_shape=[
            jax.ShapeDtypeStruct((rows, d), F32),
            jax.ShapeDtypeStruct((n_tiles * keep, dff), F32),
        ],
        scratch_shapes=[
            pltpu.VMEM((HALO + tm, d), BF16),
            pltpu.VMEM((HALO + tm, tf), F32),
        ],
        compiler_params=_params("parallel", "arbitrary"),
        name="ffn",
    )(x, x, conv_state, g_pre, w_up, w_val, conv_w, conv_b, w_down, g_post)
    new_conv = uplast.reshape(batch, -1, dff)[:, -(CONV_W - 1):, :]
    return out, new_conv


def _layer(x, batch, seq, pos0, layer, caches, cache_layer, vecs, mats, tiles):
    k_past, v_past, ret_s0, hgrn_s0, conv_s0 = caches
    (g_mix_pre, g_mix_post, g_ffn_pre, g_ffn_post, rel_bias, ret_norm_w, log_lb, log1m_lb, hgrn_norm_w,
     conv_w, conv_b) = vecs
    w_in, w_gate, w_branch, w_out, w_up, w_val, w_down = mats
    rows = batch * seq
    tn = W_MIX
    f32_cols = (COL_FG,) if k_past is None else (COL_FG, COL_KA, COL_VA)
    proj, fg, *kv_f32 = norm_proj(x, g_mix_pre, w_in, layer, tiles["tm_proj"], tn, f32_cols=f32_cols)

    chunk = min(CHUNK, seq)
    past = ATT_PAST_ROWS
    hps = tiles["att_heads"]
    hb = N_HEADS // hps
    if k_past is None:
        keep = min(past, seq)
        (kv_new,) = norm_proj(x, g_mix_pre, w_in, layer, keep, tn, row_block0=seq // keep - 1, n_row_blocks=1,
                              col_block0=COL_KA, n_col_blocks=2, out_dtype=F32)
        group, sub = tiles["att_group"], tiles["att_sub"]
        gp = group // past
        tables = _attention_tables(rel_bias, sub, chunk, past, pos0, group)
        y_a = attention(
            proj, lambda h, g: (g, h),
            proj, (), lambda h, g: (jnp.maximum(g * gp - 1, 0), hb + h),
            proj, lambda h, g: (g, hb + h),
            proj, lambda h, g: (jnp.maximum(g * gp - 1, 0), 2 * hb + h),
            proj, lambda h, g: (g, 2 * hb + h),
            tables, seq // group, group, sub, past, hps)
    else:
        keep = seq
        kv_new = jnp.concatenate(kv_f32, axis=1)
        group = sub = seq
        tables = _attention_tables(rel_bias, sub, chunk, past, pos0, group)
        n_l = k_past.shape[0]
        assert hps == N_HEADS
        kp = k_past.reshape(n_l, batch, past * N_HEADS, HEAD_DIM)
        vp = v_past.reshape(n_l, batch, past * N_HEADS, HEAD_DIM)
        y_a = attention(
            proj, lambda h, g: (g, h),
            kp, (None, None), lambda h, g: (cache_layer, g, 0, 0),
            proj, lambda h, g: (g, hb + h),
            vp, lambda h, g: (cache_layer, g, 0, 0),
            proj, lambda h, g: (g, 2 * hb + h),
            tables, batch, group, sub, past, hps, past_by_row=True)

    y_r, ret_s = retention(proj, ret_norm_w, ret_s0, cache_layer, batch, seq, tiles["ret_chunk"], tiles["ret_cps"], pos0)
    y_g, hgrn_s = hgrn2(proj, fg, log_lb, log1m_lb, hgrn_norm_w, hgrn_s0, cache_layer, batch, seq, chunk,
                          tiles["hgrn_cps"])

    x = merge(x, g_mix_pre, y_a, y_r, y_g, w_gate, w_branch, w_out, g_mix_post, layer,
              tiles["tm_merge"], tiles["tn_merge"])
    x, new_conv = ffn(x, conv_s0, cache_layer, g_ffn_pre, w_up, w_val, conv_w, conv_b, w_down, g_ffn_post, layer,
                      batch, seq, tiles["tm_ffn"], tiles["tf_ffn"])
    k_new = kv_new[:, :W_MIX].reshape(batch, keep, N_HEADS, HEAD_DIM)
    v_new = kv_new[:, W_MIX:].reshape(batch, keep, N_HEADS, HEAD_DIM)
    return x, k_new, v_new, ret_s, hgrn_s, new_conv


PROMPT_TILES = dict(tm_proj=1024, att_group=1024, att_sub=128, att_heads=4, ret_chunk=256, ret_cps=2, hgrn_cps=8, tm_merge=1024, tn_merge=256, tm_ffn=1024, tf_ffn=512)


def _sample_tiles(rows, seq):
    return dict(tm_proj=rows, att_heads=N_HEADS, ret_chunk=seq, ret_cps=1, hgrn_cps=1, tm_merge=rows, tn_merge=256, tm_ffn=rows, tf_ffn=512)


def kernel(x_prompt, x_sample, cache_attn_k, cache_attn_v, state_ret, state_hgrn, state_ffn_conv,
           norm_mix_pre, norm_mix_post, norm_ffn_pre, norm_ffn_post, w_in, attn_rel_bias, ret_norm_w,
           hgrn_lower_bound, hgrn_norm_w, w_gate, w_branch, w_out, ffn_w_up, ffn_w_val, ffn_conv_w,
           ffn_conv_b, ffn_w_down):
    depth = w_in.shape[0]
    bp, tp, d = x_prompt.shape
    bs, ts, _ = x_sample.shape

    lb_cum = jnp.cumsum(jax.nn.softmax(hgrn_lower_bound.astype(F32), axis=0), axis=0)
    lb_layers = lb_cum - lb_cum[0:1]
    log_lb = jnp.log(lb_layers)
    log1m_lb = jnp.log1p(-lb_layers)

    bf = lambda w: w.astype(BF16)
    mats = (w_in, bf(w_gate), bf(w_branch), bf(w_out), bf(ffn_w_up), bf(ffn_w_val), ffn_w_down)

    zs = jnp.zeros((1, bp, N_HEADS, HEAD_DIM, HEAD_DIM), F32)
    zc = jnp.zeros((1, bp, CONV_W - 1, D_FF), F32)
    caches_p = (None, None, zs, zs, zc)
    caches_s = (cache_attn_k, cache_attn_v, state_ret, state_hgrn, state_ffn_conv)

    xp = x_prompt.reshape(bp * tp, d)
    xs = x_sample.reshape(bs * ts, d)
    outs_p = [[] for _ in range(5)]
    outs_s = [[] for _ in range(5)]
    for l in range(depth):
        vecs = (norm_mix_pre[l][None], norm_mix_post[l][None], norm_ffn_pre[l][None], norm_ffn_post[l][None],
                attn_rel_bias[l], ret_norm_w[l], log_lb[l][None], log1m_lb[l][None], hgrn_norm_w[l],
                ffn_conv_w[l], ffn_conv_b[l][None])
        res_p = _layer(xp, bp, tp, 0, l, caches_p, 0, vecs, mats, PROMPT_TILES)
        res_s = _layer(xs, bs, ts, PAST_LEN, l, caches_s, l, vecs, mats, _sample_tiles(bs * ts, ts))
        xp, xs = res_p[0], res_s[0]
        for lst, val in zip(outs_p, res_p[1:]):
            lst.append(val)
        for lst, val in zip(outs_s, res_s[1:]):
            lst.append(val)

    return (xp.reshape(bp, tp, d), xs.reshape(bs, ts, d),
            *(jnp.stack(o) for o in outs_p), *(jnp.stack(o) for o in outs_s))
```
